```python
import math
import jax, jax.numpy as jnp
from jax import lax
import numpy as np

D_MODEL = 1024
BATCH = 32
SEQ = 256
DEPTH = 4
DEC_BATCH = 8
DEC_SEQ = 4096
PAST_LEN = 256

GRID_W = 64
N_EVEN = (DEPTH + 1) // 2
N_ODD = DEPTH // 2
A_WIDTH = D_MODEL // 2
A_GROUPS = 4
A_CH = A_WIDTH // A_GROUPS
CHUNK = 128
B_WIDTH = D_MODEL // 2
HYENA_ORDER = 2
FILTER_BANDS = 16
FILTER_EMB = 2 * FILTER_BANDS + 1
FILTER_HIDDEN = 64
IN_EVEN = 2 * A_WIDTH + (HYENA_ORDER + 1) * B_WIDTH
MIX_EVEN = A_WIDTH + B_WIDTH
MLA_HEADS = 8
Q_RANK = D_MODEL // 2
KV_RANK = D_MODEL // 4
NOPE_DIM = 128
ROPE_DIM = 64
V_DIM = 128
QK_DIM = NOPE_DIM + ROPE_DIM
ROPE_BASE = 10000.0
Q_BLOCK = 128
D_FF = ((8 * D_MODEL // 3 + 127) // 128) * 128
EPS = 1e-6

kernel_name = 'hybrid_diffusion_chunkmlp_hyena_mla_step'


def rmsnorm(x, g):
    xf = x.astype(jnp.float32)
    y = xf * lax.rsqrt(jnp.mean(xf * xf, axis=-1, keepdims=True) + EPS)
    return (y * g.astype(jnp.float32)).astype(x.dtype)


def dwconv3(x, w, b):
    xp = jnp.pad(x, ((0, 0), (1, 1), (0, 0)))
    return xp[:, :-2] * w[0] + xp[:, 1:-1] * w[1] + xp[:, 2:] * w[2] + b


def adaln(cond, w, b):
    m = jax.nn.silu(cond) @ w + b
    return jnp.split(m[:, None, :], 6, axis=-1)


def axial_rope(x, seq_len):
    rows = seq_len // GRID_W
    row = jnp.repeat(jnp.arange(rows), GRID_W)
    col = jnp.tile(jnp.arange(GRID_W), rows)
    half = ROPE_DIM // 2
    inv = 1.0 / (ROPE_BASE ** (jnp.arange(0, half, 2, dtype=jnp.float32) / half))

    def rot(xa, pos):
        ang = pos.astype(jnp.float32)[:, None] * inv[None]
        cos = jnp.cos(ang)[None, :, None, :]
        sin = jnp.sin(ang)[None, :, None, :]
        x1, x2 = jnp.split(xa.astype(jnp.float32), 2, axis=-1)
        return jnp.concatenate([x1 * cos - x2 * sin, x2 * cos + x1 * sin], axis=-1)

    xr, xc = jnp.split(x, 2, axis=-1)
    return jnp.concatenate([rot(xr, row), rot(xc, col)], axis=-1).astype(x.dtype)


def rope_heads(x, seq_len):
    return jnp.concatenate([x[..., :NOPE_DIM], axial_rope(x[..., NOPE_DIM:], seq_len)], axis=-1)


def hyena_filters(L, w1, b1, w2, b2, w3, freq, decay):
    f32 = jnp.float32
    t = jnp.arange(L, dtype=f32)
    tn = t / L
    bands = jnp.arange(1, FILTER_BANDS + 1, dtype=f32)
    ang = (2.0 * math.pi) * tn[:, None] * bands[None]
    z = jnp.concatenate([tn[:, None], jnp.sin(ang), jnp.cos(ang)], axis=-1)
    fr = freq.astype(f32)
    h = jnp.sin(fr * (z @ w1.astype(f32) + b1.astype(f32)))
    h = jnp.sin(fr * (h @ w2.astype(f32) + b2.astype(f32)))
    h = (h @ w3.astype(f32)).reshape(L, HYENA_ORDER, B_WIDTH)
    dist = jnp.abs(t - L // 2) / L
    h = h * jnp.exp(-jnp.abs(decay.astype(f32))[None] * dist[:, None, None])
    return h / (jnp.sum(jnp.abs(h), axis=0, keepdims=True) + EPS)


def long_conv(u, h, d):
    L = u.shape[1]
    n = 2 * L
    y = jnp.fft.irfft(jnp.fft.rfft(u, n=n, axis=1) * jnp.fft.rfft(h, n=n, axis=0)[None], n=n, axis=1)
    return y[:, L // 2: L // 2 + L] + u * d.astype(jnp.float32)


def even_mixer(h, P, i):
    B, L, _ = h.shape
    p = h @ P['mix_w_in'][i]
    u, v = jnp.split(jax.nn.gelu(p[..., :2 * A_WIDTH]), 2, axis=-1)
    v = v.reshape(B, L // CHUNK, CHUNK, A_GROUPS, A_CH)
    s = jnp.einsum('gpq,bnqgc->bnpgc', P['sgu_w'][i], v) + P['sgu_b'][i].T[None, None, :, :, None]
    a_out = u * s.reshape(B, L, A_WIDTH)
    xb = dwconv3(p[..., 2 * A_WIDTH:], P['hy_conv_w'][i], P['hy_conv_b'][i]).astype(jnp.float32)
    vb, x1, x2 = jnp.split(xb, 3, axis=-1)
    filt = hyena_filters(L, P['hy_f_w1'][i], P['hy_f_b1'][i], P['hy_f_w2'][i], P['hy_f_b2'][i],
                         P['hy_f_w3'][i], P['hy_f_freq'][i], P['hy_decay'][i])
    d = P['hy_d'][i]
    z = x1 * long_conv(vb, filt[:, 0], d[0])
    z = x2 * long_conv(z, filt[:, 1], d[1])
    return jnp.concatenate([a_out, z.astype(h.dtype)], axis=-1) @ P['mix_w_out'][i]


def mla_queries(h, P, j):
    B, L, _ = h.shape
    q = (rmsnorm(h @ P['mla_w_dq'][j], P['mla_q_norm'][j]) @ P['mla_w_uq'][j]).reshape(B, L, MLA_HEADS, QK_DIM)
    return rmsnorm(q, P['mla_q_head_norm'][j])


def mla_compress(h, P, j):
    dkv = h @ P['mla_w_dkv'][j]
    return rmsnorm(dkv[..., :KV_RANK], P['mla_kv_norm'][j]), dkv[..., KV_RANK:]


def mla_expand(ckv, krope, P, j):
    B, L, _ = ckv.shape
    kv = (ckv @ P['mla_w_ukv'][j]).reshape(B, L, MLA_HEADS, NOPE_DIM + V_DIM)
    k = jnp.concatenate([kv[..., :NOPE_DIM],
                         jnp.broadcast_to(krope[:, :, None, :], (B, L, MLA_HEADS, ROPE_DIM))], axis=-1)
    return rmsnorm(k, P['mla_k_head_norm'][j]), kv[..., NOPE_DIM:]


def block_attention(q, k, v):
    B, Lq, H, Dk = q.shape
    nb = Lq // Q_BLOCK
    scale = 1.0 / math.sqrt(Dk)
    qb = q.reshape(B, nb, Q_BLOCK, H, Dk).transpose(1, 0, 2, 3, 4)

    def one(qblk):
        s = jnp.einsum('bqhd,bkhd->bhqk', qblk, k).astype(jnp.float32) * scale
        pr = jax.nn.softmax(s, axis=-1)
        return jnp.einsum('bhqk,bkhd->bqhd', pr.astype(v.dtype), v)

    out = lax.map(one, qb)
    return out.transpose(1, 0, 2, 3, 4).reshape(B, Lq, H * v.shape[-1])


def conv_ffn(h, P, l):
    up = dwconv3(h @ P['ffn_w_up'][l], P['ffn_conv_w'][l], P['ffn_conv_b'][l])
    g, u = jnp.split(up, 2, axis=-1)
    return (jax.nn.silu(g) * u) @ P['ffn_w_down'][l]


def trunk(x, cond, P, ctx_cache):
    B, L, _ = x.shape
    latent = ctx_cache is not None
    new_ckv = []
    new_kr = []
    for l in range(DEPTH):
        sh1, sc1, g1, sh2, sc2, g2 = adaln(cond, P['ada_w'][l], P['ada_b'][l])
        h = rmsnorm(x, P['norm_g'][l, 0]) * (1.0 + sc1) + sh1
        if l % 2 == 0:
            out = even_mixer(h, P, l // 2)
        else:
            j = l // 2
            q = mla_queries(h, P, j)
            ckv, kr = mla_compress(h, P, j)
            k, v = mla_expand(ckv, kr, P, j)
            if latent:
                q = rope_heads(q, L)
                k = rope_heads(k, L)
                kc, vc = mla_expand(ctx_cache[0][:, j], ctx_cache[1][:, j], P, j)
                k = jnp.concatenate([k, kc], axis=1)
                v = jnp.concatenate([v, vc], axis=1)
            else:
                new_ckv.append(ckv)
                new_kr.append(kr)
            out = block_attention(q, k, v) @ P['mla_w_o'][j]
        x = x + g1 * out
        h = rmsnorm(x, P['norm_g'][l, 1]) * (1.0 + sc2) + sh2
        x = x + g2 * conv_ffn(h, P, l)
    return x, new_ckv, new_kr


def setup_inputs(seed: int = 0) -> dict:
    key = jax.random.key(seed)
    ks = iter(jax.random.split(key, 64))

    def nrm(shape, scale):
        return jax.random.normal(next(ks), shape, jnp.float32) * scale

    def gain(shape):
        return 1.0 + nrm(shape, 0.05)

    D = D_MODEL
    return {
        'x_prompt': nrm((BATCH, SEQ, D), 1.0),
        'x_sample': nrm((DEC_BATCH, DEC_SEQ, D), 1.0),
        'cache_ckv': nrm((DEC_BATCH, N_ODD, PAST_LEN, KV_RANK), 1.0),
        'cache_krope': nrm((DEC_BATCH, N_ODD, PAST_LEN, ROPE_DIM), 1.0),
        'c': nrm((DEC_BATCH, D), 1.0),
        'c_ctx': nrm((D,), 1.0),
        'ada_w': nrm((DEPTH, D, 6 * D), 0.5 * D ** -0.5),
        'ada_b': nrm((DEPTH, 6 * D), 0.02),
        'norm_g': gain((DEPTH, 2, D)),
        'mix_w_in': nrm((N_EVEN, D, IN_EVEN), D ** -0.5),
        'sgu_w': nrm((N_EVEN, A_GROUPS, CHUNK, CHUNK), CHUNK ** -0.5),
        'sgu_b': nrm((N_EVEN, A_GROUPS, CHUNK), 0.02),
        'hy_conv_w': nrm((N_EVEN, 3, (HYENA_ORDER + 1) * B_WIDTH), 3 ** -0.5),
        'hy_conv_b': nrm((N_EVEN, (HYENA_ORDER + 1) * B_WIDTH), 0.02),
        'hy_f_w1': nrm((N_EVEN, FILTER_EMB, FILTER_HIDDEN), FILTER_EMB ** -0.5),
        'hy_f_b1': nrm((N_EVEN, FILTER_HIDDEN), 0.02),
        'hy_f_w2': nrm((N_EVEN, FILTER_HIDDEN, FILTER_HIDDEN), FILTER_HIDDEN ** -0.5),
        'hy_f_b2': nrm((N_EVEN, FILTER_HIDDEN), 0.02),
        'hy_f_w3': nrm((N_EVEN, FILTER_HIDDEN, HYENA_ORDER * B_WIDTH), FILTER_HIDDEN ** -0.5),
        'hy_f_freq': 1.0 + nrm((N_EVEN, FILTER_HIDDEN), 0.1),
        'hy_decay': jax.random.uniform(next(ks), (N_EVEN, HYENA_ORDER, B_WIDTH), jnp.float32, 3.0, 15.0),
        'hy_d': nrm((N_EVEN, HYENA_ORDER, B_WIDTH), 0.1),
        'mix_w_out': nrm((N_EVEN, MIX_EVEN, D), MIX_EVEN ** -0.5),
        'mla_w_dq': nrm((N_ODD, D, Q_RANK), D ** -0.5),
        'mla_q_norm': gain((N_ODD, Q_RANK)),
        'mla_w_uq': nrm((N_ODD, Q_RANK, MLA_HEADS * QK_DIM), Q_RANK ** -0.5),
        'mla_w_dkv': nrm((N_ODD, D, KV_RANK + ROPE_DIM), D ** -0.5),
        'mla_kv_norm': gain((N_ODD, KV_RANK)),
        'mla_w_ukv': nrm((N_ODD, KV_RANK, MLA_HEADS * (NOPE_DIM + V_DIM)), KV_RANK ** -0.5),
        'mla_q_head_norm': gain((N_ODD, QK_DIM)),
        'mla_k_head_norm': gain((N_ODD, QK_DIM)),
        'mla_w_o': nrm((N_ODD, MLA_HEADS * V_DIM, D), (MLA_HEADS * V_DIM) ** -0.5),
        'ffn_w_up': nrm((DEPTH, D, 2 * D_FF), D ** -0.5),
        'ffn_conv_w': nrm((DEPTH, 3, 2 * D_FF), 3 ** -0.5),
        'ffn_conv_b': nrm((DEPTH, 2 * D_FF), 0.02),
        'ffn_w_down': nrm((DEPTH, D_FF, D), D_FF ** -0.5),
    }


def reference(x_prompt, x_sample, cache_ckv, cache_krope, c, c_ctx,
              ada_w, ada_b, norm_g,
              mix_w_in, sgu_w, sgu_b, hy_conv_w, hy_conv_b,
              hy_f_w1, hy_f_b1, hy_f_w2, hy_f_b2, hy_f_w3, hy_f_freq, hy_decay, hy_d, mix_w_out,
              mla_w_dq, mla_q_norm, mla_w_uq, mla_w_dkv, mla_kv_norm, mla_w_ukv,
              mla_q_head_norm, mla_k_head_norm, mla_w_o,
              ffn_w_up, ffn_conv_w, ffn_conv_b, ffn_w_down):
    P = {
        'ada_w': ada_w, 'ada_b': ada_b, 'norm_g': norm_g,
        'mix_w_in': mix_w_in, 'sgu_w': sgu_w, 'sgu_b': sgu_b,
        'hy_conv_w': hy_conv_w, 'hy_conv_b': hy_conv_b,
        'hy_f_w1': hy_f_w1, 'hy_f_b1': hy_f_b1, 'hy_f_w2': hy_f_w2, 'hy_f_b2': hy_f_b2,
        'hy_f_w3': hy_f_w3, 'hy_f_freq': hy_f_freq, 'hy_decay': hy_decay, 'hy_d': hy_d,
        'mix_w_out': mix_w_out,
        'mla_w_dq': mla_w_dq, 'mla_q_norm': mla_q_norm, 'mla_w_uq': mla_w_uq,
        'mla_w_dkv': mla_w_dkv, 'mla_kv_norm': mla_kv_norm, 'mla_w_ukv': mla_w_ukv,
        'mla_q_head_norm': mla_q_head_norm, 'mla_k_head_norm': mla_k_head_norm, 'mla_w_o': mla_w_o,
        'ffn_w_up': ffn_w_up, 'ffn_conv_w': ffn_conv_w, 'ffn_conv_b': ffn_conv_b, 'ffn_w_down': ffn_w_down,
    }
    y_prompt, ckv_list, kr_list = trunk(x_prompt, c_ctx[None, :], P, None)
    new_cache_ckv = jnp.stack(ckv_list, axis=1)
    new_cache_krope = jnp.stack(kr_list, axis=1)
    y_sample, _, _ = trunk(x_sample, c, P, (cache_ckv, cache_krope))
    return (y_prompt, y_sample, new_cache_ckv, new_cache_krope)
```

```python
import functools
import math

import jax
import jax.numpy as jnp
from jax import lax
from jax.experimental import pallas as pl
from jax.experimental.pallas import tpu as pltpu

f32 = jnp.float32
bf16 = jnp.bfloat16

EPS = 1e-6
GRID_W = 64
CHUNK = 128
A_GROUPS = 4
HEADS = 8
NOPE = 128
ROPE = 64
QK = NOPE + ROPE
HEAD_PAD = 256
V_DIM = 128
KV_RANK = 256
ROPE_BASE = 10000.0
FILTER_BANDS = 16
LANES = 128
HALO = 8
TOEP = 256
VMEM_LIMIT = 56 * 1024 * 1024
HIGHEST = lax.Precision.HIGHEST


def _cparams(*sem):
    return pltpu.CompilerParams(dimension_semantics=sem, vmem_limit_bytes=VMEM_LIMIT)


def _norm_mod(x, g, sc, sh):
    y = x * lax.rsqrt(jnp.mean(x * x, axis=-1, keepdims=True) + EPS)
    return (y * g) * (1.0 + sc) + sh


def _halo_ext(x_ref, xp_ref, xn_ref, g, sc, sh, first, last):
    hp = jnp.where(first, 0.0, _norm_mod(xp_ref[...], g, sc, sh))
    hn = jnp.where(last, 0.0, _norm_mod(xn_ref[...], g, sc, sh))
    h = _norm_mod(x_ref[...], g, sc, sh)
    return jnp.concatenate([hp, h, hn], axis=0).astype(bf16)


def _dwconv3(y, w, b, tm):
    return (y[HALO - 1:HALO - 1 + tm] * w[0:1] + y[HALO:HALO + tm] * w[1:2]
            + y[HALO + 1:HALO + 1 + tm] * w[2:3] + b)


def _gelu_tanh(x):
    return 0.5 * x * (1.0 + jnp.tanh(math.sqrt(2.0 / math.pi) * (x + 0.044715 * (x * x * x))))


def _row_specs(tm, d, t_rows):
    r = tm // HALO
    nb = t_rows // HALO
    return (pl.BlockSpec((tm, d), lambda i, *_: (i, 0)),
            pl.BlockSpec((HALO, d), lambda i, *_: (jnp.maximum(i * r - 1, 0), 0)),
            pl.BlockSpec((HALO, d), lambda i, *_: (jnp.minimum((i + 1) * r, nb - 1), 0)))


def _ada_kernel(c_ref, w_ref, b_ref, o_ref):
    c = c_ref[...]
    s = c * jax.nn.sigmoid(c)
    o_ref[0] = jnp.dot(s, w_ref[0], precision=HIGHEST, preferred_element_type=f32) + b_ref[0]


def _ada(cond, ada_w, ada_b):
    depth, d, n = ada_w.shape
    tn = n // 4
    rows = cond.shape[0]
    return pl.pallas_call(
        _ada_kernel,
        out_shape=jax.ShapeDtypeStruct((depth, rows, n), f32),
        grid=(depth, n // tn),
        in_specs=[pl.BlockSpec((rows, d), lambda l, j: (0, 0)),
                  pl.BlockSpec((1, d, tn), lambda l, j: (l, 0, j)),
                  pl.BlockSpec((1, 1, tn), lambda l, j: (l, 0, j))],
        out_specs=pl.BlockSpec((1, rows, tn), lambda l, j: (l, 0, j)),
        compiler_params=_cparams("arbitrary", "arbitrary"),
        name="ada_mod",
    )(cond, ada_w, ada_b.reshape(depth, 1, n))


def _even_front_kernel(x_ref, xp_ref, xn_ref, mod_ref, g_ref, win_ref, sw_ref, sb_ref, cw_ref, cb_ref,
                       a_ref, xbt_ref, *, tm, tpb, a_width):
    i = pl.program_id(0)
    mod = mod_ref[0]
    h = _halo_ext(x_ref, xp_ref, xn_ref, g_ref[...], mod[1:2], mod[0:1], (i % tpb) == 0, (i % tpb) == tpb - 1)
    p = jnp.dot(h, win_ref[...], preferred_element_type=f32)
    u = _gelu_tanh(p[HALO:HALO + tm, 0:a_width])
    v = _gelu_tanh(p[HALO:HALO + tm, a_width:2 * a_width]).astype(bf16)
    nch = tm // CHUNK
    a_ch = a_width // A_GROUPS
    for gi in range(A_GROUPS):
        cols = slice(gi * a_ch, (gi + 1) * a_ch)
        rhs = jnp.concatenate([v[n * CHUNK:(n + 1) * CHUNK, cols] for n in range(nch)], axis=1)
        s = jnp.dot(sw_ref[gi], rhs, preferred_element_type=f32)
        for n in range(nch):
            rows = slice(n * CHUNK, (n + 1) * CHUNK)
            sn = s[:, n * a_ch:(n + 1) * a_ch] + sb_ref[gi]
            a_ref[rows, cols] = (u[rows, cols] * sn).astype(bf16)
    xb = _dwconv3(p[:, 2 * a_width:], cw_ref[...], cb_ref[...], tm)
    xbt_ref[0] = xb.T


def _even_front(x, mod, g, w_in, sgu_w, sgu_b, conv_w, conv_b, *, nb, seq, tm):
    t_rows, d = x.shape
    tpb = seq // tm
    n_in = w_in.shape[1]
    a_width = A_GROUPS * sgu_w.shape[1]
    nxb = n_in - 2 * a_width
    mrows = seq if mod.shape[0] > 1 else t_rows
    kern = functools.partial(_even_front_kernel, tm=tm, tpb=tpb, a_width=a_width)
    full = lambda shape: pl.BlockSpec(shape, lambda i: (0,) * len(shape))
    return pl.pallas_call(
        kern,
        out_shape=(jax.ShapeDtypeStruct((t_rows, a_width), bf16),
                   jax.ShapeDtypeStruct((nb, nxb, seq), f32)),
        grid=(t_rows // tm,),
        in_specs=[*_row_specs(tm, d, t_rows),
                  pl.BlockSpec((1, 6, d), lambda i: ((i * tm) // mrows, 0, 0)),
                  full((1, d)), full(w_in.shape), full(sgu_w.shape), full(sgu_b.shape),
                  full(conv_w.shape), full(conv_b.shape)],
        out_specs=(pl.BlockSpec((tm, a_width), lambda i: (i, 0)),
                   pl.BlockSpec((1, nxb, tm), lambda i: (i // tpb, 0, i % tpb))),
        compiler_params=_cparams("arbitrary"),
        name="even_front",
    )(x, x, x, mod, g, w_in, sgu_w, sgu_b, conv_w, conv_b)


def _filter_kernel(w1t_ref, b1_ref, w2t_ref, b2_ref, fr_ref, w3t_ref, dec_ref, o_ref, *, seq):
    nfeat = w1t_ref.shape[1]
    t = lax.broadcasted_iota(jnp.int32, (nfeat, seq), 1).astype(f32)
    fi = lax.broadcasted_iota(jnp.int32, (nfeat, seq), 0)
    tn = t / seq
    band = jnp.where(fi <= FILTER_BANDS, fi, fi - FILTER_BANDS).astype(f32)
    ang = ((2.0 * math.pi) * tn) * band
    z = jnp.where(fi == 0, tn, jnp.where(fi <= FILTER_BANDS, jnp.sin(ang), jnp.cos(ang)))
    z = jnp.where(fi <= 2 * FILTER_BANDS, z, 0.0)
    fr = fr_ref[...]
    h = jnp.sin(fr * (jnp.dot(w1t_ref[...], z, precision=HIGHEST, preferred_element_type=f32) + b1_ref[...]))
    h = jnp.sin(fr * (jnp.dot(w2t_ref[...], h, precision=HIGHEST, preferred_element_type=f32) + b2_ref[...]))
    h = jnp.dot(w3t_ref[...], h, precision=HIGHEST, preferred_element_type=f32)
    tl = lax.broadcasted_iota(jnp.int32, (1, seq), 1).astype(f32)
    dist = jnp.abs(tl - (seq // 2)) / seq
    h = h * jnp.exp(-jnp.abs(dec_ref[...]) * dist)
    o_ref[...] = h / (jnp.sum(jnp.abs(h), axis=1, keepdims=True) + EPS)


def _hyena_filters_t(seq, w1, b1, w2, b2, w3, freq, decay):
    nfeat = 40
    hid = w1.shape[1]
    w1t = jnp.zeros((hid, nfeat), f32).at[:, :w1.shape[0]].set(w1.T)
    rows = w3.shape[1]
    rb = 256
    kern = functools.partial(_filter_kernel, seq=seq)
    full = lambda shape: pl.BlockSpec(shape, lambda i: (0,) * len(shape))
    return pl.pallas_call(
        kern,
        out_shape=jax.ShapeDtypeStruct((rows, seq), f32),
        grid=(rows // rb,),
        in_specs=[full((hid, nfeat)), full((hid, 1)), full((hid, hid)), full((hid, 1)), full((hid, 1)),
                  pl.BlockSpec((rb, hid), lambda i: (i, 0)), pl.BlockSpec((rb, 1), lambda i: (i, 0))],
        out_specs=pl.BlockSpec((rb, seq), lambda i: (i, 0)),
        compiler_params=_cparams("arbitrary"),
        name="hyena_filter",
    )(w1t, b1.reshape(hid, 1), w2.T, b2.reshape(hid, 1), freq.reshape(hid, 1), w3.T, decay.reshape(rows, 1))


def _toeplitz_bank(hrow, bank_ref, seq):
    nseg = seq // LANES
    q = lax.broadcasted_iota(jnp.int32, (LANES, LANES), 0)
    p = lax.broadcasted_iota(jnp.int32, (LANES, LANES), 1)
    upper = p >= q
    zero = jnp.zeros((LANES, LANES), f32)
    bank_ref[0] = zero.astype(bf16)
    bank_ref[nseg + 2] = zero.astype(bf16)
    prev = zero
    for k in range(nseg + 1):
        if k < nseg:
            seg = jnp.broadcast_to(hrow[:, k * LANES:(k + 1) * LANES], (LANES, LANES))
            cur = pltpu.roll(seg, 0, 1, stride=1, stride_axis=0)
        else:
            cur = zero
        bank_ref[k + 1] = jnp.where(upper, cur, prev).astype(bf16)
        prev = cur


def _long_conv(u, hrow, bank_ref, upad_ref, *, nb, seq):
    nseg = seq // LANES
    nblk = seq // TOEP
    half = seq // 2
    _toeplitz_bank(hrow, bank_ref, seq)
    upad_ref[:, 0:half] = jnp.zeros((nb, half), f32)
    upad_ref[:, half:half + seq] = u
    upad_ref[:, half + seq:2 * seq] = jnp.zeros((nb, half), f32)
    acc = None
    for kt in range(nblk + 1):
        lhs = jnp.concatenate([upad_ref[:, (i + kt) * TOEP:(i + kt + 1) * TOEP] for i in range(nblk)], axis=0)
        k0 = nseg - 2 * kt
        top = jnp.concatenate([bank_ref[k0 + 1], bank_ref[k0 + 2]], axis=1)
        bot = jnp.concatenate([bank_ref[k0], bank_ref[k0 + 1]], axis=1)
        w = jnp.concatenate([top, bot], axis=0)
        part = jnp.dot(lhs.astype(bf16), w, preferred_element_type=f32)
        acc = part if acc is None else acc + part
    return jnp.concatenate([acc[i * nb:(i + 1) * nb] for i in range(nblk)], axis=1)


def _hyena_kernel(d_ref, vb_ref, x1_ref, x2_ref, h0_ref, h1_ref, z_ref,
                  u_scr, a_scr, b_scr, o_scr, bank_ref, upad_ref, *, nb, seq, cg, nchan):
    gidx = pl.program_id(0)
    u_scr[...] = jnp.swapaxes(vb_ref[...], 0, 1)
    a_scr[...] = jnp.swapaxes(x1_ref[...], 0, 1)
    b_scr[...] = jnp.swapaxes(x2_ref[...], 0, 1)

    def body(c, carry):
        ch = gidx * cg + c
        u = u_scr[c]
        y = _long_conv(u, h0_ref[pl.ds(c, 1), :], bank_ref, upad_ref, nb=nb, seq=seq)
        z = a_scr[c] * (y + u * d_ref[ch])
        y = _long_conv(z, h1_ref[pl.ds(c, 1), :], bank_ref, upad_ref, nb=nb, seq=seq)
        o_scr[c] = b_scr[c] * (y + z * d_ref[nchan + ch])
        return carry

    lax.fori_loop(0, cg, body, 0)
    z_ref[...] = jnp.swapaxes(o_scr[...], 0, 1)


def _hyena(xbt, filt_t, d_flat, *, nb, seq):
    nchan = xbt.shape[1] // 3
    cg = 8
    ng = nchan // cg
    kern = functools.partial(_hyena_kernel, nb=nb, seq=seq, cg=cg, nchan=nchan)
    act = lambda off: pl.BlockSpec((nb, cg, seq), lambda i: (0, off + i, 0))
    return pl.pallas_call(
        kern,
        out_shape=jax.ShapeDtypeStruct((nb, nchan, seq), f32),
        grid=(ng,),
        in_specs=[pl.BlockSpec(memory_space=pltpu.SMEM),
                  act(0), act(ng), act(2 * ng),
                  pl.BlockSpec((cg, seq), lambda i: (i, 0)),
                  pl.BlockSpec((cg, seq), lambda i: (ng + i, 0))],
        out_specs=pl.BlockSpec((nb, cg, seq), lambda i: (0, i, 0)),
        scratch_shapes=[pltpu.VMEM((cg, nb, seq), f32), pltpu.VMEM((cg, nb, seq), f32),
                        pltpu.VMEM((cg, nb, seq), f32), pltpu.VMEM((cg, nb, seq), f32),
                        pltpu.VMEM((seq // LANES + 3, LANES, LANES), bf16),
                        pltpu.VMEM((nb, 2 * seq), f32)],
        compiler_params=_cparams("arbitrary"),
        name="hyena_conv",
    )(d_flat, xbt, xbt, xbt, filt_t, filt_t)


def _even_out_kernel(x_ref, a_ref, zt_ref, mod_ref, wa_ref, wz_ref, o_ref):
    out = jnp.dot(a_ref[...], wa_ref[...], preferred_element_type=f32)
    out = out + lax.dot_general(zt_ref[0].astype(bf16), wz_ref[...], (((0,), (0,)), ((), ())),
                                preferred_element_type=f32)
    o_ref[...] = x_ref[...] + mod_ref[0][2:3] * out


def _even_out(x, a, zt, mod, w_a, w_z, *, seq, tm):
    t_rows, d = x.shape
    tpb = seq // tm
    mrows = seq if mod.shape[0] > 1 else t_rows
    full = lambda shape: pl.BlockSpec(shape, lambda i: (0,) * len(shape))
    return pl.pallas_call(
        _even_out_kernel,
        out_shape=jax.ShapeDtypeStruct((t_rows, d), f32),
        grid=(t_rows // tm,),
        in_specs=[pl.BlockSpec((tm, d), lambda i: (i, 0)),
                  pl.BlockSpec((tm, a.shape[1]), lambda i: (i, 0)),
                  pl.BlockSpec((1, zt.shape[1], tm), lambda i: (i // tpb, 0, i % tpb)),
                  pl.BlockSpec((1, 6, d), lambda i: ((i * tm) // mrows, 0, 0)),
                  full(w_a.shape), full(w_z.shape)],
        out_specs=pl.BlockSpec((tm, d), lambda i: (i, 0)),
        compiler_params=_cparams("arbitrary"),
        name="even_out",
    )(x, a, zt, mod, w_a, w_z)


def _rope128(x, cos, sin):
    lane = lax.broadcasted_iota(jnp.int32, x.shape, 1)
    swapped = jnp.where((lane & 16) == 0, pltpu.roll(x, LANES - 16, 1), pltpu.roll(x, 16, 1))
    return x * cos + swapped * sin


def _expand_kv(ckv, kr_pad, wuk_ref, wuv_ref, kg_ref, cos, sin, k_ref, v_ref):
    cb = ckv.astype(bf16)
    kn = jnp.dot(cb, wuk_ref[...], preferred_element_type=f32)
    v = jnp.dot(cb, wuv_ref[...], preferred_element_type=f32)
    v_ref[...] = v.astype(bf16)
    kg = kg_ref[...]
    g_nope, g_rope = kg[:, 0:NOPE], kg[:, NOPE:2 * NOPE]
    kr_ss = jnp.sum(kr_pad * kr_pad, axis=-1, keepdims=True)
    krg = kr_pad * g_rope
    if cos is not None:
        krg = _rope128(krg, cos, sin)
    for hd in range(HEADS):
        kh = kn[:, hd * NOPE:(hd + 1) * NOPE]
        r = lax.rsqrt((jnp.sum(kh * kh, axis=-1, keepdims=True) + kr_ss) / QK + EPS)
        k_ref[:, hd * HEAD_PAD:hd * HEAD_PAD + NOPE] = (kh * r * g_nope).astype(bf16)
        k_ref[:, hd * HEAD_PAD + NOPE:(hd + 1) * HEAD_PAD] = (krg * r).astype(bf16)


def _mla_front_kernel(*refs, rope, emit_cache):
    (x_ref, mod_ref, g_ref, wdq_ref, qn_ref, wuq_ref, qg_ref, wdkv_ref, kvn_ref, wuk_ref, wuv_ref, kg_ref) = refs[:12]
    refs = refs[12:]
    if rope:
        cos_ref, sin_ref = refs[:2]
        refs = refs[2:]
        cos, sin = cos_ref[...], sin_ref[...]
    else:
        cos = sin = None
    q_ref, k_ref, v_ref = refs[:3]
    mod = mod_ref[0]
    h = _norm_mod(x_ref[...], g_ref[...], mod[1:2], mod[0:1]).astype(bf16)
    ql = jnp.dot(h, wdq_ref[...], preferred_element_type=f32)
    ql = ql * lax.rsqrt(jnp.mean(ql * ql, axis=-1, keepdims=True) + EPS) * qn_ref[...]
    q = jnp.dot(ql.astype(bf16), wuq_ref[...], preferred_element_type=f32)
    qg = qg_ref[...]
    for hd in range(HEADS):
        qh = q[:, hd * HEAD_PAD:(hd + 1) * HEAD_PAD]
        r = lax.rsqrt(jnp.sum(qh * qh, axis=-1, keepdims=True) / QK + EPS)
        qh = qh * r * qg
        if rope:
            q_ref[:, hd * HEAD_PAD:hd * HEAD_PAD + NOPE] = qh[:, 0:NOPE].astype(bf16)
            q_ref[:, hd * HEAD_PAD + NOPE:(hd + 1) * HEAD_PAD] = _rope128(qh[:, NOPE:], cos, sin).astype(bf16)
        else:
            q_ref[:, hd * HEAD_PAD:(hd + 1) * HEAD_PAD] = qh.astype(bf16)
    dkv = jnp.dot(h, wdkv_ref[...], preferred_element_type=f32)
    c_raw = dkv[:, 0:KV_RANK]
    ckv = c_raw * lax.rsqrt(jnp.mean(c_raw * c_raw, axis=-1, keepdims=True) + EPS) * kvn_ref[...]
    kr_pad = dkv[:, KV_RANK:]
    if emit_cache:
        ckv_ref, kr_ref = refs[3:5]
        ckv_ref[...] = ckv
        kr_ref[...] = kr_pad[:, 0:ROPE]
    _expand_kv(ckv, kr_pad, wuk_ref, wuv_ref, kg_ref, cos, sin, k_ref, v_ref)


def _mla_front(x, mod, g, w, tabs, *, nb, seq, tm, emit_cache):
    t_rows, d = x.shape
    tpb = seq // tm
    rope = tabs is not None
    mrows = seq if mod.shape[0] > 1 else t_rows
    kern = functools.partial(_mla_front_kernel, rope=rope, emit_cache=emit_cache)
    full = lambda a: pl.BlockSpec(a.shape, lambda i: (0,) * a.ndim)
    wnames = ("w_dq", "q_norm", "w_uq", "q_gain", "w_dkv", "kv_norm", "w_uk", "w_uv", "k_gain")
    ins = [x, mod, g] + [w[n] for n in wnames]
    in_specs = [pl.BlockSpec((tm, d), lambda i: (i, 0)),
                pl.BlockSpec((1, 6, d), lambda i: ((i * tm) // mrows, 0, 0)),
                full(g)] + [full(w[n]) for n in wnames]
    if rope:
        ins += list(tabs)
        in_specs += [pl.BlockSpec((tm, LANES), lambda i: (i % tpb, 0))] * 2
    hp = HEADS * HEAD_PAD
    out_shape = [jax.ShapeDtypeStruct((t_rows, hp), bf16), jax.ShapeDtypeStruct((t_rows, hp), bf16),
                 jax.ShapeDtypeStruct((t_rows, HEADS * V_DIM), bf16)]
    out_specs = [pl.BlockSpec((tm, hp), lambda i: (i, 0)), pl.BlockSpec((tm, hp), lambda i: (i, 0)),
                 pl.BlockSpec((tm, HEADS * V_DIM), lambda i: (i, 0))]
    if emit_cache:
        out_shape += [jax.ShapeDtypeStruct((t_rows, KV_RANK), f32), jax.ShapeDtypeStruct((t_rows, ROPE), f32)]
        out_specs += [pl.BlockSpec((tm, KV_RANK), lambda i: (i, 0)), pl.BlockSpec((tm, ROPE), lambda i: (i, 0))]
    return pl.pallas_call(
        kern, out_shape=tuple(out_shape), grid=(t_rows // tm,), in_specs=in_specs, out_specs=tuple(out_specs),
        compiler_params=_cparams("arbitrary"), name="mla_front",
    )(*ins)


def _cache_expand_kernel(ckv_ref, kr_ref, wuk_ref, wuv_ref, kg_ref, k_ref, v_ref):
    _expand_kv(ckv_ref[...], kr_ref[...], wuk_ref, wuv_ref, kg_ref, None, None, k_ref, v_ref)


def _cache_expand(ckv, kr_pad, w, *, nb, seq):
    t_rows = ckv.shape[0]
    tm = seq
    full = lambda a: pl.BlockSpec(a.shape, lambda i: (0,) * a.ndim)
    hp = HEADS * HEAD_PAD
    return pl.pallas_call(
        _cache_expand_kernel,
        out_shape=(jax.ShapeDtypeStruct((t_rows, hp), bf16), jax.ShapeDtypeStruct((t_rows, HEADS * V_DIM), bf16)),
        grid=(t_rows // tm,),
        in_specs=[pl.BlockSpec((tm, KV_RANK), lambda i: (i, 0)), pl.BlockSpec((tm, LANES), lambda i: (i, 0)),
                  full(w["w_uk"]), full(w["w_uv"]), full(w["k_gain"])],
        out_specs=(pl.BlockSpec((tm, hp), lambda i: (i, 0)),
                   pl.BlockSpec((tm, HEADS * V_DIM), lambda i: (i, 0))),
        compiler_params=_cparams("arbitrary"), name="mla_cache_expand",
    )(ckv, kr_pad, w["w_uk"], w["w_uv"], w["k_gain"])


def _attn_kernel(*refs, tk, n_own, n_cache):
    if n_cache:
        q_ref, k_ref, v_ref, kc_ref, vc_ref, o_ref, s_scr = refs
    else:
        q_ref, k_ref, v_ref, o_ref, s_scr = refs
    q = q_ref[...]
    tq = q.shape[0]
    nt = (((1,), (1,)), ((), ()))
    tn = (((0,), (0,)), ((), ()))
    c_exp = (1.0 / math.sqrt(QK)) * math.log2(math.e)

    def scores(j, m):
        off = pl.multiple_of(j * tk, tk)
        s = lax.dot_general(k_ref[pl.ds(off, tk), :], q, nt, preferred_element_type=f32)
        s_scr[pl.ds(off, tk), :] = s
        return jnp.maximum(m, jnp.max(s, axis=0, keepdims=True))

    m = lax.fori_loop(0, n_own, scores, jnp.full((1, tq), -jnp.inf, f32))
    for j in range(n_cache):
        s = lax.dot_general(kc_ref[j * tk:(j + 1) * tk, :], q, nt, preferred_element_type=f32)
        s_scr[(n_own + j) * tk:(n_own + j + 1) * tk, :] = s
        m = jnp.maximum(m, jnp.max(s, axis=0, keepdims=True))

    def probs(off):
        p = jnp.exp2((s_scr[pl.ds(off, tk), :] - m) * c_exp)
        return p.astype(bf16), jnp.sum(p, axis=0, keepdims=True)

    def weighted(j, carry):
        acc, l = carry
        off = pl.multiple_of(j * tk, tk)
        p, ps = probs(off)
        acc = acc + lax.dot_general(v_ref[pl.ds(off, tk), :], p, tn, preferred_element_type=f32)
        return acc, l + ps

    acc, l = lax.fori_loop(0, n_own, weighted, (jnp.zeros((V_DIM, tq), f32), jnp.zeros((1, tq), f32)))
    for j in range(n_cache):
        p, ps = probs((n_own + j) * tk)
        acc = acc + lax.dot_general(vc_ref[j * tk:(j + 1) * tk, :], p, tn, preferred_element_type=f32)
        l = l + ps
    o_ref[0] = (acc / l).astype(bf16)


def _attention(q, k, v, kc, vc, *, nb, seq, tq, tk):
    n_own = seq // tk
    past = 0 if kc is None else kc.shape[0] // nb
    n_cache = past // tk
    nq = seq // tq
    kern = functools.partial(_attn_kernel, tk=tk, n_own=n_own, n_cache=n_cache)
    ins = [q, k, v]
    in_specs = [pl.BlockSpec((tq, HEAD_PAD), lambda b, h, i: (b * nq + i, h)),
                pl.BlockSpec((seq, HEAD_PAD), lambda b, h, i: (b, h)),
                pl.BlockSpec((seq, V_DIM), lambda b, h, i: (b, h))]
    if n_cache:
        ins += [kc, vc]
        in_specs += [pl.BlockSpec((past, HEAD_PAD), lambda b, h, i: (b, h)),
                     pl.BlockSpec((past, V_DIM), lambda b, h, i: (b, h))]
    return pl.pallas_call(
        kern,
        out_shape=jax.ShapeDtypeStruct((nb, HEADS * V_DIM, seq), bf16),
        grid=(nb, HEADS, nq),
        in_specs=in_specs,
        out_specs=pl.BlockSpec((1, V_DIM, tq), lambda b, h, i: (b, h, i)),
        scratch_shapes=[pltpu.VMEM((seq + past, tq), f32)],
        compiler_params=_cparams("arbitrary", "arbitrary", "arbitrary"),
        name="mla_attention",
    )(*ins)


def _attn_out_kernel(x_ref, at_ref, mod_ref, wo_ref, o_ref):
    out = lax.dot_general(at_ref[0], wo_ref[...], (((0,), (0,)), ((), ())), preferred_element_type=f32)
    o_ref[...] = x_ref[...] + mod_ref[0][2:3] * out


def _attn_out(x, at, mod, w_o, *, seq, tm):
    t_rows, d = x.shape
    tpb = seq // tm
    mrows = seq if mod.shape[0] > 1 else t_rows
    return pl.pallas_call(
        _attn_out_kernel,
        out_shape=jax.ShapeDtypeStruct((t_rows, d), f32),
        grid=(t_rows // tm,),
        in_specs=[pl.BlockSpec((tm, d), lambda i: (i, 0)),
                  pl.BlockSpec((1, at.shape[1], tm), lambda i: (i // tpb, 0, i % tpb)),
                  pl.BlockSpec((1, 6, d), lambda i: ((i * tm) // mrows, 0, 0)),
                  pl.BlockSpec(w_o.shape, lambda i: (0, 0))],
        out_specs=pl.BlockSpec((tm, d), lambda i: (i, 0)),
        compiler_params=_cparams("arbitrary"),
        name="mla_out",
    )(x, at, mod, w_o)


def _ffn_kernel(x_ref, xp_ref, xn_ref, mod_ref, g_ref, wg_ref, wu_ref, cwg_ref, cwu_ref, cbg_ref, cbu_ref, wd_ref,
                o_ref, h_scr, acc_scr, *, tm, tpb):
    i = pl.program_id(0)
    j = pl.program_id(1)

    @pl.when(j == 0)
    def _():
        mod = mod_ref[0]
        h_scr[...] = _halo_ext(x_ref, xp_ref, xn_ref, g_ref[...], mod[4:5], mod[3:4],
                               (i % tpb) == 0, (i % tpb) == tpb - 1)
        acc_scr[...] = jnp.zeros_like(acc_scr)

    h = h_scr[...]
    gate = _dwconv3(jnp.dot(h, wg_ref[...], preferred_element_type=f32), cwg_ref[...], cbg_ref[...], tm)
    up = _dwconv3(jnp.dot(h, wu_ref[...], preferred_element_type=f32), cwu_ref[...], cbu_ref[...], tm)
    act = ((gate * jax.nn.sigmoid(gate)) * up).astype(bf16)
    acc_scr[...] += jnp.dot(act, wd_ref[...], preferred_element_type=f32)

    @pl.when(j == pl.num_programs(1) - 1)
    def _():
        o_ref[...] = x_ref[...] + mod_ref[0][5:6] * acc_scr[...]


def _ffn(x, mod, g, w_up, conv_w, conv_b, w_down, *, seq, tm):
    t_rows, d = x.shape
    dff = w_down.shape[0]
    tf = dff // 2
    nj = dff // tf
    tpb = seq // tm
    mrows = seq if mod.shape[0] > 1 else t_rows
    kern = functools.partial(_ffn_kernel, tm=tm, tpb=tpb)
    return pl.pallas_call(
        kern,
        out_shape=jax.ShapeDtypeStruct((t_rows, d), f32),
        grid=(t_rows // tm, nj),
        in_specs=[*_row_specs(tm, d, t_rows),
                  pl.BlockSpec((1, 6, d), lambda i, j: ((i * tm) // mrows, 0, 0)),
                  pl.BlockSpec((1, d), lambda i, j: (0, 0)),
                  pl.BlockSpec((d, tf), lambda i, j: (0, j)),
                  pl.BlockSpec((d, tf), lambda i, j: (0, j + nj)),
                  pl.BlockSpec((3, tf), lambda i, j: (0, j)),
                  pl.BlockSpec((3, tf), lambda i, j: (0, j + nj)),
                  pl.BlockSpec((1, tf), lambda i, j: (0, j)),
                  pl.BlockSpec((1, tf), lambda i, j: (0, j + nj)),
                  pl.BlockSpec((tf, d), lambda i, j: (j, 0))],
        out_specs=pl.BlockSpec((tm, d), lambda i, j: (i, 0)),
        scratch_shapes=[pltpu.VMEM((tm + 2 * HALO, d), bf16), pltpu.VMEM((tm, d), f32)],
        compiler_params=_cparams("arbitrary", "arbitrary"),
        name="conv_ffn",
    )(x, x, x, mod, g, w_up, w_up, conv_w, conv_w, conv_b, conv_b, w_down)


def _rope_tables(seq):
    half = ROPE // 2
    rows = seq // GRID_W
    row = jnp.repeat(jnp.arange(rows), GRID_W)
    col = jnp.tile(jnp.arange(GRID_W), rows)
    inv = 1.0 / (ROPE_BASE ** (jnp.arange(0, half, 2, dtype=f32) / half))
    ar = row.astype(f32)[:, None] * inv[None]
    ac = col.astype(f32)[:, None] * inv[None]
    pad = jnp.zeros((seq, LANES - ROPE), f32)
    cos = jnp.concatenate([jnp.cos(ar), jnp.cos(ar), jnp.cos(ac), jnp.cos(ac), pad], axis=1)
    sin = jnp.concatenate([-jnp.sin(ar), jnp.sin(ar), -jnp.sin(ac), jnp.sin(ac), pad], axis=1)
    return cos, sin


def _pad_heads(a, width):
    lead = a.shape[:-1]
    a = a.reshape(*lead, HEADS, -1)
    a = jnp.pad(a, [(0, 0)] * len(lead) + [(0, 0), (0, width - a.shape[-1])])
    return a.reshape(*lead, HEADS * width)


def _mla_weights(j, mla_w_dq, mla_q_norm, mla_w_uq, mla_w_dkv, mla_kv_norm, mla_w_ukv, mla_q_head_norm,
                 mla_k_head_norm):
    ukv = mla_w_ukv[j].reshape(KV_RANK, HEADS, NOPE + V_DIM)
    kg = mla_k_head_norm[j]
    return {
        "w_dq": mla_w_dq[j].astype(bf16),
        "q_norm": mla_q_norm[j].reshape(1, -1),
        "w_uq": _pad_heads(mla_w_uq[j], HEAD_PAD).astype(bf16),
        "q_gain": jnp.pad(mla_q_head_norm[j], (0, HEAD_PAD - QK)).reshape(1, HEAD_PAD),
        "w_dkv": jnp.pad(mla_w_dkv[j], ((0, 0), (0, LANES - ROPE))).astype(bf16),
        "kv_norm": mla_kv_norm[j].reshape(1, -1),
        "w_uk": ukv[:, :, :NOPE].reshape(KV_RANK, HEADS * NOPE).astype(bf16),
        "w_uv": ukv[:, :, NOPE:].reshape(KV_RANK, HEADS * V_DIM).astype(bf16),
        "k_gain": jnp.pad(kg, (0, 2 * NOPE - QK)).reshape(1, 2 * NOPE),
    }


def kernel(x_prompt, x_sample, cache_ckv, cache_krope, c, c_ctx, ada_w, ada_b, norm_g, mix_w_in, sgu_w, sgu_b, hy_conv_w, hy_conv_b, hy_f_w1, hy_f_b1, hy_f_w2, hy_f_b2, hy_f_w3, hy_f_freq, hy_decay, hy_d, mix_w_out, mla_w_dq, mla_q_norm, mla_w_uq, mla_w_dkv, mla_kv_norm, mla_w_ukv, mla_q_head_norm, mla_k_head_norm, mla_w_o, ffn_w_up, ffn_conv_w, ffn_conv_b, ffn_w_down):
    depth = ada_w.shape[0]
    d = x_prompt.shape[-1]
    nbp, seqp, _ = x_prompt.shape
    nbs, seqs, _ = x_sample.shape
    a_width = A_GROUPS * sgu_w.shape[-1]

    cond = jnp.zeros((16, d), f32).at[0].set(c_ctx).at[1:1 + nbs].set(c)
    mods = _ada(cond, ada_w, ada_b).reshape(depth, 16, 6, d)

    w_in = mix_w_in.astype(bf16)
    sgu_wb = sgu_w.astype(bf16)
    sgu_bb = jnp.broadcast_to(sgu_b[..., None], sgu_b.shape + (sgu_w.shape[-1],))
    w_out = mix_w_out.astype(bf16)
    w_up = ffn_w_up.astype(bf16)
    w_down = ffn_w_down.astype(bf16)
    w_o = mla_w_o.astype(bf16)
    mla_w = [_mla_weights(j, mla_w_dq, mla_q_norm, mla_w_uq, mla_w_dkv, mla_kv_norm, mla_w_ukv,
                          mla_q_head_norm, mla_k_head_norm) for j in range(depth // 2)]

    def trunk(x3, mod_all, cache):
        nb, seq, _ = x3.shape
        x = x3.reshape(nb * seq, d)
        tm = min(seq, 512)
        latent = cache is not None
        tabs = _rope_tables(seq) if latent else None
        new_ckv, new_kr = [], []
        for l in range(depth):
            mod = mod_all[l]
            g1 = norm_g[l, 0].reshape(1, d)
            g2 = norm_g[l, 1].reshape(1, d)
            if l % 2 == 0:
                i = l // 2
                a, xbt = _even_front(x, mod, g1, w_in[i], sgu_wb[i], sgu_bb[i], hy_conv_w[i],
                                     hy_conv_b[i].reshape(1, -1), nb=nb, seq=seq, tm=tm)
                filt = _hyena_filters_t(seq, hy_f_w1[i], hy_f_b1[i], hy_f_w2[i], hy_f_b2[i], hy_f_w3[i],
                                        hy_f_freq[i], hy_decay[i])
                zt = _hyena(xbt, filt, hy_d[i].reshape(-1), nb=nb, seq=seq)
                x = _even_out(x, a, zt, mod, w_out[i, :a_width], w_out[i, a_width:], seq=seq, tm=tm)
            else:
                j = l // 2
                outs = _mla_front(x, mod, g1, mla_w[j], tabs, nb=nb, seq=seq, tm=tm, emit_cache=not latent)
                q, k, v = outs[:3]
                if latent:
                    past = cache[0].shape[2]
                    ckv_c = cache[0][:, j].reshape(nb * past, KV_RANK)
                    kr_c = jnp.pad(cache[1][:, j].reshape(nb * past, ROPE), ((0, 0), (0, LANES - ROPE)))
                    kc, vc = _cache_expand(ckv_c, kr_c, mla_w[j], nb=nb, seq=past)
                else:
                    kc = vc = None
                    new_ckv.append(outs[3].reshape(nb, seq, KV_RANK))
                    new_kr.append(outs[4].reshape(nb, seq, ROPE))
                at = _attention(q, k, v, kc, vc, nb=nb, seq=seq, tq=min(seq, 512), tk=256)
                x = _attn_out(x, at, mod, w_o[j], seq=seq, tm=tm)
            x = _ffn(x, mod, g2, w_up[l], ffn_conv_w[l], ffn_conv_b[l].reshape(1, -1), w_down[l], seq=seq, tm=tm)
        return x.reshape(nb, seq, d), new_ckv, new_kr

    y_prompt, ckv_list, kr_list = trunk(x_prompt, mods[:, 0:1], None)
    y_sample, _, _ = trunk(x_sample, mods[:, 1:1 + nbs], (cache_ckv, cache_krope))
    return (y_prompt, y_sample, jnp.stack(ckv_list, axis=1), jnp.stack(kr_list, axis=1))
```

```python
import functools
import math

import jax
import jax.numpy as jnp
from jax import lax
from jax.experimental import pallas as pl
from jax.experimental.pallas import tpu as pltpu

f32 = jnp.float32
bf16 = jnp.bfloat16

EPS = 1e-6
GRID_W = 64
CHUNK = 128
A_GROUPS = 4
HEADS = 8
NOPE = 128
ROPE = 64
QK = NOPE + ROPE
HEAD_PAD = 256
V_DIM = 128
KV_RANK = 256
ROPE_BASE = 10000.0
FILTER_BANDS = 16
LANES = 128
HALO = 8
TOEP = 256
VMEM_LIMIT = 56 * 1024 * 1024
HIGHEST = lax.Precision.HIGHEST


def _cparams(*sem):
    return pltpu.CompilerParams(dimension_semantics=sem, vmem_limit_bytes=VMEM_LIMIT)


def _norm_mod(x, g, sc, sh):
    y = x * lax.rsqrt(jnp.mean(x * x, axis=-1, keepdims=True) + EPS)
    return (y * g) * (1.0 + sc) + sh


def _halo_ext(x_ref, xp_ref, xn_ref, g, sc, sh, first, last):
    hp = jnp.where(first, 0.0, _norm_mod(xp_ref[...], g, sc, sh))
    hn = jnp.where(last, 0.0, _norm_mod(xn_ref[...], g, sc, sh))
    h = _norm_mod(x_ref[...], g, sc, sh)
    return jnp.concatenate([hp, h, hn], axis=0).astype(bf16)


def _dwconv3(y, w, b, tm):
    return (y[HALO - 1:HALO - 1 + tm] * w[0:1] + y[HALO:HALO + tm] * w[1:2]
            + y[HALO + 1:HALO + 1 + tm] * w[2:3] + b)


def _gelu_tanh(x):
    return 0.5 * x * (1.0 + jnp.tanh(math.sqrt(2.0 / math.pi) * (x + 0.044715 * (x * x * x))))


def _row_specs(tm, d, t_rows):
    r = tm // HALO
    nb = t_rows // HALO
    return (pl.BlockSpec((tm, d), lambda i, *_: (i, 0)),
            pl.BlockSpec((HALO, d), lambda i, *_: (jnp.maximum(i * r - 1, 0), 0)),
            pl.BlockSpec((HALO, d), lambda i, *_: (jnp.minimum((i + 1) * r, nb - 1), 0)))


def _ada_kernel(c_ref, w_ref, b_ref, o_ref):
    c = c_ref[...]
    s = c * jax.nn.sigmoid(c)
    o_ref[0] = jnp.dot(s, w_ref[0], precision=HIGHEST, preferred_element_type=f32) + b_ref[0]


def _ada(cond, ada_w, ada_b):
    depth, d, n = ada_w.shape
    tn = n // 4
    rows = cond.shape[0]
    return pl.pallas_call(
        _ada_kernel,
        out_shape=jax.ShapeDtypeStruct((depth, rows, n), f32),
        grid=(depth, n // tn),
        in_specs=[pl.BlockSpec((rows, d), lambda l, j: (0, 0)),
                  pl.BlockSpec((1, d, tn), lambda l, j: (l, 0, j)),
                  pl.BlockSpec((1, 1, tn), lambda l, j: (l, 0, j))],
        out_specs=pl.BlockSpec((1, rows, tn), lambda l, j: (l, 0, j)),
        compiler_params=_cparams("arbitrary", "arbitrary"),
        name="ada_mod",
    )(cond, ada_w, ada_b.reshape(depth, 1, n))


def _even_front_kernel(x_ref, xp_ref, xn_ref, mod_ref, g_ref, win_ref, sw_ref, sb_ref, cw_ref, cb_ref,
                       a_ref, xbt_ref, *, tm, tpb, a_width):
    i = pl.program_id(0)
    mod = mod_ref[0]
    h = _halo_ext(x_ref, xp_ref, xn_ref, g_ref[...], mod[1:2], mod[0:1], (i % tpb) == 0, (i % tpb) == tpb - 1)
    p = jnp.dot(h, win_ref[...], preferred_element_type=f32)
    u = _gelu_tanh(p[HALO:HALO + tm, 0:a_width])
    v = _gelu_tanh(p[HALO:HALO + tm, a_width:2 * a_width]).astype(bf16)
    nch = tm // CHUNK
    a_ch = a_width // A_GROUPS
    for gi in range(A_GROUPS):
        cols = slice(gi * a_ch, (gi + 1) * a_ch)
        rhs = jnp.concatenate([v[n * CHUNK:(n + 1) * CHUNK, cols] for n in range(nch)], axis=1)
        s = jnp.dot(sw_ref[gi], rhs, preferred_element_type=f32)
        for n in range(nch):
            rows = slice(n * CHUNK, (n + 1) * CHUNK)
            sn = s[:, n * a_ch:(n + 1) * a_ch] + sb_ref[gi]
            a_ref[rows, cols] = (u[rows, cols] * sn).astype(bf16)
    xb = _dwconv3(p[:, 2 * a_width:], cw_ref[...], cb_ref[...], tm)
    xbt_ref[0] = xb.T


def _even_front(x, mod, g, w_in, sgu_w, sgu_b, conv_w, conv_b, *, nb, seq, tm):
    t_rows, d = x.shape
    tpb = seq // tm
    n_in = w_in.shape[1]
    a_width = A_GROUPS * sgu_w.shape[1]
    nxb = n_in - 2 * a_width
    mrows = seq if mod.shape[0] > 1 else t_rows
    kern = functools.partial(_even_front_kernel, tm=tm, tpb=tpb, a_width=a_width)
    full = lambda shape: pl.BlockSpec(shape, lambda i: (0,) * len(shape))
    return pl.pallas_call(
        kern,
        out_shape=(jax.ShapeDtypeStruct((t_rows, a_width), bf16),
                   jax.ShapeDtypeStruct((nb, nxb, seq), f32)),
        grid=(t_rows // tm,),
        in_specs=[*_row_specs(tm, d, t_rows),
                  pl.BlockSpec((1, 6, d), lambda i: ((i * tm) // mrows, 0, 0)),
                  full((1, d)), full(w_in.shape), full(sgu_w.shape), full(sgu_b.shape),
                  full(conv_w.shape), full(conv_b.shape)],
        out_specs=(pl.BlockSpec((tm, a_width), lambda i: (i, 0)),
                   pl.BlockSpec((1, nxb, tm), lambda i: (i // tpb, 0, i % tpb))),
        compiler_params=_cparams("arbitrary"),
        name="even_front",
    )(x, x, x, mod, g, w_in, sgu_w, sgu_b, conv_w, conv_b)


def _filter_kernel(w1t_ref, b1_ref, w2t_ref, b2_ref, fr_ref, w3t_ref, dec_ref, o_ref, *, seq):
    nfeat = w1t_ref.shape[1]
    t = lax.broadcasted_iota(jnp.int32, (nfeat, seq), 1).astype(f32)
    fi = lax.broadcasted_iota(jnp.int32, (nfeat, seq), 0)
    tn = t / seq
    band = jnp.where(fi <= FILTER_BANDS, fi, fi - FILTER_BANDS).astype(f32)
    ang = ((2.0 * math.pi) * tn) * band
    z = jnp.where(fi == 0, tn, jnp.where(fi <= FILTER_BANDS, jnp.sin(ang), jnp.cos(ang)))
    z = jnp.where(fi <= 2 * FILTER_BANDS, z, 0.0)
    fr = fr_ref[...]
    h = jnp.sin(fr * (jnp.dot(w1t_ref[...], z, precision=HIGHEST, preferred_element_type=f32) + b1_ref[...]))
    h = jnp.sin(fr * (jnp.dot(w2t_ref[...], h, precision=HIGHEST, preferred_element_type=f32) + b2_ref[...]))
    h = jnp.dot(w3t_ref[...], h, precision=HIGHEST, preferred_element_type=f32)
    tl = lax.broadcasted_iota(jnp.int32, (1, seq), 1).astype(f32)
    dist = jnp.abs(tl - (seq // 2)) / seq
    h = h * jnp.exp(-jnp.abs(dec_ref[...]) * dist)
    o_ref[...] = h / (jnp.sum(jnp.abs(h), axis=1, keepdims=True) + EPS)


def _hyena_filters_t(seq, w1, b1, w2, b2, w3, freq, decay):
    nfeat = 40
    hid = w1.shape[1]
    w1t = jnp.zeros((hid, nfeat), f32).at[:, :w1.shape[0]].set(w1.T)
    rows = w3.shape[1]
    rb = 256
    kern = functools.partial(_filter_kernel, seq=seq)
    full = lambda shape: pl.BlockSpec(shape, lambda i: (0,) * len(shape))
    return pl.pallas_call(
        kern,
        out_shape=jax.ShapeDtypeStruct((rows, seq), f32),
        grid=(rows // rb,),
        in_specs=[full((hid, nfeat)), full((hid, 1)), full((hid, hid)), full((hid, 1)), full((hid, 1)),
                  pl.BlockSpec((rb, hid), lambda i: (i, 0)), pl.BlockSpec((rb, 1), lambda i: (i, 0))],
        out_specs=pl.BlockSpec((rb, seq), lambda i: (i, 0)),
        compiler_params=_cparams("arbitrary"),
        name="hyena_filter",
    )(w1t, b1.reshape(hid, 1), w2.T, b2.reshape(hid, 1), freq.reshape(hid, 1), w3.T, decay.reshape(rows, 1))


def _toeplitz_bank(hrow, bank_ref, seq):
    nseg = seq // LANES
    q = lax.broadcasted_iota(jnp.int32, (LANES, LANES), 0)
    p = lax.broadcasted_iota(jnp.int32, (LANES, LANES), 1)
    upper = p >= q
    zero = jnp.zeros((LANES, LANES), f32)
    bank_ref[0] = zero.astype(bf16)
    bank_ref[nseg + 2] = zero.astype(bf16)
    prev = zero
    for k in range(nseg + 1):
        if k < nseg:
            seg = jnp.broadcast_to(hrow[:, k * LANES:(k + 1) * LANES], (LANES, LANES))
            cur = pltpu.roll(seg, 0, 1, stride=1, stride_axis=0)
        else:
            cur = zero
        bank_ref[k + 1] = jnp.where(upper, cur, prev).astype(bf16)
        prev = cur


def _long_conv(u, hrow, bank_ref, upad_ref, *, nb, seq):
    nseg = seq // LANES
    nblk = seq // TOEP
    half = seq // 2
    _toeplitz_bank(hrow, bank_ref, seq)
    upad_ref[:, 0:half] = jnp.zeros((nb, half), f32)
    upad_ref[:, half:half + seq] = u
    upad_ref[:, half + seq:2 * seq] = jnp.zeros((nb, half), f32)
    acc = None
    for kt in range(nblk + 1):
        lhs = jnp.concatenate([upad_ref[:, (i + kt) * TOEP:(i + kt + 1) * TOEP] for i in range(nblk)], axis=0)
        k0 = nseg - 2 * kt
        top = jnp.concatenate([bank_ref[k0 + 1], bank_ref[k0 + 2]], axis=1)
        bot = jnp.concatenate([bank_ref[k0], bank_ref[k0 + 1]], axis=1)
        w = jnp.concatenate([top, bot], axis=0)
        part = jnp.dot(lhs.astype(bf16), w, preferred_element_type=f32)
        acc = part if acc is None else acc + part
    return jnp.concatenate([acc[i * nb:(i + 1) * nb] for i in range(nblk)], axis=1)


def _hyena_kernel(d_ref, vb_ref, x1_ref, x2_ref, h0_ref, h1_ref, z_ref,
                  u_scr, a_scr, b_scr, o_scr, bank_ref, upad_ref, *, nb, seq, cg, nchan):
    gidx = pl.program_id(0)
    u_scr[...] = jnp.swapaxes(vb_ref[...], 0, 1)
    a_scr[...] = jnp.swapaxes(x1_ref[...], 0, 1)
    b_scr[...] = jnp.swapaxes(x2_ref[...], 0, 1)

    def body(c, carry):
        ch = gidx * cg + c
        u = u_scr[c]
        y = _long_conv(u, h0_ref[pl.ds(c, 1), :], bank_ref, upad_ref, nb=nb, seq=seq)
        z = a_scr[c] * (y + u * d_ref[ch])
        y = _long_conv(z, h1_ref[pl.ds(c, 1), :], bank_ref, upad_ref, nb=nb, seq=seq)
        o_scr[c] = b_scr[c] * (y + z * d_ref[nchan + ch])
        return carry

    lax.fori_loop(0, cg, body, 0)
    z_ref[...] = jnp.swapaxes(o_scr[...], 0, 1)


def _hyena(xbt, filt_t, d_flat, *, nb, seq):
    nchan = xbt.shape[1] // 3
    cg = 8
    ng = nchan // cg
    kern = functools.partial(_hyena_kernel, nb=nb, seq=seq, cg=cg, nchan=nchan)
    act = lambda off: pl.BlockSpec((nb, cg, seq), lambda i: (0, off + i, 0))
    return pl.pallas_call(
        kern,
        out_shape=jax.ShapeDtypeStruct((nb, nchan, seq), f32),
        grid=(ng,),
        in_specs=[pl.BlockSpec(memory_space=pltpu.SMEM),
                  act(0), act(ng), act(2 * ng),
                  pl.BlockSpec((cg, seq), lambda i: (i, 0)),
                  pl.BlockSpec((cg, seq), lambda i: (ng + i, 0))],
        out_specs=pl.BlockSpec((nb, cg, seq), lambda i: (0, i, 0)),
        scratch_shapes=[pltpu.VMEM((cg, nb, seq), f32), pltpu.VMEM((cg, nb, seq), f32),
                        pltpu.VMEM((cg, nb, seq), f32), pltpu.VMEM((cg, nb, seq), f32),
                        pltpu.VMEM((seq // LANES + 3, LANES, LANES), bf16),
                        pltpu.VMEM((nb, 2 * seq), f32)],
        compiler_params=_cparams("arbitrary"),
        name="hyena_conv",
    )(d_flat, xbt, xbt, xbt, filt_t, filt_t)


def _even_out_kernel(x_ref, a_ref, zt_ref, mod_ref, wa_ref, wz_ref, o_ref):
    out = jnp.dot(a_ref[...], wa_ref[...], preferred_element_type=f32)
    out = out + lax.dot_general(zt_ref[0].astype(bf16), wz_ref[...], (((0,), (0,)), ((), ())),
                                preferred_element_type=f32)
    o_ref[...] = x_ref[...] + mod_ref[0][2:3] * out


def _even_out(x, a, zt, mod, w_a, w_z, *, seq, tm):
    t_rows, d = x.shape
    tpb = seq // tm
    mrows = seq if mod.shape[0] > 1 else t_rows
    full = lambda shape: pl.BlockSpec(shape, lambda i: (0,) * len(shape))
    return pl.pallas_call(
        _even_out_kernel,
        out_shape=jax.ShapeDtypeStruct((t_rows, d), f32),
        grid=(t_rows // tm,),
        in_specs=[pl.BlockSpec((tm, d), lambda i: (i, 0)),
                  pl.BlockSpec((tm, a.shape[1]), lambda i: (i, 0)),
                  pl.BlockSpec((1, zt.shape[1], tm), lambda i: (i // tpb, 0, i % tpb)),
                  pl.BlockSpec((1, 6, d), lambda i: ((i * tm) // mrows, 0, 0)),
                  full(w_a.shape), full(w_z.shape)],
        out_specs=pl.BlockSpec((tm, d), lambda i: (i, 0)),
        compiler_params=_cparams("arbitrary"),
        name="even_out",
    )(x, a, zt, mod, w_a, w_z)


def _rope128(x, cos, sin):
    lane = lax.broadcasted_iota(jnp.int32, x.shape, 1)
    swapped = jnp.where((lane & 16) == 0, pltpu.roll(x, LANES - 16, 1), pltpu.roll(x, 16, 1))
    return x * cos + swapped * sin


def _expand_kv(ckv, kr_pad, wuk_ref, wuv_ref, kg_ref, cos, sin, k_ref, v_ref):
    cb = ckv.astype(bf16)
    kn = jnp.dot(cb, wuk_ref[...], preferred_element_type=f32)
    v = jnp.dot(cb, wuv_ref[...], preferred_element_type=f32)
    v_ref[...] = v.astype(bf16)
    kg = kg_ref[...]
    g_nope, g_rope = kg[:, 0:NOPE], kg[:, NOPE:2 * NOPE]
    kr_ss = jnp.sum(kr_pad * kr_pad, axis=-1, keepdims=True)
    krg = kr_pad * g_rope
    if cos is not None:
        krg = _rope128(krg, cos, sin)
    for hd in range(HEADS):
        kh = kn[:, hd * NOPE:(hd + 1) * NOPE]
        r = lax.rsqrt((jnp.sum(kh * kh, axis=-1, keepdims=True) + kr_ss) / QK + EPS)
        k_ref[:, hd * HEAD_PAD:hd * HEAD_PAD + NOPE] = (kh * r * g_nope).astype(bf16)
        k_ref[:, hd * HEAD_PAD + NOPE:(hd + 1) * HEAD_PAD] = (krg * r).astype(bf16)


def _mla_front_kernel(*refs, rope, emit_cache):
    (x_ref, mod_ref, g_ref, wdq_ref, qn_ref, wuq_ref, qg_ref, wdkv_ref, kvn_ref, wuk_ref, wuv_ref, kg_ref) = refs[:12]
    refs = refs[12:]
    if rope:
        cos_ref, sin_ref = refs[:2]
        refs = refs[2:]
        cos, sin = cos_ref[...], sin_ref[...]
    else:
        cos = sin = None
    q_ref, k_ref, v_ref = refs[:3]
    mod = mod_ref[0]
    h = _norm_mod(x_ref[...], g_ref[...], mod[1:2], mod[0:1]).astype(bf16)
    ql = jnp.dot(h, wdq_ref[...], preferred_element_type=f32)
    ql = ql * lax.rsqrt(jnp.mean(ql * ql, axis=-1, keepdims=True) + EPS) * qn_ref[...]
    q = jnp.dot(ql.astype(bf16), wuq_ref[...], preferred_element_type=f32)
    qg = qg_ref[...]
    for hd in range(HEADS):
        qh = q[:, hd * HEAD_PAD:(hd + 1) * HEAD_PAD]
        r = lax.rsqrt(jnp.sum(qh * qh, axis=-1, keepdims=True) / QK + EPS)
        qh = qh * r * qg
        if rope:
            q_ref[:, hd * HEAD_PAD:hd * HEAD_PAD + NOPE] = qh[:, 0:NOPE].astype(bf16)
            q_ref[:, hd * HEAD_PAD + NOPE:(hd + 1) * HEAD_PAD] = _rope128(qh[:, NOPE:], cos, sin).astype(bf16)
        else:
            q_ref[:, hd * HEAD_PAD:(hd + 1) * HEAD_PAD] = qh.astype(bf16)
    dkv = jnp.dot(h, wdkv_ref[...], preferred_element_type=f32)
    c_raw = dkv[:, 0:KV_RANK]
    ckv = c_raw * lax.rsqrt(jnp.mean(c_raw * c_raw, axis=-1, keepdims=True) + EPS) * kvn_ref[...]
    kr_pad = dkv[:, KV_RANK:]
    if emit_cache:
        ckv_ref, kr_ref = refs[3:5]
        ckv_ref[...] = ckv
        kr_ref[...] = kr_pad[:, 0:ROPE]
    _expand_kv(ckv, kr_pad, wuk_ref, wuv_ref, kg_ref, cos, sin, k_ref, v_ref)


def _mla_front(x, mod, g, w, tabs, *, nb, seq, tm, emit_cache):
    t_rows, d = x.shape
    tpb = seq // tm
    rope = tabs is not None
    mrows = seq if mod.shape[0] > 1 else t_rows
    kern = functools.partial(_mla_front_kernel, rope=rope, emit_cache=emit_cache)
    full = lambda a: pl.BlockSpec(a.shape, lambda i: (0,) * a.ndim)
    wnames = ("w_dq", "q_norm", "w_uq", "q_gain", "w_dkv", "kv_norm", "w_uk", "w_uv", "k_gain")
    ins = [x, mod, g] + [w[n] for n in wnames]
    in_specs = [pl.BlockSpec((tm, d), lambda i: (i, 0)),
                pl.BlockSpec((1, 6, d), lambda i: ((i * tm) // mrows, 0, 0)),
                full(g)] + [full(w[n]) for n in wnames]
    if rope:
        ins += list(tabs)
        in_specs += [pl.BlockSpec((tm, LANES), lambda i: (i % tpb, 0))] * 2
    hp = HEADS * HEAD_PAD
    out_shape = [jax.ShapeDtypeStruct((t_rows, hp), bf16), jax.ShapeDtypeStruct((t_rows, hp), bf16),
                 jax.ShapeDtypeStruct((t_rows, HEADS * V_DIM), bf16)]
    out_specs = [pl.BlockSpec((tm, hp), lambda i: (i, 0)), pl.BlockSpec((tm, hp), lambda i: (i, 0)),
                 pl.BlockSpec((tm, HEADS * V_DIM), lambda i: (i, 0))]
    if emit_cache:
        out_shape += [jax.ShapeDtypeStruct((t_rows, KV_RANK), f32), jax.ShapeDtypeStruct((t_rows, ROPE), f32)]
        out_specs += [pl.BlockSpec((tm, KV_RANK), lambda i: (i, 0)), pl.BlockSpec((tm, ROPE), lambda i: (i, 0))]
    return pl.pallas_call(
        kern, out_shape=tuple(out_shape), grid=(t_rows // tm,), in_specs=in_specs, out_specs=tuple(out_specs),
        compiler_params=_cparams("arbitrary"), name="mla_front",
    )(*ins)


def _cache_expand_kernel(ckv_ref, kr_ref, wuk_ref, wuv_ref, kg_ref, k_ref, v_ref):
    _expand_kv(ckv_ref[...], kr_ref[...], wuk_ref, wuv_ref, kg_ref, None, None, k_ref, v_ref)


def _cache_expand(ckv, kr_pad, w, *, nb, seq):
    t_rows = ckv.shape[0]
    tm = seq
    full = lambda a: pl.BlockSpec(a.shape, lambda i: (0,) * a.ndim)
    hp = HEADS * HEAD_PAD
    return pl.pallas_call(
        _cache_expand_kernel,
        out_shape=(jax.ShapeDtypeStruct((t_rows, hp), bf16), jax.ShapeDtypeStruct((t_rows, HEADS * V_DIM), bf16)),
        grid=(t_rows // tm,),
        in_specs=[pl.BlockSpec((tm, KV_RANK), lambda i: (i, 0)), pl.BlockSpec((tm, LANES), lambda i: (i, 0)),
                  full(w["w_uk"]), full(w["w_uv"]), full(w["k_gain"])],
        out_specs=(pl.BlockSpec((tm, hp), lambda i: (i, 0)),
                   pl.BlockSpec((tm, HEADS * V_DIM), lambda i: (i, 0))),
        compiler_params=_cparams("arbitrary"), name="mla_cache_expand",
    )(ckv, kr_pad, w["w_uk"], w["w_uv"], w["k_gain"])


def _attn_kernel(*refs, cached):
    if cached:
        q_ref, k_ref, v_ref, kc_ref, vc_ref, o_ref = refs
        parts = ((k_ref, v_ref), (kc_ref, vc_ref))
    else:
        q_ref, k_ref, v_ref, o_ref = refs
        parts = ((k_ref, v_ref),)
    q = q_ref[...]
    nt = (((1,), (1,)), ((), ()))
    tn = (((0,), (0,)), ((), ()))
    c_exp = (1.0 / math.sqrt(QK)) * math.log2(math.e)
    scores = [lax.dot_general(kr[...], q, nt, preferred_element_type=f32) for kr, _ in parts]
    m = functools.reduce(jnp.maximum, [jnp.max(s, axis=0, keepdims=True) for s in scores])
    acc = l = None
    for s, (_, vr) in zip(scores, parts):
        p = jnp.exp2((s - m) * c_exp)
        ps = jnp.sum(p, axis=0, keepdims=True)
        pv = lax.dot_general(vr[...], p.astype(bf16), tn, preferred_element_type=f32)
        acc = pv if acc is None else acc + pv
        l = ps if l is None else l + ps
    o_ref[0] = (acc / l).astype(bf16)


def _attention(q, k, v, kc, vc, *, nb, seq, tq):
    cached = kc is not None
    nq = seq // tq
    kern = functools.partial(_attn_kernel, cached=cached)
    ins = [q, k, v]
    in_specs = [pl.BlockSpec((tq, HEAD_PAD), lambda b, h, i: (b * nq + i, h)),
                pl.BlockSpec((seq, HEAD_PAD), lambda b, h, i: (b, h)),
                pl.BlockSpec((seq, V_DIM), lambda b, h, i: (b, h))]
    if cached:
        past = kc.shape[0] // nb
        ins += [kc, vc]
        in_specs += [pl.BlockSpec((past, HEAD_PAD), lambda b, h, i: (b, h)),
                     pl.BlockSpec((past, V_DIM), lambda b, h, i: (b, h))]
    return pl.pallas_call(
        kern,
        out_shape=jax.ShapeDtypeStruct((nb, HEADS * V_DIM, seq), bf16),
        grid=(nb, HEADS, nq),
        in_specs=in_specs,
        out_specs=pl.BlockSpec((1, V_DIM, tq), lambda b, h, i: (b, h, i)),
        compiler_params=_cparams("arbitrary", "arbitrary", "arbitrary"),
        name="mla_attention",
    )(*ins)


def _attn_out_kernel(x_ref, at_ref, mod_ref, wo_ref, o_ref):
    out = lax.dot_general(at_ref[0], wo_ref[...], (((0,), (0,)), ((), ())), preferred_element_type=f32)
    o_ref[...] = x_ref[...] + mod_ref[0][2:3] * out


def _attn_out(x, at, mod, w_o, *, seq, tm):
    t_rows, d = x.shape
    tpb = seq // tm
    mrows = seq if mod.shape[0] > 1 else t_rows
    return pl.pallas_call(
        _attn_out_kernel,
        out_shape=jax.ShapeDtypeStruct((t_rows, d), f32),
        grid=(t_rows // tm,),
        in_specs=[pl.BlockSpec((tm, d), lambda i: (i, 0)),
                  pl.BlockSpec((1, at.shape[1], tm), lambda i: (i // tpb, 0, i % tpb)),
                  pl.BlockSpec((1, 6, d), lambda i: ((i * tm) // mrows, 0, 0)),
                  pl.BlockSpec(w_o.shape, lambda i: (0, 0))],
        out_specs=pl.BlockSpec((tm, d), lambda i: (i, 0)),
        compiler_params=_cparams("arbitrary"),
        name="mla_out",
    )(x, at, mod, w_o)


def _ffn_kernel(x_ref, xp_ref, xn_ref, mod_ref, g_ref, wup_ref, cw_ref, cb_ref, wd_ref, o_ref, *, tm, tpb, chunks):
    i = pl.program_id(0)
    mod = mod_ref[0]
    h = _halo_ext(x_ref, xp_ref, xn_ref, g_ref[...], mod[4:5], mod[3:4], (i % tpb) == 0, (i % tpb) == tpb - 1)
    dff = wd_ref.shape[0]
    acc = None
    for c0, c1 in chunks:
        halves = []
        for off in (c0, dff + c0):
            cols = slice(off, off + c1 - c0)
            y = jnp.dot(h, wup_ref[:, cols], preferred_element_type=f32)
            halves.append(_dwconv3(y, cw_ref[:, cols], cb_ref[:, cols], tm))
        gate, up = halves
        act = ((gate * jax.nn.sigmoid(gate)) * up).astype(bf16)
        part = jnp.dot(act, wd_ref[c0:c1, :], preferred_element_type=f32)
        acc = part if acc is None else acc + part
    o_ref[...] = x_ref[...] + mod[5:6] * acc


def _ffn(x, mod, g, w_up, conv_w, conv_b, w_down, *, seq, tm):
    t_rows, d = x.shape
    dff = w_down.shape[0]
    step = 6 * TOEP
    chunks = tuple((c, min(c + step, dff)) for c in range(0, dff, step))
    tpb = seq // tm
    mrows = seq if mod.shape[0] > 1 else t_rows
    kern = functools.partial(_ffn_kernel, tm=tm, tpb=tpb, chunks=chunks)
    resident = lambda a: pl.BlockSpec(a.shape, lambda i: (0,) * a.ndim, pipeline_mode=pl.Buffered(1))
    return pl.pallas_call(
        kern,
        out_shape=jax.ShapeDtypeStruct((t_rows, d), f32),
        grid=(t_rows // tm,),
        in_specs=[*_row_specs(tm, d, t_rows),
                  pl.BlockSpec((1, 6, d), lambda i: ((i * tm) // mrows, 0, 0)),
                  resident(g), resident(w_up), resident(conv_w), resident(conv_b), resident(w_down)],
        out_specs=pl.BlockSpec((tm, d), lambda i: (i, 0)),
        compiler_params=_cparams("arbitrary"),
        name="conv_ffn",
    )(x, x, x, mod, g, w_up, conv_w, conv_b, w_down)


def _rope_tables(seq):
    half = ROPE // 2
    rows = seq // GRID_W
    row = jnp.repeat(jnp.arange(rows), GRID_W)
    col = jnp.tile(jnp.arange(GRID_W), rows)
    inv = 1.0 / (ROPE_BASE ** (jnp.arange(0, half, 2, dtype=f32) / half))
    ar = row.astype(f32)[:, None] * inv[None]
    ac = col.astype(f32)[:, None] * inv[None]
    pad = jnp.zeros((seq, LANES - ROPE), f32)
    cos = jnp.concatenate([jnp.cos(ar), jnp.cos(ar), jnp.cos(ac), jnp.cos(ac), pad], axis=1)
    sin = jnp.concatenate([-jnp.sin(ar), jnp.sin(ar), -jnp.sin(ac), jnp.sin(ac), pad], axis=1)
    return cos, sin


def _pad_heads(a, width):
    lead = a.shape[:-1]
    a = a.reshape(*lead, HEADS, -1)
    a = jnp.pad(a, [(0, 0)] * len(lead) + [(0, 0), (0, width - a.shape[-1])])
    return a.reshape(*lead, HEADS * width)


def _mla_weights(j, mla_w_dq, mla_q_norm, mla_w_uq, mla_w_dkv, mla_kv_norm, mla_w_ukv, mla_q_head_norm,
                 mla_k_head_norm):
    ukv = mla_w_ukv[j].reshape(KV_RANK, HEADS, NOPE + V_DIM)
    kg = mla_k_head_norm[j]
    return {
        "w_dq": mla_w_dq[j].astype(bf16),
        "q_norm": mla_q_norm[j].reshape(1, -1),
        "w_uq": _pad_heads(mla_w_uq[j], HEAD_PAD).astype(bf16),
        "q_gain": jnp.pad(mla_q_head_norm[j], (0, HEAD_PAD - QK)).reshape(1, HEAD_PAD),
        "w_dkv": jnp.pad(mla_w_dkv[j], ((0, 0), (0, LANES - ROPE))).astype(bf16),
        "kv_norm": mla_kv_norm[j].reshape(1, -1),
        "w_uk": ukv[:, :, :NOPE].reshape(KV_RANK, HEADS * NOPE).astype(bf16),
        "w_uv": ukv[:, :, NOPE:].reshape(KV_RANK, HEADS * V_DIM).astype(bf16),
        "k_gain": jnp.pad(kg, (0, 2 * NOPE - QK)).reshape(1, 2 * NOPE),
    }


def kernel(x_prompt, x_sample, cache_ckv, cache_krope, c, c_ctx, ada_w, ada_b, norm_g, mix_w_in, sgu_w, sgu_b, hy_conv_w, hy_conv_b, hy_f_w1, hy_f_b1, hy_f_w2, hy_f_b2, hy_f_w3, hy_f_freq, hy_decay, hy_d, mix_w_out, mla_w_dq, mla_q_norm, mla_w_uq, mla_w_dkv, mla_kv_norm, mla_w_ukv, mla_q_head_norm, mla_k_head_norm, mla_w_o, ffn_w_up, ffn_conv_w, ffn_conv_b, ffn_w_down):
    depth = ada_w.shape[0]
    d = x_prompt.shape[-1]
    nbp, seqp, _ = x_prompt.shape
    nbs, seqs, _ = x_sample.shape
    a_width = A_GROUPS * sgu_w.shape[-1]

    cond = jnp.zeros((16, d), f32).at[0].set(c_ctx).at[1:1 + nbs].set(c)
    mods = _ada(cond, ada_w, ada_b).reshape(depth, 16, 6, d)

    w_in = mix_w_in.astype(bf16)
    sgu_wb = sgu_w.astype(bf16)
    sgu_bb = jnp.broadcast_to(sgu_b[..., None], sgu_b.shape + (sgu_w.shape[-1],))
    w_out = mix_w_out.astype(bf16)
    w_up = ffn_w_up.astype(bf16)
    w_down = ffn_w_down.astype(bf16)
    w_o = mla_w_o.astype(bf16)
    mla_w = [_mla_weights(j, mla_w_dq, mla_q_norm, mla_w_uq, mla_w_dkv, mla_kv_norm, mla_w_ukv,
                          mla_q_head_norm, mla_k_head_norm) for j in range(depth // 2)]

    def trunk(x3, mod_all, cache):
        nb, seq, _ = x3.shape
        x = x3.reshape(nb * seq, d)
        tm = min(seq, 512)
        latent = cache is not None
        tabs = _rope_tables(seq) if latent else None
        new_ckv, new_kr = [], []
        for l in range(depth):
            mod = mod_all[l]
            g1 = norm_g[l, 0].reshape(1, d)
            g2 = norm_g[l, 1].reshape(1, d)
            if l % 2 == 0:
                i = l // 2
                a, xbt = _even_front(x, mod, g1, w_in[i], sgu_wb[i], sgu_bb[i], hy_conv_w[i],
                                     hy_conv_b[i].reshape(1, -1), nb=nb, seq=seq, tm=tm)
                filt = _hyena_filters_t(seq, hy_f_w1[i], hy_f_b1[i], hy_f_w2[i], hy_f_b2[i], hy_f_w3[i],
                                        hy_f_freq[i], hy_decay[i])
                zt = _hyena(xbt, filt, hy_d[i].reshape(-1), nb=nb, seq=seq)
                x = _even_out(x, a, zt, mod, w_out[i, :a_width], w_out[i, a_width:], seq=seq, tm=tm)
            else:
                j = l // 2
                outs = _mla_front(x, mod, g1, mla_w[j], tabs, nb=nb, seq=seq, tm=tm, emit_cache=not latent)
                q, k, v = outs[:3]
                if latent:
                    past = cache[0].shape[2]
                    ckv_c = cache[0][:, j].reshape(nb * past, KV_RANK)
                    kr_c = jnp.pad(cache[1][:, j].reshape(nb * past, ROPE), ((0, 0), (0, LANES - ROPE)))
                    kc, vc = _cache_expand(ckv_c, kr_c, mla_w[j], nb=nb, seq=past)
                else:
                    kc = vc = None
                    new_ckv.append(outs[3].reshape(nb, seq, KV_RANK))
                    new_kr.append(outs[4].reshape(nb, seq, ROPE))
                at = _attention(q, k, v, kc, vc, nb=nb, seq=seq, tq=min(seq, 1024))
                x = _attn_out(x, at, mod, w_o[j], seq=seq, tm=tm)
            x = _ffn(x, mod, g2, w_up[l], ffn_conv_w[l], ffn_conv_b[l].reshape(1, -1), w_down[l], seq=seq, tm=tm)
        return x.reshape(nb, seq, d), new_ckv, new_kr

    y_prompt, ckv_list, kr_list = trunk(x_prompt, mods[:, 0:1], None)
    y_sample, _, _ = trunk(x_sample, mods[:, 1:1 + nbs], (cache_ckv, cache_krope))
    return (y_prompt, y_sample, jnp.stack(ckv_list, axis=1), jnp.stack(kr_list, axis=1))
```

```python
import functools
import math

import jax
import jax.numpy as jnp
from jax import lax
from jax.experimental import pallas as pl
from jax.experimental.pallas import tpu as pltpu

f32 = jnp.float32
bf16 = jnp.bfloat16

EPS = 1e-6
GRID_W = 64
CHUNK = 128
A_GROUPS = 4
HEADS = 8
NOPE = 128
ROPE = 64
QK = NOPE + ROPE
HEAD_PAD = 256
V_DIM = 128
KV_RANK = 256
ROPE_BASE = 10000.0
FILTER_BANDS = 16
LANES = 128
HALO = 8
TOEP = 256
VMEM_LIMIT = 56 * 1024 * 1024
HIGHEST = lax.Precision.HIGHEST


def _cparams(*sem):
    return pltpu.CompilerParams(dimension_semantics=sem, vmem_limit_bytes=VMEM_LIMIT)


def _norm_mod(x, g, sc, sh):
    y = x * lax.rsqrt(jnp.mean(x * x, axis=-1, keepdims=True) + EPS)
    return (y * g) * (1.0 + sc) + sh


def _halo_ext(x_ref, xp_ref, xn_ref, g, sc, sh, first, last):
    hp = jnp.where(first, 0.0, _norm_mod(xp_ref[...], g, sc, sh))
    hn = jnp.where(last, 0.0, _norm_mod(xn_ref[...], g, sc, sh))
    h = _norm_mod(x_ref[...], g, sc, sh)
    return jnp.concatenate([hp, h, hn], axis=0).astype(bf16)


def _dwconv3(y, w, b, tm):
    return (y[HALO - 1:HALO - 1 + tm] * w[0:1] + y[HALO:HALO + tm] * w[1:2]
            + y[HALO + 1:HALO + 1 + tm] * w[2:3] + b)


def _gelu_tanh(x):
    return 0.5 * x * (1.0 + jnp.tanh(math.sqrt(2.0 / math.pi) * (x + 0.044715 * (x * x * x))))


def _row_specs(tm, d, t_rows):
    r = tm // HALO
    nb = t_rows // HALO
    return (pl.BlockSpec((tm, d), lambda i, *_: (i, 0)),
            pl.BlockSpec((HALO, d), lambda i, *_: (jnp.maximum(i * r - 1, 0), 0)),
            pl.BlockSpec((HALO, d), lambda i, *_: (jnp.minimum((i + 1) * r, nb - 1), 0)))


def _ada_kernel(c_ref, w_ref, b_ref, o_ref):
    c = c_ref[...]
    s = c * jax.nn.sigmoid(c)
    o_ref[0] = jnp.dot(s, w_ref[0], precision=HIGHEST, preferred_element_type=f32) + b_ref[0]


def _ada(cond, ada_w, ada_b):
    depth, d, n = ada_w.shape
    tn = n // 4
    rows = cond.shape[0]
    return pl.pallas_call(
        _ada_kernel,
        out_shape=jax.ShapeDtypeStruct((depth, rows, n), f32),
        grid=(depth, n // tn),
        in_specs=[pl.BlockSpec((rows, d), lambda l, j: (0, 0)),
                  pl.BlockSpec((1, d, tn), lambda l, j: (l, 0, j)),
                  pl.BlockSpec((1, 1, tn), lambda l, j: (l, 0, j))],
        out_specs=pl.BlockSpec((1, rows, tn), lambda l, j: (l, 0, j)),
        compiler_params=_cparams("arbitrary", "arbitrary"),
        name="ada_mod",
    )(cond, ada_w, ada_b.reshape(depth, 1, n))


def _even_front_kernel(x_ref, xp_ref, xn_ref, mod_ref, g_ref, win_ref, sw_ref, sb_ref, cw_ref, cb_ref,
                       a_ref, xbt_ref, *, tm, tpb, a_width):
    i = pl.program_id(0)
    mod = mod_ref[0]
    h = _halo_ext(x_ref, xp_ref, xn_ref, g_ref[...], mod[1:2], mod[0:1], (i % tpb) == 0, (i % tpb) == tpb - 1)
    p = jnp.dot(h, win_ref[...], preferred_element_type=f32)
    u = _gelu_tanh(p[HALO:HALO + tm, 0:a_width])
    v = _gelu_tanh(p[HALO:HALO + tm, a_width:2 * a_width]).astype(bf16)
    nch = tm // CHUNK
    a_ch = a_width // A_GROUPS
    for gi in range(A_GROUPS):
        cols = slice(gi * a_ch, (gi + 1) * a_ch)
        rhs = jnp.concatenate([v[n * CHUNK:(n + 1) * CHUNK, cols] for n in range(nch)], axis=1)
        s = jnp.dot(sw_ref[gi], rhs, preferred_element_type=f32)
        for n in range(nch):
            rows = slice(n * CHUNK, (n + 1) * CHUNK)
            sn = s[:, n * a_ch:(n + 1) * a_ch] + sb_ref[gi]
            a_ref[rows, cols] = (u[rows, cols] * sn).astype(bf16)
    xb = _dwconv3(p[:, 2 * a_width:], cw_ref[...], cb_ref[...], tm)
    xbt_ref[0] = xb.T


def _even_front(x, mod, g, w_in, sgu_w, sgu_b, conv_w, conv_b, *, nb, seq, tm):
    t_rows, d = x.shape
    tpb = seq // tm
    n_in = w_in.shape[1]
    a_width = A_GROUPS * sgu_w.shape[1]
    nxb = n_in - 2 * a_width
    mrows = seq if mod.shape[0] > 1 else t_rows
    kern = functools.partial(_even_front_kernel, tm=tm, tpb=tpb, a_width=a_width)
    full = lambda shape: pl.BlockSpec(shape, lambda i: (0,) * len(shape))
    return pl.pallas_call(
        kern,
        out_shape=(jax.ShapeDtypeStruct((t_rows, a_width), bf16),
                   jax.ShapeDtypeStruct((nb, nxb, seq), f32)),
        grid=(t_rows // tm,),
        in_specs=[*_row_specs(tm, d, t_rows),
                  pl.BlockSpec((1, 6, d), lambda i: ((i * tm) // mrows, 0, 0)),
                  full((1, d)), full(w_in.shape), full(sgu_w.shape), full(sgu_b.shape),
                  full(conv_w.shape), full(conv_b.shape)],
        out_specs=(pl.BlockSpec((tm, a_width), lambda i: (i, 0)),
                   pl.BlockSpec((1, nxb, tm), lambda i: (i // tpb, 0, i % tpb))),
        compiler_params=_cparams("arbitrary"),
        name="even_front",
    )(x, x, x, mod, g, w_in, sgu_w, sgu_b, conv_w, conv_b)


def _filter_kernel(w1t_ref, b1_ref, w2t_ref, b2_ref, fr_ref, w3t_ref, dec_ref, o_ref, *, seq):
    nfeat = w1t_ref.shape[1]
    t = lax.broadcasted_iota(jnp.int32, (nfeat, seq), 1).astype(f32)
    fi = lax.broadcasted_iota(jnp.int32, (nfeat, seq), 0)
    tn = t / seq
    band = jnp.where(fi <= FILTER_BANDS, fi, fi - FILTER_BANDS).astype(f32)
    ang = ((2.0 * math.pi) * tn) * band
    z = jnp.where(fi == 0, tn, jnp.where(fi <= FILTER_BANDS, jnp.sin(ang), jnp.cos(ang)))
    z = jnp.where(fi <= 2 * FILTER_BANDS, z, 0.0)
    fr = fr_ref[...]
    h = jnp.sin(fr * (jnp.dot(w1t_ref[...], z, precision=HIGHEST, preferred_element_type=f32) + b1_ref[...]))
    h = jnp.sin(fr * (jnp.dot(w2t_ref[...], h, precision=HIGHEST, preferred_element_type=f32) + b2_ref[...]))
    h = jnp.dot(w3t_ref[...], h, precision=HIGHEST, preferred_element_type=f32)
    tl = lax.broadcasted_iota(jnp.int32, (1, seq), 1).astype(f32)
    dist = jnp.abs(tl - (seq // 2)) / seq
    h = h * jnp.exp(-jnp.abs(dec_ref[...]) * dist)
    o_ref[...] = h / (jnp.sum(jnp.abs(h), axis=1, keepdims=True) + EPS)


def _hyena_filters_t(seq, w1, b1, w2, b2, w3, freq, decay):
    nfeat = 40
    hid = w1.shape[1]
    w1t = jnp.zeros((hid, nfeat), f32).at[:, :w1.shape[0]].set(w1.T)
    rows = w3.shape[1]
    rb = 256
    kern = functools.partial(_filter_kernel, seq=seq)
    full = lambda shape: pl.BlockSpec(shape, lambda i: (0,) * len(shape))
    return pl.pallas_call(
        kern,
        out_shape=jax.ShapeDtypeStruct((rows, seq), f32),
        grid=(rows // rb,),
        in_specs=[full((hid, nfeat)), full((hid, 1)), full((hid, hid)), full((hid, 1)), full((hid, 1)),
                  pl.BlockSpec((rb, hid), lambda i: (i, 0)), pl.BlockSpec((rb, 1), lambda i: (i, 0))],
        out_specs=pl.BlockSpec((rb, seq), lambda i: (i, 0)),
        compiler_params=_cparams("arbitrary"),
        name="hyena_filter",
    )(w1t, b1.reshape(hid, 1), w2.T, b2.reshape(hid, 1), freq.reshape(hid, 1), w3.T, decay.reshape(rows, 1))


def _toeplitz_bank(hrow, bank_ref, seq):
    nseg = seq // LANES
    q = lax.broadcasted_iota(jnp.int32, (LANES, LANES), 0)
    p = lax.broadcasted_iota(jnp.int32, (LANES, LANES), 1)
    upper = p >= q
    zero = jnp.zeros((LANES, LANES), f32)
    bank_ref[0] = zero.astype(bf16)
    bank_ref[nseg + 2] = zero.astype(bf16)
    prev = zero
    for k in range(nseg + 1):
        if k < nseg:
            seg = jnp.broadcast_to(hrow[:, k * LANES:(k + 1) * LANES], (LANES, LANES))
            cur = pltpu.roll(seg, 0, 1, stride=1, stride_axis=0)
        else:
            cur = zero
        bank_ref[k + 1] = jnp.where(upper, cur, prev).astype(bf16)
        prev = cur


def _long_conv(u, hrow, bank_ref, upad_ref, *, nb, seq):
    nseg = seq // LANES
    nblk = seq // TOEP
    half = seq // 2
    _toeplitz_bank(hrow, bank_ref, seq)
    upad_ref[:, 0:half] = jnp.zeros((nb, half), f32)
    upad_ref[:, half:half + seq] = u
    upad_ref[:, half + seq:2 * seq] = jnp.zeros((nb, half), f32)
    acc = None
    for kt in range(nblk + 1):
        lhs = jnp.concatenate([upad_ref[:, (i + kt) * TOEP:(i + kt + 1) * TOEP] for i in range(nblk)], axis=0)
        k0 = nseg - 2 * kt
        top = jnp.concatenate([bank_ref[k0 + 1], bank_ref[k0 + 2]], axis=1)
        bot = jnp.concatenate([bank_ref[k0], bank_ref[k0 + 1]], axis=1)
        w = jnp.concatenate([top, bot], axis=0)
        part = jnp.dot(lhs.astype(bf16), w, preferred_element_type=f32)
        acc = part if acc is None else acc + part
    return jnp.concatenate([acc[i * nb:(i + 1) * nb] for i in range(nblk)], axis=1)


def _hyena_kernel(d_ref, vb_ref, x1_ref, x2_ref, h0_ref, h1_ref, z_ref,
                  u_scr, a_scr, b_scr, o_scr, bank_ref, upad_ref, *, nb, seq, cg, nchan):
    gidx = pl.program_id(0)
    u_scr[...] = jnp.swapaxes(vb_ref[...], 0, 1)
    a_scr[...] = jnp.swapaxes(x1_ref[...], 0, 1)
    b_scr[...] = jnp.swapaxes(x2_ref[...], 0, 1)

    def body(c, carry):
        ch = gidx * cg + c
        u = u_scr[c]
        y = _long_conv(u, h0_ref[pl.ds(c, 1), :], bank_ref, upad_ref, nb=nb, seq=seq)
        z = a_scr[c] * (y + u * d_ref[ch])
        y = _long_conv(z, h1_ref[pl.ds(c, 1), :], bank_ref, upad_ref, nb=nb, seq=seq)
        o_scr[c] = b_scr[c] * (y + z * d_ref[nchan + ch])
        return carry

    lax.fori_loop(0, cg, body, 0)
    z_ref[...] = jnp.swapaxes(o_scr[...], 0, 1)


def _hyena(xbt, filt_t, d_flat, *, nb, seq):
    nchan = xbt.shape[1] // 3
    cg = 8
    ng = nchan // cg
    kern = functools.partial(_hyena_kernel, nb=nb, seq=seq, cg=cg, nchan=nchan)
    act = lambda off: pl.BlockSpec((nb, cg, seq), lambda i: (0, off + i, 0))
    return pl.pallas_call(
        kern,
        out_shape=jax.ShapeDtypeStruct((nb, nchan, seq), f32),
        grid=(ng,),
        in_specs=[pl.BlockSpec(memory_space=pltpu.SMEM),
                  act(0), act(ng), act(2 * ng),
                  pl.BlockSpec((cg, seq), lambda i: (i, 0)),
                  pl.BlockSpec((cg, seq), lambda i: (ng + i, 0))],
        out_specs=pl.BlockSpec((nb, cg, seq), lambda i: (0, i, 0)),
        scratch_shapes=[pltpu.VMEM((cg, nb, seq), f32), pltpu.VMEM((cg, nb, seq), f32),
                        pltpu.VMEM((cg, nb, seq), f32), pltpu.VMEM((cg, nb, seq), f32),
                        pltpu.VMEM((seq // LANES + 3, LANES, LANES), bf16),
                        pltpu.VMEM((nb, 2 * seq), f32)],
        compiler_params=_cparams("arbitrary"),
        name="hyena_conv",
    )(d_flat, xbt, xbt, xbt, filt_t, filt_t)


def _even_out_kernel(x_ref, a_ref, zt_ref, mod_ref, wa_ref, wz_ref, o_ref):
    out = jnp.dot(a_ref[...], wa_ref[...], preferred_element_type=f32)
    out = out + lax.dot_general(zt_ref[0].astype(bf16), wz_ref[...], (((0,), (0,)), ((), ())),
                                preferred_element_type=f32)
    o_ref[...] = x_ref[...] + mod_ref[0][2:3] * out


def _even_out(x, a, zt, mod, w_a, w_z, *, seq, tm):
    t_rows, d = x.shape
    tpb = seq // tm
    mrows = seq if mod.shape[0] > 1 else t_rows
    full = lambda shape: pl.BlockSpec(shape, lambda i: (0,) * len(shape))
    return pl.pallas_call(
        _even_out_kernel,
        out_shape=jax.ShapeDtypeStruct((t_rows, d), f32),
        grid=(t_rows // tm,),
        in_specs=[pl.BlockSpec((tm, d), lambda i: (i, 0)),
                  pl.BlockSpec((tm, a.shape[1]), lambda i: (i, 0)),
                  pl.BlockSpec((1, zt.shape[1], tm), lambda i: (i // tpb, 0, i % tpb)),
                  pl.BlockSpec((1, 6, d), lambda i: ((i * tm) // mrows, 0, 0)),
                  full(w_a.shape), full(w_z.shape)],
        out_specs=pl.BlockSpec((tm, d), lambda i: (i, 0)),
        compiler_params=_cparams("arbitrary"),
        name="even_out",
    )(x, a, zt, mod, w_a, w_z)


def _rope128(x, cos, sin):
    lane = lax.broadcasted_iota(jnp.int32, x.shape, 1)
    swapped = jnp.where((lane & 16) == 0, pltpu.roll(x, LANES - 16, 1), pltpu.roll(x, 16, 1))
    return x * cos + swapped * sin


def _expand_kv(ckv, kr_pad, wuk_ref, wuv_ref, kg_ref, cos, sin, k_ref, v_ref):
    cb = ckv.astype(bf16)
    kn = jnp.dot(cb, wuk_ref[...], preferred_element_type=f32)
    v = jnp.dot(cb, wuv_ref[...], preferred_element_type=f32)
    v_ref[0] = v.T.astype(bf16)
    kg = kg_ref[...]
    g_nope, g_rope = kg[:, 0:NOPE], kg[:, NOPE:2 * NOPE]
    kr_ss = jnp.sum(kr_pad * kr_pad, axis=-1, keepdims=True)
    krg = kr_pad * g_rope
    if cos is not None:
        krg = _rope128(krg, cos, sin)
    for hd in range(HEADS):
        kh = kn[:, hd * NOPE:(hd + 1) * NOPE]
        r = lax.rsqrt((jnp.sum(kh * kh, axis=-1, keepdims=True) + kr_ss) / QK + EPS)
        k_ref[:, hd * HEAD_PAD:hd * HEAD_PAD + NOPE] = (kh * r * g_nope).astype(bf16)
        k_ref[:, hd * HEAD_PAD + NOPE:(hd + 1) * HEAD_PAD] = (krg * r).astype(bf16)


def _mla_front_kernel(*refs, rope, emit_cache):
    (x_ref, mod_ref, g_ref, wdq_ref, qn_ref, wuq_ref, qg_ref, wdkv_ref, kvn_ref, wuk_ref, wuv_ref, kg_ref) = refs[:12]
    refs = refs[12:]
    if rope:
        cos_ref, sin_ref = refs[:2]
        refs = refs[2:]
        cos, sin = cos_ref[...], sin_ref[...]
    else:
        cos = sin = None
    q_ref, k_ref, v_ref = refs[:3]
    mod = mod_ref[0]
    h = _norm_mod(x_ref[...], g_ref[...], mod[1:2], mod[0:1]).astype(bf16)
    ql = jnp.dot(h, wdq_ref[...], preferred_element_type=f32)
    ql = ql * lax.rsqrt(jnp.mean(ql * ql, axis=-1, keepdims=True) + EPS) * qn_ref[...]
    q = jnp.dot(ql.astype(bf16), wuq_ref[...], preferred_element_type=f32)
    qg = qg_ref[...]
    c_exp = (1.0 / math.sqrt(QK)) * math.log2(math.e)
    for hd in range(HEADS):
        qh = q[:, hd * HEAD_PAD:(hd + 1) * HEAD_PAD]
        r = lax.rsqrt(jnp.sum(qh * qh, axis=-1, keepdims=True) / QK + EPS) * c_exp
        qh = qh * r * qg
        if rope:
            q_ref[:, hd * HEAD_PAD:hd * HEAD_PAD + NOPE] = qh[:, 0:NOPE].astype(bf16)
            q_ref[:, hd * HEAD_PAD + NOPE:(hd + 1) * HEAD_PAD] = _rope128(qh[:, NOPE:], cos, sin).astype(bf16)
        else:
            q_ref[:, hd * HEAD_PAD:(hd + 1) * HEAD_PAD] = qh.astype(bf16)
    dkv = jnp.dot(h, wdkv_ref[...], preferred_element_type=f32)
    c_raw = dkv[:, 0:KV_RANK]
    ckv = c_raw * lax.rsqrt(jnp.mean(c_raw * c_raw, axis=-1, keepdims=True) + EPS) * kvn_ref[...]
    kr_pad = dkv[:, KV_RANK:]
    if emit_cache:
        ckv_ref, kr_ref = refs[3:5]
        ckv_ref[...] = ckv
        kr_ref[...] = kr_pad[:, 0:ROPE]
    _expand_kv(ckv, kr_pad, wuk_ref, wuv_ref, kg_ref, cos, sin, k_ref, v_ref)


def _mla_front(x, mod, g, w, tabs, *, nb, seq, tm, emit_cache):
    t_rows, d = x.shape
    tpb = seq // tm
    rope = tabs is not None
    mrows = seq if mod.shape[0] > 1 else t_rows
    kern = functools.partial(_mla_front_kernel, rope=rope, emit_cache=emit_cache)
    full = lambda a: pl.BlockSpec(a.shape, lambda i: (0,) * a.ndim)
    wnames = ("w_dq", "q_norm", "w_uq", "q_gain", "w_dkv", "kv_norm", "w_uk", "w_uv", "k_gain")
    ins = [x, mod, g] + [w[n] for n in wnames]
    in_specs = [pl.BlockSpec((tm, d), lambda i: (i, 0)),
                pl.BlockSpec((1, 6, d), lambda i: ((i * tm) // mrows, 0, 0)),
                full(g)] + [full(w[n]) for n in wnames]
    if rope:
        ins += list(tabs)
        in_specs += [pl.BlockSpec((tm, LANES), lambda i: (i % tpb, 0))] * 2
    hp = HEADS * HEAD_PAD
    out_shape = [jax.ShapeDtypeStruct((t_rows, hp), bf16), jax.ShapeDtypeStruct((t_rows, hp), bf16),
                 jax.ShapeDtypeStruct((nb, HEADS * V_DIM, seq), bf16)]
    out_specs = [pl.BlockSpec((tm, hp), lambda i: (i, 0)), pl.BlockSpec((tm, hp), lambda i: (i, 0)),
                 pl.BlockSpec((1, HEADS * V_DIM, tm), lambda i: (i // tpb, 0, i % tpb))]
    if emit_cache:
        out_shape += [jax.ShapeDtypeStruct((t_rows, KV_RANK), f32), jax.ShapeDtypeStruct((t_rows, ROPE), f32)]
        out_specs += [pl.BlockSpec((tm, KV_RANK), lambda i: (i, 0)), pl.BlockSpec((tm, ROPE), lambda i: (i, 0))]
    return pl.pallas_call(
        kern, out_shape=tuple(out_shape), grid=(t_rows // tm,), in_specs=in_specs, out_specs=tuple(out_specs),
        compiler_params=_cparams("arbitrary"), name="mla_front",
    )(*ins)


def _cache_expand_kernel(ckv_ref, kr_ref, wuk_ref, wuv_ref, kg_ref, k_ref, v_ref):
    _expand_kv(ckv_ref[...], kr_ref[...], wuk_ref, wuv_ref, kg_ref, None, None, k_ref, v_ref)


def _cache_expand(ckv, kr_pad, w, *, nb, seq):
    t_rows = ckv.shape[0]
    tm = seq
    full = lambda a: pl.BlockSpec(a.shape, lambda i: (0,) * a.ndim)
    hp = HEADS * HEAD_PAD
    return pl.pallas_call(
        _cache_expand_kernel,
        out_shape=(jax.ShapeDtypeStruct((t_rows, hp), bf16), jax.ShapeDtypeStruct((nb, HEADS * V_DIM, seq), bf16)),
        grid=(t_rows // tm,),
        in_specs=[pl.BlockSpec((tm, KV_RANK), lambda i: (i, 0)), pl.BlockSpec((tm, LANES), lambda i: (i, 0)),
                  full(w["w_uk"]), full(w["w_uv"]), full(w["k_gain"])],
        out_specs=(pl.BlockSpec((tm, hp), lambda i: (i, 0)),
                   pl.BlockSpec((1, HEADS * V_DIM, tm), lambda i: (i, 0, 0))),
        compiler_params=_cparams("arbitrary"), name="mla_cache_expand",
    )(ckv, kr_pad, w["w_uk"], w["w_uv"], w["k_gain"])


def _attn_kernel(*refs, cached, tk):
    if cached:
        q_ref, k_ref, kc_ref, vt_ref, vct_ref, o_ref, s_scr, m_scr = refs
        parts = ((k_ref, vt_ref), (kc_ref, vct_ref))
    else:
        q_ref, k_ref, vt_ref, o_ref, s_scr, m_scr = refs
        parts = ((k_ref, vt_ref),)
    nt = (((1,), (1,)), ((), ()))

    @pl.when(pl.program_id(0) == 0)
    def _():
        s_scr[...] = jnp.zeros_like(s_scr)
        m_scr[...] = jnp.zeros_like(m_scr)

    q = q_ref[...]
    m_old = m_scr[...]
    ones = jnp.ones((16, tk), bf16)
    m_new = acc = None
    base = 0
    for kr, vtr in parts:
        for j in range(kr.shape[0] // tk):
            rows = slice(base + j * tk, base + (j + 1) * tk)
            p = jnp.exp2(s_scr[rows, :] - m_old).astype(bf16)
            lhs = jnp.concatenate([vtr[0, :, j * tk:(j + 1) * tk], ones], axis=0)
            part = jnp.dot(lhs, p, preferred_element_type=f32)
            acc = part if acc is None else acc + part
            s = lax.dot_general(kr[j * tk:(j + 1) * tk, :], q, nt, preferred_element_type=f32)
            s_scr[rows, :] = s
            cm = jnp.max(s, axis=0, keepdims=True)
            m_new = cm if m_new is None else jnp.maximum(m_new, cm)
        base += kr.shape[0]
    m_scr[...] = m_new
    o_ref[0] = (acc[0:V_DIM] / acc[V_DIM:V_DIM + 1]).astype(bf16)


def _attention(q, k, vt, kc, vct, *, nb, seq, tq):
    cached = kc is not None
    nq = seq // tq
    n_tiles = nb * HEADS * nq
    past = kc.shape[0] // nb if cached else 0
    tk = min(seq, TOEP)

    def cur(n):
        n = jnp.minimum(n, n_tiles - 1)
        return n // (HEADS * nq), (n // nq) % HEADS, n % nq

    def prev(n):
        n = jnp.maximum(n - 1, 0)
        return n // (HEADS * nq), (n // nq) % HEADS, n % nq

    def spec(shape, fn):
        return pl.BlockSpec(shape, fn)

    kern = functools.partial(_attn_kernel, cached=cached, tk=tk)
    q_spec = spec((tq, HEAD_PAD), lambda n: (cur(n)[0] * nq + cur(n)[2], cur(n)[1]))
    k_spec = lambda rows: spec((rows, HEAD_PAD), lambda n: (cur(n)[0], cur(n)[1]))
    v_spec = lambda cols: spec((1, V_DIM, cols), lambda n: (prev(n)[0], prev(n)[1], 0))
    if cached:
        ins = [q, k, kc, vt, vct]
        in_specs = [q_spec, k_spec(seq), k_spec(past), v_spec(seq), v_spec(past)]
    else:
        ins = [q, k, vt]
        in_specs = [q_spec, k_spec(seq), v_spec(seq)]
    return pl.pallas_call(
        kern,
        out_shape=jax.ShapeDtypeStruct((nb, HEADS * V_DIM, seq), bf16),
        grid=(n_tiles + 1,),
        in_specs=in_specs,
        out_specs=pl.BlockSpec((1, V_DIM, tq), lambda n: prev(n)),
        scratch_shapes=[pltpu.VMEM((seq + past, tq), f32), pltpu.VMEM((1, tq), f32)],
        compiler_params=_cparams("arbitrary"),
        name="mla_attention",
    )(*ins)


def _attn_out_kernel(x_ref, at_ref, mod_ref, wo_ref, o_ref):
    out = lax.dot_general(at_ref[0], wo_ref[...], (((0,), (0,)), ((), ())), preferred_element_type=f32)
    o_ref[...] = x_ref[...] + mod_ref[0][2:3] * out


def _attn_out(x, at, mod, w_o, *, seq, tm):
    t_rows, d = x.shape
    tpb = seq // tm
    mrows = seq if mod.shape[0] > 1 else t_rows
    return pl.pallas_call(
        _attn_out_kernel,
        out_shape=jax.ShapeDtypeStruct((t_rows, d), f32),
        grid=(t_rows // tm,),
        in_specs=[pl.BlockSpec((tm, d), lambda i: (i, 0)),
                  pl.BlockSpec((1, at.shape[1], tm), lambda i: (i // tpb, 0, i % tpb)),
                  pl.BlockSpec((1, 6, d), lambda i: ((i * tm) // mrows, 0, 0)),
                  pl.BlockSpec(w_o.shape, lambda i: (0, 0))],
        out_specs=pl.BlockSpec((tm, d), lambda i: (i, 0)),
        compiler_params=_cparams("arbitrary"),
        name="mla_out",
    )(x, at, mod, w_o)


def _ffn_kernel(x_ref, xp_ref, xn_ref, mod_ref, g_ref, wup_ref, cw_ref, cb_ref, wd_ref, o_ref, *, tm, tpb, chunks):
    i = pl.program_id(0)
    mod = mod_ref[0]
    h = _halo_ext(x_ref, xp_ref, xn_ref, g_ref[...], mod[4:5], mod[3:4], (i % tpb) == 0, (i % tpb) == tpb - 1)
    dff = wd_ref.shape[0]
    acc = None
    for c0, c1 in chunks:
        halves = []
        for off in (c0, dff + c0):
            cols = slice(off, off + c1 - c0)
            y = jnp.dot(h, wup_ref[:, cols], preferred_element_type=f32)
            halves.append(_dwconv3(y, cw_ref[:, cols], cb_ref[:, cols], tm))
        gate, up = halves
        act = ((gate * jax.nn.sigmoid(gate)) * up).astype(bf16)
        part = jnp.dot(act, wd_ref[c0:c1, :], preferred_element_type=f32)
        acc = part if acc is None else acc + part
    o_ref[...] = x_ref[...] + mod[5:6] * acc


def _ffn(x, mod, g, w_up, conv_w, conv_b, w_down, *, seq, tm):
    t_rows, d = x.shape
    dff = w_down.shape[0]
    step = 6 * TOEP
    chunks = tuple((c, min(c + step, dff)) for c in range(0, dff, step))
    tpb = seq // tm
    mrows = seq if mod.shape[0] > 1 else t_rows
    kern = functools.partial(_ffn_kernel, tm=tm, tpb=tpb, chunks=chunks)
    resident = lambda a: pl.BlockSpec(a.shape, lambda i: (0,) * a.ndim, pipeline_mode=pl.Buffered(1))
    return pl.pallas_call(
        kern,
        out_shape=jax.ShapeDtypeStruct((t_rows, d), f32),
        grid=(t_rows // tm,),
        in_specs=[*_row_specs(tm, d, t_rows),
                  pl.BlockSpec((1, 6, d), lambda i: ((i * tm) // mrows, 0, 0)),
                  resident(g), resident(w_up), resident(conv_w), resident(conv_b), resident(w_down)],
        out_specs=pl.BlockSpec((tm, d), lambda i: (i, 0)),
        compiler_params=_cparams("arbitrary"),
        name="conv_ffn",
    )(x, x, x, mod, g, w_up, conv_w, conv_b, w_down)


def _rope_tables(seq):
    half = ROPE // 2
    rows = seq // GRID_W
    row = jnp.repeat(jnp.arange(rows), GRID_W)
    col = jnp.tile(jnp.arange(GRID_W), rows)
    inv = 1.0 / (ROPE_BASE ** (jnp.arange(0, half, 2, dtype=f32) / half))
    ar = row.astype(f32)[:, None] * inv[None]
    ac = col.astype(f32)[:, None] * inv[None]
    pad = jnp.zeros((seq, LANES - ROPE), f32)
    cos = jnp.concatenate([jnp.cos(ar), jnp.cos(ar), jnp.cos(ac), jnp.cos(ac), pad], axis=1)
    sin = jnp.concatenate([-jnp.sin(ar), jnp.sin(ar), -jnp.sin(ac), jnp.sin(ac), pad], axis=1)
    return cos, sin


def _pad_heads(a, width):
    lead = a.shape[:-1]
    a = a.reshape(*lead, HEADS, -1)
    a = jnp.pad(a, [(0, 0)] * len(lead) + [(0, 0), (0, width - a.shape[-1])])
    return a.reshape(*lead, HEADS * width)


def _mla_weights(j, mla_w_dq, mla_q_norm, mla_w_uq, mla_w_dkv, mla_kv_norm, mla_w_ukv, mla_q_head_norm,
                 mla_k_head_norm):
    ukv = mla_w_ukv[j].reshape(KV_RANK, HEADS, NOPE + V_DIM)
    kg = mla_k_head_norm[j]
    return {
        "w_dq": mla_w_dq[j].astype(bf16),
        "q_norm": mla_q_norm[j].reshape(1, -1),
        "w_uq": _pad_heads(mla_w_uq[j], HEAD_PAD).astype(bf16),
        "q_gain": jnp.pad(mla_q_head_norm[j], (0, HEAD_PAD - QK)).reshape(1, HEAD_PAD),
        "w_dkv": jnp.pad(mla_w_dkv[j], ((0, 0), (0, LANES - ROPE))).astype(bf16),
        "kv_norm": mla_kv_norm[j].reshape(1, -1),
        "w_uk": ukv[:, :, :NOPE].reshape(KV_RANK, HEADS * NOPE).astype(bf16),
        "w_uv": ukv[:, :, NOPE:].reshape(KV_RANK, HEADS * V_DIM).astype(bf16),
        "k_gain": jnp.pad(kg, (0, 2 * NOPE - QK)).reshape(1, 2 * NOPE),
    }


def kernel(x_prompt, x_sample, cache_ckv, cache_krope, c, c_ctx, ada_w, ada_b, norm_g, mix_w_in, sgu_w, sgu_b, hy_conv_w, hy_conv_b, hy_f_w1, hy_f_b1, hy_f_w2, hy_f_b2, hy_f_w3, hy_f_freq, hy_decay, hy_d, mix_w_out, mla_w_dq, mla_q_norm, mla_w_uq, mla_w_dkv, mla_kv_norm, mla_w_ukv, mla_q_head_norm, mla_k_head_norm, mla_w_o, ffn_w_up, ffn_conv_w, ffn_conv_b, ffn_w_down):
    depth = ada_w.shape[0]
    d = x_prompt.shape[-1]
    nbp, seqp, _ = x_prompt.shape
    nbs, seqs, _ = x_sample.shape
    a_width = A_GROUPS * sgu_w.shape[-1]

    cond = jnp.zeros((16, d), f32).at[0].set(c_ctx).at[1:1 + nbs].set(c)
    mods = _ada(cond, ada_w, ada_b).reshape(depth, 16, 6, d)

    w_in = mix_w_in.astype(bf16)
    sgu_wb = sgu_w.astype(bf16)
    sgu_bb = jnp.broadcast_to(sgu_b[..., None], sgu_b.shape + (sgu_w.shape[-1],))
    w_out = mix_w_out.astype(bf16)
    w_up = ffn_w_up.astype(bf16)
    w_down = ffn_w_down.astype(bf16)
    w_o = mla_w_o.astype(bf16)
    mla_w = [_mla_weights(j, mla_w_dq, mla_q_norm, mla_w_uq, mla_w_dkv, mla_kv_norm, mla_w_ukv,
                          mla_q_head_norm, mla_k_head_norm) for j in range(depth // 2)]

    def trunk(x3, mod_all, cache):
        nb, seq, _ = x3.shape
        x = x3.reshape(nb * seq, d)
        tm = min(seq, 512)
        latent = cache is not None
        tabs = _rope_tables(seq) if latent else None
        new_ckv, new_kr = [], []
        for l in range(depth):
            mod = mod_all[l]
            g1 = norm_g[l, 0].reshape(1, d)
            g2 = norm_g[l, 1].reshape(1, d)
            if l % 2 == 0:
                i = l // 2
                a, xbt = _even_front(x, mod, g1, w_in[i], sgu_wb[i], sgu_bb[i], hy_conv_w[i],
                                     hy_conv_b[i].reshape(1, -1), nb=nb, seq=seq, tm=tm)
                filt = _hyena_filters_t(seq, hy_f_w1[i], hy_f_b1[i], hy_f_w2[i], hy_f_b2[i], hy_f_w3[i],
                                        hy_f_freq[i], hy_decay[i])
                zt = _hyena(xbt, filt, hy_d[i].reshape(-1), nb=nb, seq=seq)
                x = _even_out(x, a, zt, mod, w_out[i, :a_width], w_out[i, a_width:], seq=seq, tm=tm)
            else:
                j = l // 2
                outs = _mla_front(x, mod, g1, mla_w[j], tabs, nb=nb, seq=seq, tm=tm, emit_cache=not latent)
                q, k, v = outs[:3]
                if latent:
                    past = cache[0].shape[2]
                    ckv_c = cache[0][:, j].reshape(nb * past, KV_RANK)
                    kr_c = jnp.pad(cache[1][:, j].reshape(nb * past, ROPE), ((0, 0), (0, LANES - ROPE)))
                    kc, vc = _cache_expand(ckv_c, kr_c, mla_w[j], nb=nb, seq=past)
                else:
                    kc = vc = None
                    new_ckv.append(outs[3].reshape(nb, seq, KV_RANK))
                    new_kr.append(outs[4].reshape(nb, seq, ROPE))
                at = _attention(q, k, v, kc, vc, nb=nb, seq=seq, tq=min(seq, 1024))
                x = _attn_out(x, at, mod, w_o[j], seq=seq, tm=tm)
            x = _ffn(x, mod, g2, w_up[l], ffn_conv_w[l], ffn_conv_b[l].reshape(1, -1), w_down[l], seq=seq, tm=tm)
        return x.reshape(nb, seq, d), new_ckv, new_kr

    y_prompt, ckv_list, kr_list = trunk(x_prompt, mods[:, 0:1], None)
    y_sample, _, _ = trunk(x_sample, mods[:, 1:1 + nbs], (cache_ckv, cache_krope))
    return (y_prompt, y_sample, jnp.stack(ckv_list, axis=1), jnp.stack(kr_list, axis=1))
```

```python
import functools
import math

import jax
import jax.numpy as jnp
from jax import lax
from jax.experimental import pallas as pl
from jax.experimental.pallas import tpu as pltpu

f32 = jnp.float32
bf16 = jnp.bfloat16

EPS = 1e-6
GRID_W = 64
CHUNK = 128
A_GROUPS = 4
HEADS = 8
NOPE = 128
ROPE = 64
QK = NOPE + ROPE
HEAD_PAD = 256
V_DIM = 128
KV_RANK = 256
ROPE_BASE = 10000.0
FILTER_BANDS = 16
LANES = 128
HALO = 8
TOEP = 256
VMEM_LIMIT = 56 * 1024 * 1024
HIGHEST = lax.Precision.HIGHEST


def _cparams(*sem):
    return pltpu.CompilerParams(dimension_semantics=sem, vmem_limit_bytes=VMEM_LIMIT)


def _norm_mod(x, g, sc, sh):
    y = x * lax.rsqrt(jnp.mean(x * x, axis=-1, keepdims=True) + EPS)
    return (y * g) * (1.0 + sc) + sh


def _halo_ext(x_ref, xp_ref, xn_ref, g, sc, sh, first, last):
    hp = jnp.where(first, 0.0, _norm_mod(xp_ref[...], g, sc, sh))
    hn = jnp.where(last, 0.0, _norm_mod(xn_ref[...], g, sc, sh))
    h = _norm_mod(x_ref[...], g, sc, sh)
    return jnp.concatenate([hp, h, hn], axis=0).astype(bf16)


def _dwconv3(y, w, b, tm):
    return (y[HALO - 1:HALO - 1 + tm] * w[0:1] + y[HALO:HALO + tm] * w[1:2]
            + y[HALO + 1:HALO + 1 + tm] * w[2:3] + b)


def _gelu_tanh(x):
    return 0.5 * x * (1.0 + jnp.tanh(math.sqrt(2.0 / math.pi) * (x + 0.044715 * (x * x * x))))


def _row_specs(tm, d, t_rows):
    r = tm // HALO
    nb = t_rows // HALO
    return (pl.BlockSpec((tm, d), lambda i, *_: (i, 0)),
            pl.BlockSpec((HALO, d), lambda i, *_: (jnp.maximum(i * r - 1, 0), 0)),
            pl.BlockSpec((HALO, d), lambda i, *_: (jnp.minimum((i + 1) * r, nb - 1), 0)))


def _ada_kernel(c_ref, w_ref, b_ref, o_ref):
    c = c_ref[...]
    s = c * jax.nn.sigmoid(c)
    o_ref[0] = jnp.dot(s, w_ref[0], precision=HIGHEST, preferred_element_type=f32) + b_ref[0]


def _ada(cond, ada_w, ada_b):
    depth, d, n = ada_w.shape
    tn = n // 4
    rows = cond.shape[0]
    return pl.pallas_call(
        _ada_kernel,
        out_shape=jax.ShapeDtypeStruct((depth, rows, n), f32),
        grid=(depth, n // tn),
        in_specs=[pl.BlockSpec((rows, d), lambda l, j: (0, 0)),
                  pl.BlockSpec((1, d, tn), lambda l, j: (l, 0, j)),
                  pl.BlockSpec((1, 1, tn), lambda l, j: (l, 0, j))],
        out_specs=pl.BlockSpec((1, rows, tn), lambda l, j: (l, 0, j)),
        compiler_params=_cparams("arbitrary", "arbitrary"),
        name="ada_mod",
    )(cond, ada_w, ada_b.reshape(depth, 1, n))


def _even_front_kernel(x_ref, xp_ref, xn_ref, mod_ref, g_ref, win_ref, sw_ref, sb_ref, cw_ref, cb_ref,
                       a_ref, xbt_ref, *, tm, tpb, a_width):
    i = pl.program_id(0)
    mod = mod_ref[0]
    h = _halo_ext(x_ref, xp_ref, xn_ref, g_ref[...], mod[1:2], mod[0:1], (i % tpb) == 0, (i % tpb) == tpb - 1)
    p = jnp.dot(h, win_ref[...], preferred_element_type=f32)
    u = _gelu_tanh(p[HALO:HALO + tm, 0:a_width])
    v = _gelu_tanh(p[HALO:HALO + tm, a_width:2 * a_width]).astype(bf16)
    nch = tm // CHUNK
    a_ch = a_width // A_GROUPS
    for gi in range(A_GROUPS):
        cols = slice(gi * a_ch, (gi + 1) * a_ch)
        rhs = jnp.concatenate([v[n * CHUNK:(n + 1) * CHUNK, cols] for n in range(nch)], axis=1)
        s = jnp.dot(sw_ref[gi], rhs, preferred_element_type=f32)
        for n in range(nch):
            rows = slice(n * CHUNK, (n + 1) * CHUNK)
            sn = s[:, n * a_ch:(n + 1) * a_ch] + sb_ref[gi]
            a_ref[rows, cols] = (u[rows, cols] * sn).astype(bf16)
    xb = _dwconv3(p[:, 2 * a_width:], cw_ref[...], cb_ref[...], tm)
    xbt_ref[0] = xb.T


def _even_front(x, mod, g, w_in, sgu_w, sgu_b, conv_w, conv_b, *, nb, seq, tm):
    t_rows, d = x.shape
    tpb = seq // tm
    n_in = w_in.shape[1]
    a_width = A_GROUPS * sgu_w.shape[1]
    nxb = n_in - 2 * a_width
    mrows = seq if mod.shape[0] > 1 else t_rows
    kern = functools.partial(_even_front_kernel, tm=tm, tpb=tpb, a_width=a_width)
    full = lambda shape: pl.BlockSpec(shape, lambda i: (0,) * len(shape))
    return pl.pallas_call(
        kern,
        out_shape=(jax.ShapeDtypeStruct((t_rows, a_width), bf16),
                   jax.ShapeDtypeStruct((nb, nxb, seq), f32)),
        grid=(t_rows // tm,),
        in_specs=[*_row_specs(tm, d, t_rows),
                  pl.BlockSpec((1, 6, d), lambda i: ((i * tm) // mrows, 0, 0)),
                  full((1, d)), full(w_in.shape), full(sgu_w.shape), full(sgu_b.shape),
                  full(conv_w.shape), full(conv_b.shape)],
        out_specs=(pl.BlockSpec((tm, a_width), lambda i: (i, 0)),
                   pl.BlockSpec((1, nxb, tm), lambda i: (i // tpb, 0, i % tpb))),
        compiler_params=_cparams("arbitrary"),
        name="even_front",
    )(x, x, x, mod, g, w_in, sgu_w, sgu_b, conv_w, conv_b)


def _filter_kernel(w1t_ref, b1_ref, w2t_ref, b2_ref, fr_ref, w3t_ref, dec_ref, o_ref, *, seq):
    nfeat = w1t_ref.shape[1]
    t = lax.broadcasted_iota(jnp.int32, (nfeat, seq), 1).astype(f32)
    fi = lax.broadcasted_iota(jnp.int32, (nfeat, seq), 0)
    tn = t / seq
    band = jnp.where(fi <= FILTER_BANDS, fi, fi - FILTER_BANDS).astype(f32)
    ang = ((2.0 * math.pi) * tn) * band
    z = jnp.where(fi == 0, tn, jnp.where(fi <= FILTER_BANDS, jnp.sin(ang), jnp.cos(ang)))
    z = jnp.where(fi <= 2 * FILTER_BANDS, z, 0.0)
    fr = fr_ref[...]
    h = jnp.sin(fr * (jnp.dot(w1t_ref[...], z, precision=HIGHEST, preferred_element_type=f32) + b1_ref[...]))
    h = jnp.sin(fr * (jnp.dot(w2t_ref[...], h, precision=HIGHEST, preferred_element_type=f32) + b2_ref[...]))
    h = jnp.dot(w3t_ref[...], h, precision=HIGHEST, preferred_element_type=f32)
    tl = lax.broadcasted_iota(jnp.int32, (1, seq), 1).astype(f32)
    dist = jnp.abs(tl - (seq // 2)) / seq
    h = h * jnp.exp(-jnp.abs(dec_ref[...]) * dist)
    h = h / (jnp.sum(jnp.abs(h), axis=1, keepdims=True) + EPS)
    hb = h.astype(bf16).astype(f32)
    lane = lax.broadcasted_iota(jnp.int32, hb.shape, 1)
    rolled = pltpu.roll(hb, 1, 1)
    prev_bits = lax.bitcast_convert_type(rolled, jnp.int32)
    cur_bits = lax.shift_right_logical(lax.bitcast_convert_type(hb, jnp.int32), 16)
    o_ref[:, 0:seq] = cur_bits | jnp.where(lane == 0, 0, prev_bits)
    tail_lane = lax.broadcasted_iota(jnp.int32, (hb.shape[0], LANES), 1)
    wrapped = lax.bitcast_convert_type(rolled[:, 0:LANES], jnp.int32)
    o_ref[:, seq:seq + LANES] = jnp.where(tail_lane == 0, wrapped, 0)


def _hyena_filters_t(seq, w1, b1, w2, b2, w3, freq, decay):
    nfeat = 40
    hid = w1.shape[1]
    w1t = jnp.zeros((hid, nfeat), f32).at[:, :w1.shape[0]].set(w1.T)
    rows = w3.shape[1]
    rb = 256
    kern = functools.partial(_filter_kernel, seq=seq)
    full = lambda shape: pl.BlockSpec(shape, lambda i: (0,) * len(shape))
    return pl.pallas_call(
        kern,
        out_shape=jax.ShapeDtypeStruct((rows, seq + LANES), jnp.int32),
        grid=(rows // rb,),
        in_specs=[full((hid, nfeat)), full((hid, 1)), full((hid, hid)), full((hid, 1)), full((hid, 1)),
                  pl.BlockSpec((rb, hid), lambda i: (i, 0)), pl.BlockSpec((rb, 1), lambda i: (i, 0))],
        out_specs=pl.BlockSpec((rb, seq + LANES), lambda i: (i, 0)),
        compiler_params=_cparams("arbitrary"),
        name="hyena_filter",
    )(w1t, b1.reshape(hid, 1), w2.T, b2.reshape(hid, 1), freq.reshape(hid, 1), w3.T, decay.reshape(rows, 1))


def _bank_steps(hrow, bank_ref, seq):
    nseg = seq // LANES
    hl = LANES // 2
    upper = lax.broadcasted_iota(jnp.int32, (hl, LANES), 1) >= 2 * lax.broadcasted_iota(jnp.int32, (hl, LANES), 0)
    zero = jnp.zeros((hl, LANES), jnp.int32)
    bank_ref[0] = zero
    bank_ref[nseg + 2] = zero
    prev = zero
    for k in range(nseg + 1):
        seg = jnp.broadcast_to(hrow[:, k * LANES:(k + 1) * LANES], (hl, LANES))
        cur = pltpu.roll(seg, 0, 1, stride=2, stride_axis=0)
        bank_ref[k + 1] = jnp.where(upper, cur, prev)
        prev = cur
        yield


def _conv_steps(u, bank_ref, upad_ref, out, *, nb, seq):
    nseg = seq // LANES
    nblk = seq // TOEP
    half = seq // 2
    upad_ref[:, 0:half] = jnp.zeros((nb, half), f32)
    upad_ref[:, half:half + seq] = u
    upad_ref[:, half + seq:2 * seq] = jnp.zeros((nb, half), f32)
    acc = None
    for kt in range(nblk + 1):
        k0 = nseg - 2 * kt
        lhs = jnp.concatenate([upad_ref[:, (i + kt) * TOEP:(i + kt + 1) * TOEP] for i in range(nblk)], axis=0)
        blk = lambda idx: pltpu.bitcast(bank_ref[idx], bf16)
        top = jnp.concatenate([blk(k0 + 1), blk(k0 + 2)], axis=1)
        bot = jnp.concatenate([blk(k0), blk(k0 + 1)], axis=1)
        w = jnp.concatenate([top, bot], axis=0)
        part = jnp.dot(lhs.astype(bf16), w, preferred_element_type=f32)
        acc = part if acc is None else acc + part
        yield
    out.append(jnp.concatenate([acc[i * nb:(i + 1) * nb] for i in range(nblk)], axis=1))


def _interleave(main, side, ratio):
    main, side = list(main), list(side)
    while main or side:
        main = [g for g in main if next(g, StopIteration) is not StopIteration]
        for _ in range(ratio if main else 1 << 30):
            side = [g for g in side if next(g, StopIteration) is not StopIteration]
            if not side:
                break


def _hyena_kernel(d_ref, vb_ref, x1_ref, x2_ref, h0_ref, h1_ref, z_ref,
                  u_scr, a_scr, b_scr, o_scr, bank00, bank01, bank10, bank11, upad0, upad1, *, nb, seq, cg, nchan):
    gidx = pl.program_id(0)
    bank_ref = ((bank00, bank01), (bank10, bank11))
    upad_ref = (upad0, upad1)
    u_scr[...] = jnp.swapaxes(vb_ref[...], 0, 1)
    a_scr[...] = jnp.swapaxes(x1_ref[...], 0, 1)
    b_scr[...] = jnp.swapaxes(x2_ref[...], 0, 1)

    def banks(c):
        gens = []
        if c < cg:
            gens.append(_bank_steps(h0_ref[c:c + 1, :], bank_ref[c % 2][0], seq))
        if 1 <= c <= cg:
            gens.append(_bank_steps(h1_ref[c - 1:c, :], bank_ref[c % 2][1], seq))
        return gens

    ratio = -(-(seq // LANES + 1) // (seq // TOEP + 1))
    _interleave([], banks(0), ratio)
    z_prev = None
    for c in range(cg + 1):
        convs, first, second = [], [], []
        if c < cg:
            u = u_scr[c]
            convs.append(_conv_steps(u, bank_ref[c % 2][0], upad_ref[0], first, nb=nb, seq=seq))
        if c >= 1:
            convs.append(_conv_steps(z_prev, bank_ref[c % 2][1], upad_ref[1], second, nb=nb, seq=seq))
        _interleave(convs, banks(c + 1), ratio)
        if c >= 1:
            o_scr[c - 1] = b_scr[c - 1] * (second[0] + z_prev * d_ref[nchan + gidx * cg + c - 1])
        if c < cg:
            z_prev = a_scr[c] * (first[0] + u * d_ref[gidx * cg + c])
    z_ref[...] = jnp.swapaxes(o_scr[...], 0, 1)


def _hyena(xbt, filt_t, d_flat, *, nb, seq):
    nchan = xbt.shape[1] // 3
    cg = 8
    ng = nchan // cg
    kern = functools.partial(_hyena_kernel, nb=nb, seq=seq, cg=cg, nchan=nchan)
    act = lambda off: pl.BlockSpec((nb, cg, seq), lambda i: (0, off + i, 0))
    wseq = seq + LANES
    return pl.pallas_call(
        kern,
        out_shape=jax.ShapeDtypeStruct((nb, nchan, seq), f32),
        grid=(ng,),
        in_specs=[pl.BlockSpec(memory_space=pltpu.SMEM),
                  act(0), act(ng), act(2 * ng),
                  pl.BlockSpec((cg, wseq), lambda i: (i, 0)),
                  pl.BlockSpec((cg, wseq), lambda i: (ng + i, 0))],
        out_specs=pl.BlockSpec((nb, cg, seq), lambda i: (0, i, 0)),
        scratch_shapes=[pltpu.VMEM((cg, nb, seq), f32), pltpu.VMEM((cg, nb, seq), f32),
                        pltpu.VMEM((cg, nb, seq), f32), pltpu.VMEM((cg, nb, seq), f32),
                        *[pltpu.VMEM((seq // LANES + 3, LANES // 2, LANES), jnp.int32) for _ in range(4)],
                        pltpu.VMEM((nb, 2 * seq), f32), pltpu.VMEM((nb, 2 * seq), f32)],
        compiler_params=_cparams("arbitrary"),
        name="hyena_conv",
    )(d_flat, xbt, xbt, xbt, filt_t, filt_t)


def _even_out_kernel(x_ref, a_ref, zt_ref, mod_ref, wa_ref, wz_ref, o_ref):
    out = jnp.dot(a_ref[...], wa_ref[...], preferred_element_type=f32)
    out = out + lax.dot_general(zt_ref[0].astype(bf16), wz_ref[...], (((0,), (0,)), ((), ())),
                                preferred_element_type=f32)
    o_ref[...] = x_ref[...] + mod_ref[0][2:3] * out


def _even_out(x, a, zt, mod, w_a, w_z, *, seq, tm):
    t_rows, d = x.shape
    tpb = seq // tm
    mrows = seq if mod.shape[0] > 1 else t_rows
    full = lambda shape: pl.BlockSpec(shape, lambda i: (0,) * len(shape))
    return pl.pallas_call(
        _even_out_kernel,
        out_shape=jax.ShapeDtypeStruct((t_rows, d), f32),
        grid=(t_rows // tm,),
        in_specs=[pl.BlockSpec((tm, d), lambda i: (i, 0)),
                  pl.BlockSpec((tm, a.shape[1]), lambda i: (i, 0)),
                  pl.BlockSpec((1, zt.shape[1], tm), lambda i: (i // tpb, 0, i % tpb)),
                  pl.BlockSpec((1, 6, d), lambda i: ((i * tm) // mrows, 0, 0)),
                  full(w_a.shape), full(w_z.shape)],
        out_specs=pl.BlockSpec((tm, d), lambda i: (i, 0)),
        compiler_params=_cparams("arbitrary"),
        name="even_out",
    )(x, a, zt, mod, w_a, w_z)


def _rope128(x, cos, sin):
    lane = lax.broadcasted_iota(jnp.int32, x.shape, 1)
    swapped = jnp.where((lane & 16) == 0, pltpu.roll(x, LANES - 16, 1), pltpu.roll(x, 16, 1))
    return x * cos + swapped * sin


def _expand_kv(ckv, kr_pad, wuk_ref, wuv_ref, kg_ref, cos, sin, k_ref, v_ref):
    cb = ckv.astype(bf16)
    kn = jnp.dot(cb, wuk_ref[...], preferred_element_type=f32)
    v = jnp.dot(cb, wuv_ref[...], preferred_element_type=f32)
    v_ref[0] = v.T.astype(bf16)
    kg = kg_ref[...]
    g_nope, g_rope = kg[:, 0:NOPE], kg[:, NOPE:2 * NOPE]
    kr_ss = jnp.sum(kr_pad * kr_pad, axis=-1, keepdims=True)
    krg = kr_pad * g_rope
    if cos is not None:
        krg = _rope128(krg, cos, sin)
    for hd in range(HEADS):
        kh = kn[:, hd * NOPE:(hd + 1) * NOPE]
        r = lax.rsqrt((jnp.sum(kh * kh, axis=-1, keepdims=True) + kr_ss) / QK + EPS)
        k_ref[:, hd * HEAD_PAD:hd * HEAD_PAD + NOPE] = (kh * r * g_nope).astype(bf16)
        k_ref[:, hd * HEAD_PAD + NOPE:(hd + 1) * HEAD_PAD] = (krg * r).astype(bf16)


def _mla_front_kernel(*refs, rope, emit_cache):
    (x_ref, mod_ref, g_ref, wdq_ref, qn_ref, wuq_ref, qg_ref, wdkv_ref, kvn_ref, wuk_ref, wuv_ref, kg_ref) = refs[:12]
    refs = refs[12:]
    if rope:
        cos_ref, sin_ref = refs[:2]
        refs = refs[2:]
        cos, sin = cos_ref[...], sin_ref[...]
    else:
        cos = sin = None
    q_ref, k_ref, v_ref = refs[:3]
    mod = mod_ref[0]
    h = _norm_mod(x_ref[...], g_ref[...], mod[1:2], mod[0:1]).astype(bf16)
    ql = jnp.dot(h, wdq_ref[...], preferred_element_type=f32)
    ql = ql * lax.rsqrt(jnp.mean(ql * ql, axis=-1, keepdims=True) + EPS) * qn_ref[...]
    q = jnp.dot(ql.astype(bf16), wuq_ref[...], preferred_element_type=f32)
    qg = qg_ref[...]
    c_exp = (1.0 / math.sqrt(QK)) * math.log2(math.e)
    for hd in range(HEADS):
        qh = q[:, hd * HEAD_PAD:(hd + 1) * HEAD_PAD]
        r = lax.rsqrt(jnp.sum(qh * qh, axis=-1, keepdims=True) / QK + EPS) * c_exp
        qh = qh * r * qg
        if rope:
            q_ref[:, hd * HEAD_PAD:hd * HEAD_PAD + NOPE] = qh[:, 0:NOPE].astype(bf16)
            q_ref[:, hd * HEAD_PAD + NOPE:(hd + 1) * HEAD_PAD] = _rope128(qh[:, NOPE:], cos, sin).astype(bf16)
        else:
            q_ref[:, hd * HEAD_PAD:(hd + 1) * HEAD_PAD] = qh.astype(bf16)
    dkv = jnp.dot(h, wdkv_ref[...], preferred_element_type=f32)
    c_raw = dkv[:, 0:KV_RANK]
    ckv = c_raw * lax.rsqrt(jnp.mean(c_raw * c_raw, axis=-1, keepdims=True) + EPS) * kvn_ref[...]
    kr_pad = dkv[:, KV_RANK:]
    if emit_cache:
        ckv_ref, kr_ref = refs[3:5]
        ckv_ref[...] = ckv
        kr_ref[...] = kr_pad[:, 0:ROPE]
    _expand_kv(ckv, kr_pad, wuk_ref, wuv_ref, kg_ref, cos, sin, k_ref, v_ref)


def _mla_front(x, mod, g, w, tabs, *, nb, seq, tm, emit_cache):
    t_rows, d = x.shape
    tpb = seq // tm
    rope = tabs is not None
    mrows = seq if mod.shape[0] > 1 else t_rows
    kern = functools.partial(_mla_front_kernel, rope=rope, emit_cache=emit_cache)
    full = lambda a: pl.BlockSpec(a.shape, lambda i: (0,) * a.ndim)
    wnames = ("w_dq", "q_norm", "w_uq", "q_gain", "w_dkv", "kv_norm", "w_uk", "w_uv", "k_gain")
    ins = [x, mod, g] + [w[n] for n in wnames]
    in_specs = [pl.BlockSpec((tm, d), lambda i: (i, 0)),
                pl.BlockSpec((1, 6, d), lambda i: ((i * tm) // mrows, 0, 0)),
                full(g)] + [full(w[n]) for n in wnames]
    if rope:
        ins += list(tabs)
        in_specs += [pl.BlockSpec((tm, LANES), lambda i: (i % tpb, 0))] * 2
    hp = HEADS * HEAD_PAD
    out_shape = [jax.ShapeDtypeStruct((t_rows, hp), bf16), jax.ShapeDtypeStruct((t_rows, hp), bf16),
                 jax.ShapeDtypeStruct((nb, HEADS * V_DIM, seq), bf16)]
    out_specs = [pl.BlockSpec((tm, hp), lambda i: (i, 0)), pl.BlockSpec((tm, hp), lambda i: (i, 0)),
                 pl.BlockSpec((1, HEADS * V_DIM, tm), lambda i: (i // tpb, 0, i % tpb))]
    if emit_cache:
        out_shape += [jax.ShapeDtypeStruct((t_rows, KV_RANK), f32), jax.ShapeDtypeStruct((t_rows, ROPE), f32)]
        out_specs += [pl.BlockSpec((tm, KV_RANK), lambda i: (i, 0)), pl.BlockSpec((tm, ROPE), lambda i: (i, 0))]
    return pl.pallas_call(
        kern, out_shape=tuple(out_shape), grid=(t_rows // tm,), in_specs=in_specs, out_specs=tuple(out_specs),
        compiler_params=_cparams("arbitrary"), name="mla_front",
    )(*ins)


def _cache_expand_kernel(ckv_ref, kr_ref, wuk_ref, wuv_ref, kg_ref, k_ref, v_ref):
    _expand_kv(ckv_ref[...], kr_ref[...], wuk_ref, wuv_ref, kg_ref, None, None, k_ref, v_ref)


def _cache_expand(ckv, kr_pad, w, *, nb, seq):
    t_rows = ckv.shape[0]
    tm = seq
    full = lambda a: pl.BlockSpec(a.shape, lambda i: (0,) * a.ndim)
    hp = HEADS * HEAD_PAD
    return pl.pallas_call(
        _cache_expand_kernel,
        out_shape=(jax.ShapeDtypeStruct((t_rows, hp), bf16), jax.ShapeDtypeStruct((nb, HEADS * V_DIM, seq), bf16)),
        grid=(t_rows // tm,),
        in_specs=[pl.BlockSpec((tm, KV_RANK), lambda i: (i, 0)), pl.BlockSpec((tm, LANES), lambda i: (i, 0)),
                  full(w["w_uk"]), full(w["w_uv"]), full(w["k_gain"])],
        out_specs=(pl.BlockSpec((tm, hp), lambda i: (i, 0)),
                   pl.BlockSpec((1, HEADS * V_DIM, tm), lambda i: (i, 0, 0))),
        compiler_params=_cparams("arbitrary"), name="mla_cache_expand",
    )(ckv, kr_pad, w["w_uk"], w["w_uv"], w["k_gain"])


def _attn_kernel(*refs, cached, tk):
    if cached:
        q_ref, k_ref, kc_ref, vt_ref, vct_ref, o_ref, s_scr, m_scr = refs
        parts = ((k_ref, vt_ref), (kc_ref, vct_ref))
    else:
        q_ref, k_ref, vt_ref, o_ref, s_scr, m_scr = refs
        parts = ((k_ref, vt_ref),)
    nt = (((1,), (1,)), ((), ()))

    @pl.when(pl.program_id(0) == 0)
    def _():
        s_scr[...] = jnp.zeros_like(s_scr)
        m_scr[...] = jnp.zeros_like(m_scr)

    q = q_ref[...]
    m_old = m_scr[...]
    ones = jnp.ones((16, tk), bf16)
    m_new = acc = None
    base = 0
    for kr, vtr in parts:
        for j in range(kr.shape[0] // tk):
            rows = slice(base + j * tk, base + (j + 1) * tk)
            p = jnp.exp2(s_scr[rows, :] - m_old).astype(bf16)
            lhs = jnp.concatenate([vtr[0, :, j * tk:(j + 1) * tk], ones], axis=0)
            part = jnp.dot(lhs, p, preferred_element_type=f32)
            acc = part if acc is None else acc + part
            s = lax.dot_general(kr[j * tk:(j + 1) * tk, :], q, nt, preferred_element_type=f32)
            s_scr[rows, :] = s
            cm = jnp.max(s, axis=0, keepdims=True)
            m_new = cm if m_new is None else jnp.maximum(m_new, cm)
        base += kr.shape[0]
    m_scr[...] = m_new
    o_ref[0] = (acc[0:V_DIM] / acc[V_DIM:V_DIM + 1]).astype(bf16)


def _attention(q, k, vt, kc, vct, *, nb, seq, tq):
    cached = kc is not None
    nq = seq // tq
    n_tiles = nb * HEADS * nq
    past = kc.shape[0] // nb if cached else 0
    tk = min(seq, TOEP)

    def cur(n):
        n = jnp.minimum(n, n_tiles - 1)
        return n // (HEADS * nq), (n // nq) % HEADS, n % nq

    def prev(n):
        n = jnp.maximum(n - 1, 0)
        return n // (HEADS * nq), (n // nq) % HEADS, n % nq

    def spec(shape, fn):
        return pl.BlockSpec(shape, fn)

    kern = functools.partial(_attn_kernel, cached=cached, tk=tk)
    q_spec = spec((tq, HEAD_PAD), lambda n: (cur(n)[0] * nq + cur(n)[2], cur(n)[1]))
    k_spec = lambda rows: spec((rows, HEAD_PAD), lambda n: (cur(n)[0], cur(n)[1]))
    v_spec = lambda cols: spec((1, V_DIM, cols), lambda n: (prev(n)[0], prev(n)[1], 0))
    if cached:
        ins = [q, k, kc, vt, vct]
        in_specs = [q_spec, k_spec(seq), k_spec(past), v_spec(seq), v_spec(past)]
    else:
        ins = [q, k, vt]
        in_specs = [q_spec, k_spec(seq), v_spec(seq)]
    return pl.pallas_call(
        kern,
        out_shape=jax.ShapeDtypeStruct((nb, HEADS * V_DIM, seq), bf16),
        grid=(n_tiles + 1,),
        in_specs=in_specs,
        out_specs=pl.BlockSpec((1, V_DIM, tq), lambda n: prev(n)),
        scratch_shapes=[pltpu.VMEM((seq + past, tq), f32), pltpu.VMEM((1, tq), f32)],
        compiler_params=_cparams("arbitrary"),
        name="mla_attention",
    )(*ins)


def _attn_out_kernel(x_ref, at_ref, mod_ref, wo_ref, o_ref):
    out = lax.dot_general(at_ref[0], wo_ref[...], (((0,), (0,)), ((), ())), preferred_element_type=f32)
    o_ref[...] = x_ref[...] + mod_ref[0][2:3] * out


def _attn_out(x, at, mod, w_o, *, seq, tm):
    t_rows, d = x.shape
    tpb = seq // tm
    mrows = seq if mod.shape[0] > 1 else t_rows
    return pl.pallas_call(
        _attn_out_kernel,
        out_shape=jax.ShapeDtypeStruct((t_rows, d), f32),
        grid=(t_rows // tm,),
        in_specs=[pl.BlockSpec((tm, d), lambda i: (i, 0)),
                  pl.BlockSpec((1, at.shape[1], tm), lambda i: (i // tpb, 0, i % tpb)),
                  pl.BlockSpec((1, 6, d), lambda i: ((i * tm) // mrows, 0, 0)),
                  pl.BlockSpec(w_o.shape, lambda i: (0, 0))],
        out_specs=pl.BlockSpec((tm, d), lambda i: (i, 0)),
        compiler_params=_cparams("arbitrary"),
        name="mla_out",
    )(x, at, mod, w_o)


def _ffn_kernel(x_ref, xp_ref, xn_ref, mod_ref, g_ref, wup_ref, cw_ref, cb_ref, wd_ref, o_ref, *, tm, tpb, chunks):
    i = pl.program_id(0)
    mod = mod_ref[0]
    h = _halo_ext(x_ref, xp_ref, xn_ref, g_ref[...], mod[4:5], mod[3:4], (i % tpb) == 0, (i % tpb) == tpb - 1)
    dff = wd_ref.shape[0]
    acc = None
    for c0, c1 in chunks:
        halves = []
        for off in (c0, dff + c0):
            cols = slice(off, off + c1 - c0)
            y = jnp.dot(h, wup_ref[:, cols], preferred_element_type=f32)
            halves.append(_dwconv3(y, cw_ref[:, cols], cb_ref[:, cols], tm))
        gate, up = halves
        act = ((gate * jax.nn.sigmoid(gate)) * up).astype(bf16)
        part = jnp.dot(act, wd_ref[c0:c1, :], preferred_element_type=f32)
        acc = part if acc is None else acc + part
    o_ref[...] = x_ref[...] + mod[5:6] * acc


def _ffn(x, mod, g, w_up, conv_w, conv_b, w_down, *, seq, tm):
    t_rows, d = x.shape
    dff = w_down.shape[0]
    step = 6 * TOEP
    chunks = tuple((c, min(c + step, dff)) for c in range(0, dff, step))
    tpb = seq // tm
    mrows = seq if mod.shape[0] > 1 else t_rows
    kern = functools.partial(_ffn_kernel, tm=tm, tpb=tpb, chunks=chunks)
    resident = lambda a: pl.BlockSpec(a.shape, lambda i: (0,) * a.ndim, pipeline_mode=pl.Buffered(1))
    return pl.pallas_call(
        kern,
        out_shape=jax.ShapeDtypeStruct((t_rows, d), f32),
        grid=(t_rows // tm,),
        in_specs=[*_row_specs(tm, d, t_rows),
                  pl.BlockSpec((1, 6, d), lambda i: ((i * tm) // mrows, 0, 0)),
                  resident(g), resident(w_up), resident(conv_w), resident(conv_b), resident(w_down)],
        out_specs=pl.BlockSpec((tm, d), lambda i: (i, 0)),
        compiler_params=_cparams("arbitrary"),
        name="conv_ffn",
    )(x, x, x, mod, g, w_up, conv_w, conv_b, w_down)


def _rope_tables(seq):
    half = ROPE // 2
    rows = seq // GRID_W
    row = jnp.repeat(jnp.arange(rows), GRID_W)
    col = jnp.tile(jnp.arange(GRID_W), rows)
    inv = 1.0 / (ROPE_BASE ** (jnp.arange(0, half, 2, dtype=f32) / half))
    ar = row.astype(f32)[:, None] * inv[None]
    ac = col.astype(f32)[:, None] * inv[None]
    pad = jnp.zeros((seq, LANES - ROPE), f32)
    cos = jnp.concatenate([jnp.cos(ar), jnp.cos(ar), jnp.cos(ac), jnp.cos(ac), pad], axis=1)
    sin = jnp.concatenate([-jnp.sin(ar), jnp.sin(ar), -jnp.sin(ac), jnp.sin(ac), pad], axis=1)
    return cos, sin


def _pad_heads(a, width):
    lead = a.shape[:-1]
    a = a.reshape(*lead, HEADS, -1)
    a = jnp.pad(a, [(0, 0)] * len(lead) + [(0, 0), (0, width - a.shape[-1])])
    return a.reshape(*lead, HEADS * width)


def _mla_weights(j, mla_w_dq, mla_q_norm, mla_w_uq, mla_w_dkv, mla_kv_norm, mla_w_ukv, mla_q_head_norm,
                 mla_k_head_norm):
    ukv = mla_w_ukv[j].reshape(KV_RANK, HEADS, NOPE + V_DIM)
    kg = mla_k_head_norm[j]
    return {
        "w_dq": mla_w_dq[j].astype(bf16),
        "q_norm": mla_q_norm[j].reshape(1, -1),
        "w_uq": _pad_heads(mla_w_uq[j], HEAD_PAD).astype(bf16),
        "q_gain": jnp.pad(mla_q_head_norm[j], (0, HEAD_PAD - QK)).reshape(1, HEAD_PAD),
        "w_dkv": jnp.pad(mla_w_dkv[j], ((0, 0), (0, LANES - ROPE))).astype(bf16),
        "kv_norm": mla_kv_norm[j].reshape(1, -1),
        "w_uk": ukv[:, :, :NOPE].reshape(KV_RANK, HEADS * NOPE).astype(bf16),
        "w_uv": ukv[:, :, NOPE:].reshape(KV_RANK, HEADS * V_DIM).astype(bf16),
        "k_gain": jnp.pad(kg, (0, 2 * NOPE - QK)).reshape(1, 2 * NOPE),
    }


def kernel(x_prompt, x_sample, cache_ckv, cache_krope, c, c_ctx, ada_w, ada_b, norm_g, mix_w_in, sgu_w, sgu_b, hy_conv_w, hy_conv_b, hy_f_w1, hy_f_b1, hy_f_w2, hy_f_b2, hy_f_w3, hy_f_freq, hy_decay, hy_d, mix_w_out, mla_w_dq, mla_q_norm, mla_w_uq, mla_w_dkv, mla_kv_norm, mla_w_ukv, mla_q_head_norm, mla_k_head_norm, mla_w_o, ffn_w_up, ffn_conv_w, ffn_conv_b, ffn_w_down):
    depth = ada_w.shape[0]
    d = x_prompt.shape[-1]
    nbp, seqp, _ = x_prompt.shape
    nbs, seqs, _ = x_sample.shape
    a_width = A_GROUPS * sgu_w.shape[-1]

    cond = jnp.zeros((16, d), f32).at[0].set(c_ctx).at[1:1 + nbs].set(c)
    mods = _ada(cond, ada_w, ada_b).reshape(depth, 16, 6, d)

    w_in = mix_w_in.astype(bf16)
    sgu_wb = sgu_w.astype(bf16)
    sgu_bb = jnp.broadcast_to(sgu_b[..., None], sgu_b.shape + (sgu_w.shape[-1],))
    w_out = mix_w_out.astype(bf16)
    w_up = ffn_w_up.astype(bf16)
    w_down = ffn_w_down.astype(bf16)
    w_o = mla_w_o.astype(bf16)
    mla_w = [_mla_weights(j, mla_w_dq, mla_q_norm, mla_w_uq, mla_w_dkv, mla_kv_norm, mla_w_ukv,
                          mla_q_head_norm, mla_k_head_norm) for j in range(depth // 2)]

    def trunk(x3, mod_all, cache):
        nb, seq, _ = x3.shape
        x = x3.reshape(nb * seq, d)
        tm = min(seq, 512)
        latent = cache is not None
        tabs = _rope_tables(seq) if latent else None
        new_ckv, new_kr = [], []
        for l in range(depth):
            mod = mod_all[l]
            g1 = norm_g[l, 0].reshape(1, d)
            g2 = norm_g[l, 1].reshape(1, d)
            if l % 2 == 0:
                i = l // 2
                a, xbt = _even_front(x, mod, g1, w_in[i], sgu_wb[i], sgu_bb[i], hy_conv_w[i],
                                     hy_conv_b[i].reshape(1, -1), nb=nb, seq=seq, tm=tm)
                filt = _hyena_filters_t(seq, hy_f_w1[i], hy_f_b1[i], hy_f_w2[i], hy_f_b2[i], hy_f_w3[i],
                                        hy_f_freq[i], hy_decay[i])
                zt = _hyena(xbt, filt, hy_d[i].reshape(-1), nb=nb, seq=seq)
                x = _even_out(x, a, zt, mod, w_out[i, :a_width], w_out[i, a_width:], seq=seq, tm=tm)
            else:
                j = l // 2
                outs = _mla_front(x, mod, g1, mla_w[j], tabs, nb=nb, seq=seq, tm=tm, emit_cache=not latent)
                q, k, v = outs[:3]
                if latent:
                    past = cache[0].shape[2]
                    ckv_c = cache[0][:, j].reshape(nb * past, KV_RANK)
                    kr_c = jnp.pad(cache[1][:, j].reshape(nb * past, ROPE), ((0, 0), (0, LANES - ROPE)))
                    kc, vc = _cache_expand(ckv_c, kr_c, mla_w[j], nb=nb, seq=past)
                else:
                    kc = vc = None
                    new_ckv.append(outs[3].reshape(nb, seq, KV_RANK))
                    new_kr.append(outs[4].reshape(nb, seq, ROPE))
                at = _attention(q, k, v, kc, vc, nb=nb, seq=seq, tq=min(seq, 1024))
                x = _attn_out(x, at, mod, w_o[j], seq=seq, tm=tm)
            x = _ffn(x, mod, g2, w_up[l], ffn_conv_w[l], ffn_conv_b[l].reshape(1, -1), w_down[l], seq=seq, tm=tm)
        return x.reshape(nb, seq, d), new_ckv, new_kr

    y_prompt, ckv_list, kr_list = trunk(x_prompt, mods[:, 0:1], None)
    y_sample, _, _ = trunk(x_sample, mods[:, 1:1 + nbs], (cache_ckv, cache_krope))
    return (y_prompt, y_sample, jnp.stack(ckv_list, axis=1), jnp.stack(kr_list, axis=1))
```

```python
import functools
import math

import jax
import jax.numpy as jnp
from jax import lax
from jax.experimental import pallas as pl
from jax.experimental.pallas import tpu as pltpu

f32 = jnp.float32
bf16 = jnp.bfloat16

EPS = 1e-6
GRID_W = 64
CHUNK = 128
A_GROUPS = 4
HEADS = 8
NOPE = 128
ROPE = 64
QK = NOPE + ROPE
HEAD_PAD = 256
V_DIM = 128
KV_RANK = 256
ROPE_BASE = 10000.0
FILTER_BANDS = 16
LANES = 128
HALO = 8
TOEP = 256
VMEM_LIMIT = 56 * 1024 * 1024
HIGHEST = lax.Precision.HIGHEST


def _cparams(*sem):
    return pltpu.CompilerParams(dimension_semantics=sem, vmem_limit_bytes=VMEM_LIMIT)


def _norm_mod(x, g, sc, sh):
    y = x * lax.rsqrt(jnp.mean(x * x, axis=-1, keepdims=True) + EPS)
    return (y * g) * (1.0 + sc) + sh


def _halo_ext(x, xp, xn, g, sc, sh, first, last):
    hp = jnp.where(first, 0.0, _norm_mod(xp, g, sc, sh))
    hn = jnp.where(last, 0.0, _norm_mod(xn, g, sc, sh))
    h = _norm_mod(x, g, sc, sh)
    return jnp.concatenate([hp, h, hn], axis=0).astype(bf16)


def _dwconv3(y, w, b, tm):
    return (y[HALO - 1:HALO - 1 + tm] * w[0:1] + y[HALO:HALO + tm] * w[1:2]
            + y[HALO + 1:HALO + 1 + tm] * w[2:3] + b)


def _gelu_tanh(x):
    return 0.5 * x * (1.0 + jnp.tanh(math.sqrt(2.0 / math.pi) * (x + 0.044715 * (x * x * x))))


def _row_specs(tm, d, t_rows):
    r = tm // HALO
    nb = t_rows // HALO
    return (pl.BlockSpec((tm, d), lambda i, *_: (i, 0)),
            pl.BlockSpec((HALO, d), lambda i, *_: (jnp.maximum(i * r - 1, 0), 0)),
            pl.BlockSpec((HALO, d), lambda i, *_: (jnp.minimum((i + 1) * r, nb - 1), 0)))


def _ada_kernel(c_ref, w_ref, b_ref, o_ref):
    c = c_ref[...]
    s = c * jax.nn.sigmoid(c)
    o_ref[0] = jnp.dot(s, w_ref[0], precision=HIGHEST, preferred_element_type=f32) + b_ref[0]


def _ada(cond, ada_w, ada_b):
    depth, d, n = ada_w.shape
    tn = n // 4
    rows = cond.shape[0]
    return pl.pallas_call(
        _ada_kernel,
        out_shape=jax.ShapeDtypeStruct((depth, rows, n), f32),
        grid=(depth, n // tn),
        in_specs=[pl.BlockSpec((rows, d), lambda l, j: (0, 0)),
                  pl.BlockSpec((1, d, tn), lambda l, j: (l, 0, j)),
                  pl.BlockSpec((1, 1, tn), lambda l, j: (l, 0, j))],
        out_specs=pl.BlockSpec((1, rows, tn), lambda l, j: (l, 0, j)),
        compiler_params=_cparams("arbitrary", "arbitrary"),
        name="ada_mod",
    )(cond, ada_w, ada_b.reshape(depth, 1, n))


def _even_front_kernel(x_ref, xp_ref, xn_ref, mod_ref, g_ref, win_ref, sw_ref, sb_ref, cw_ref, cb_ref,
                       a_ref, xbt_ref, *, tm, tpb, a_width):
    i = pl.program_id(0)
    mod = mod_ref[0]
    h = _halo_ext(x_ref[...], xp_ref[...], xn_ref[...], g_ref[...], mod[1:2], mod[0:1],
                  (i % tpb) == 0, (i % tpb) == tpb - 1)
    p = jnp.dot(h, win_ref[...], preferred_element_type=f32)
    u = _gelu_tanh(p[HALO:HALO + tm, 0:a_width])
    v = _gelu_tanh(p[HALO:HALO + tm, a_width:2 * a_width]).astype(bf16)
    nch = tm // CHUNK
    a_ch = a_width // A_GROUPS
    for gi in range(A_GROUPS):
        cols = slice(gi * a_ch, (gi + 1) * a_ch)
        rhs = jnp.concatenate([v[n * CHUNK:(n + 1) * CHUNK, cols] for n in range(nch)], axis=1)
        s = jnp.dot(sw_ref[gi], rhs, preferred_element_type=f32)
        for n in range(nch):
            rows = slice(n * CHUNK, (n + 1) * CHUNK)
            sn = s[:, n * a_ch:(n + 1) * a_ch] + sb_ref[gi]
            a_ref[rows, cols] = (u[rows, cols] * sn).astype(bf16)
    xb = _dwconv3(p[:, 2 * a_width:], cw_ref[...], cb_ref[...], tm)
    xbt_ref[0] = xb.T


def _even_front(x, mod, g, w_in, sgu_w, sgu_b, conv_w, conv_b, *, nb, seq, tm):
    t_rows, d = x.shape
    tpb = seq // tm
    n_in = w_in.shape[1]
    a_width = A_GROUPS * sgu_w.shape[1]
    nxb = n_in - 2 * a_width
    mrows = seq if mod.shape[0] > 1 else t_rows
    kern = functools.partial(_even_front_kernel, tm=tm, tpb=tpb, a_width=a_width)
    full = lambda shape: pl.BlockSpec(shape, lambda i: (0,) * len(shape))
    return pl.pallas_call(
        kern,
        out_shape=(jax.ShapeDtypeStruct((t_rows, a_width), bf16),
                   jax.ShapeDtypeStruct((nb, nxb, seq), f32)),
        grid=(t_rows // tm,),
        in_specs=[*_row_specs(tm, d, t_rows),
                  pl.BlockSpec((1, 6, d), lambda i: ((i * tm) // mrows, 0, 0)),
                  full((1, d)), full(w_in.shape), full(sgu_w.shape), full(sgu_b.shape),
                  full(conv_w.shape), full(conv_b.shape)],
        out_specs=(pl.BlockSpec((tm, a_width), lambda i: (i, 0)),
                   pl.BlockSpec((1, nxb, tm), lambda i: (i // tpb, 0, i % tpb))),
        compiler_params=_cparams("arbitrary"),
        name="even_front",
    )(x, x, x, mod, g, w_in, sgu_w, sgu_b, conv_w, conv_b)


def _filter_kernel(w1t_ref, b1_ref, w2t_ref, b2_ref, fr_ref, w3t_ref, dec_ref, o_ref, hid_scr, *, seq):
    @pl.when(pl.program_id(0) == 0)
    def _():
        nfeat = w1t_ref.shape[1]
        t = lax.broadcasted_iota(jnp.int32, (nfeat, seq), 1).astype(f32)
        fi = lax.broadcasted_iota(jnp.int32, (nfeat, seq), 0)
        tn = t / seq
        band = jnp.where(fi <= FILTER_BANDS, fi, fi - FILTER_BANDS).astype(f32)
        ang = ((2.0 * math.pi) * tn) * band
        z = jnp.where(fi == 0, tn, jnp.where(fi <= FILTER_BANDS, jnp.sin(ang), jnp.cos(ang)))
        z = jnp.where(fi <= 2 * FILTER_BANDS, z, 0.0)
        fr = fr_ref[...]
        h1 = jnp.sin(fr * (jnp.dot(w1t_ref[...], z, precision=HIGHEST, preferred_element_type=f32) + b1_ref[...]))
        hid_scr[...] = jnp.sin(fr * (jnp.dot(w2t_ref[...], h1, precision=HIGHEST, preferred_element_type=f32)
                                     + b2_ref[...]))

    h = jnp.dot(w3t_ref[...], hid_scr[...], precision=HIGHEST, preferred_element_type=f32)
    tl = lax.broadcasted_iota(jnp.int32, (1, seq), 1).astype(f32)
    dist = jnp.abs(tl - (seq // 2)) / seq
    h = h * jnp.exp(-jnp.abs(dec_ref[...]) * dist)
    h = h / (jnp.sum(jnp.abs(h), axis=1, keepdims=True) + EPS)
    hb = h.astype(bf16).astype(f32)
    lane = lax.broadcasted_iota(jnp.int32, hb.shape, 1)
    rolled = pltpu.roll(hb, 1, 1)
    prev_bits = lax.bitcast_convert_type(rolled, jnp.int32)
    cur_bits = lax.shift_right_logical(lax.bitcast_convert_type(hb, jnp.int32), 16)
    o_ref[:, 0:seq] = cur_bits | jnp.where(lane == 0, 0, prev_bits)
    tail_lane = lax.broadcasted_iota(jnp.int32, (hb.shape[0], LANES), 1)
    wrapped = lax.bitcast_convert_type(rolled[:, 0:LANES], jnp.int32)
    o_ref[:, seq:seq + LANES] = jnp.where(tail_lane == 0, wrapped, 0)


def _hyena_filters_t(seq, w1, b1, w2, b2, w3, freq, decay):
    nfeat = 40
    hid = w1.shape[1]
    w1t = jnp.zeros((hid, nfeat), f32).at[:, :w1.shape[0]].set(w1.T)
    rows = w3.shape[1]
    rb = 256
    kern = functools.partial(_filter_kernel, seq=seq)
    full = lambda shape: pl.BlockSpec(shape, lambda i: (0,) * len(shape))
    return pl.pallas_call(
        kern,
        out_shape=jax.ShapeDtypeStruct((rows, seq + LANES), jnp.int32),
        grid=(rows // rb,),
        in_specs=[full((hid, nfeat)), full((hid, 1)), full((hid, hid)), full((hid, 1)), full((hid, 1)),
                  pl.BlockSpec((rb, hid), lambda i: (i, 0)), pl.BlockSpec((rb, 1), lambda i: (i, 0))],
        out_specs=pl.BlockSpec((rb, seq + LANES), lambda i: (i, 0)),
        scratch_shapes=[pltpu.VMEM((hid, seq), f32)],
        compiler_params=_cparams("arbitrary"),
        name="hyena_filter",
    )(w1t, b1.reshape(hid, 1), w2.T, b2.reshape(hid, 1), freq.reshape(hid, 1), w3.T, decay.reshape(rows, 1))


def _bank_steps(hrow, bank_ref, seq):
    nseg = seq // LANES
    hl = LANES // 2
    upper = lax.broadcasted_iota(jnp.int32, (hl, LANES), 1) >= 2 * lax.broadcasted_iota(jnp.int32, (hl, LANES), 0)
    zero = jnp.zeros((hl, LANES), jnp.int32)
    bank_ref[0] = zero
    bank_ref[nseg + 2] = zero
    prev = zero
    for k in range(nseg + 1):
        seg = jnp.broadcast_to(hrow[:, k * LANES:(k + 1) * LANES], (hl, LANES))
        cur = pltpu.roll(seg, 0, 1, stride=2, stride_axis=0)
        bank_ref[k + 1] = jnp.where(upper, cur, prev)
        prev = cur
        yield


def _conv_steps(u, bank_ref, upad_ref, out, *, nb, seq):
    nseg = seq // LANES
    nblk = seq // TOEP
    half = seq // 2
    upad_ref[:, 0:half] = jnp.zeros((nb, half), f32)
    upad_ref[:, half:half + seq] = u
    upad_ref[:, half + seq:2 * seq] = jnp.zeros((nb, half), f32)
    acc = None
    for kt in range(nblk + 1):
        k0 = nseg - 2 * kt
        lhs = jnp.concatenate([upad_ref[:, (i + kt) * TOEP:(i + kt + 1) * TOEP] for i in range(nblk)], axis=0)
        blk = lambda idx: pltpu.bitcast(bank_ref[idx], bf16)
        top = jnp.concatenate([blk(k0 + 1), blk(k0 + 2)], axis=1)
        bot = jnp.concatenate([blk(k0), blk(k0 + 1)], axis=1)
        w = jnp.concatenate([top, bot], axis=0)
        part = jnp.dot(lhs.astype(bf16), w, preferred_element_type=f32)
        acc = part if acc is None else acc + part
        yield
    out.append(jnp.concatenate([acc[i * nb:(i + 1) * nb] for i in range(nblk)], axis=1))


def _interleave(main, side, ratio):
    main, side = list(main), list(side)
    while main or side:
        main = [g for g in main if next(g, StopIteration) is not StopIteration]
        for _ in range(ratio if main else 1 << 30):
            side = [g for g in side if next(g, StopIteration) is not StopIteration]
            if not side:
                break


def _hyena_kernel(d_ref, vb_ref, x1_ref, x2_ref, h0_ref, h1_ref, z_ref,
                  u_scr, a_scr, b_scr, o_scr, bank00, bank01, bank10, bank11, upad0, upad1, *, nb, seq, cg, nchan):
    gidx = pl.program_id(0)
    bank_ref = ((bank00, bank01), (bank10, bank11))
    upad_ref = (upad0, upad1)
    u_scr[...] = jnp.swapaxes(vb_ref[...], 0, 1)
    a_scr[...] = jnp.swapaxes(x1_ref[...], 0, 1)
    b_scr[...] = jnp.swapaxes(x2_ref[...], 0, 1)

    def banks(c):
        gens = []
        if c < cg:
            gens.append(_bank_steps(h0_ref[c:c + 1, :], bank_ref[c % 2][0], seq))
        if 1 <= c <= cg:
            gens.append(_bank_steps(h1_ref[c - 1:c, :], bank_ref[c % 2][1], seq))
        return gens

    ratio = -(-(seq // LANES + 1) // (seq // TOEP + 1))
    _interleave([], banks(0), ratio)
    z_prev = None
    for c in range(cg + 1):
        convs, first, second = [], [], []
        if c < cg:
            u = u_scr[c]
            convs.append(_conv_steps(u, bank_ref[c % 2][0], upad_ref[0], first, nb=nb, seq=seq))
        if c >= 1:
            convs.append(_conv_steps(z_prev, bank_ref[c % 2][1], upad_ref[1], second, nb=nb, seq=seq))
        _interleave(convs, banks(c + 1), ratio)
        if c >= 1:
            o_scr[c - 1] = b_scr[c - 1] * (second[0] + z_prev * d_ref[nchan + gidx * cg + c - 1])
        if c < cg:
            z_prev = a_scr[c] * (first[0] + u * d_ref[gidx * cg + c])
    z_ref[...] = jnp.swapaxes(o_scr[...], 0, 1)


def _hyena(xbt, filt_t, d_flat, *, nb, seq):
    nchan = xbt.shape[1] // 3
    cg = 8
    ng = nchan // cg
    kern = functools.partial(_hyena_kernel, nb=nb, seq=seq, cg=cg, nchan=nchan)
    act = lambda off: pl.BlockSpec((nb, cg, seq), lambda i: (0, off + i, 0))
    wseq = seq + LANES
    return pl.pallas_call(
        kern,
        out_shape=jax.ShapeDtypeStruct((nb, nchan, seq), f32),
        grid=(ng,),
        in_specs=[pl.BlockSpec(memory_space=pltpu.SMEM),
                  act(0), act(ng), act(2 * ng),
                  pl.BlockSpec((cg, wseq), lambda i: (i, 0)),
                  pl.BlockSpec((cg, wseq), lambda i: (ng + i, 0))],
        out_specs=pl.BlockSpec((nb, cg, seq), lambda i: (0, i, 0)),
        scratch_shapes=[pltpu.VMEM((cg, nb, seq), f32), pltpu.VMEM((cg, nb, seq), f32),
                        pltpu.VMEM((cg, nb, seq), f32), pltpu.VMEM((cg, nb, seq), f32),
                        *[pltpu.VMEM((seq // LANES + 3, LANES // 2, LANES), jnp.int32) for _ in range(4)],
                        pltpu.VMEM((nb, 2 * seq), f32), pltpu.VMEM((nb, 2 * seq), f32)],
        compiler_params=_cparams("arbitrary"),
        name="hyena_conv",
    )(d_flat, xbt, xbt, xbt, filt_t, filt_t)


def _rope128(x, cos, sin):
    lane = lax.broadcasted_iota(jnp.int32, x.shape, 1)
    swapped = jnp.where((lane & 16) == 0, pltpu.roll(x, LANES - 16, 1), pltpu.roll(x, 16, 1))
    return x * cos + swapped * sin


def _expand_kv(ckv, kr_pad, wuk_ref, wuv_ref, kg_ref, cos, sin, k_ref, v_ref):
    cb = ckv.astype(bf16)
    kn = jnp.dot(cb, wuk_ref[...], preferred_element_type=f32)
    v = jnp.dot(cb, wuv_ref[...], preferred_element_type=f32)
    v_ref[0] = v.T.astype(bf16)
    kg = kg_ref[...]
    g_nope, g_rope = kg[:, 0:NOPE], kg[:, NOPE:2 * NOPE]
    kr_ss = jnp.sum(kr_pad * kr_pad, axis=-1, keepdims=True)
    krg = kr_pad * g_rope
    if cos is not None:
        krg = _rope128(krg, cos, sin)
    for hd in range(HEADS):
        kh = kn[:, hd * NOPE:(hd + 1) * NOPE]
        r = lax.rsqrt((jnp.sum(kh * kh, axis=-1, keepdims=True) + kr_ss) / QK + EPS)
        k_ref[:, hd * HEAD_PAD:hd * HEAD_PAD + NOPE] = (kh * r * g_nope).astype(bf16)
        k_ref[:, hd * HEAD_PAD + NOPE:(hd + 1) * HEAD_PAD] = (krg * r).astype(bf16)


def _mla_front_kernel(*refs, rope, emit_cache):
    (x_ref, mod_ref, g_ref, wdq_ref, qn_ref, wuq_ref, qg_ref, wdkv_ref, kvn_ref, wuk_ref, wuv_ref, kg_ref) = refs[:12]
    refs = refs[12:]
    if rope:
        cos_ref, sin_ref = refs[:2]
        refs = refs[2:]
        cos, sin = cos_ref[...], sin_ref[...]
    else:
        cos = sin = None
    q_ref, k_ref, v_ref = refs[:3]
    mod = mod_ref[0]
    h = _norm_mod(x_ref[...], g_ref[...], mod[1:2], mod[0:1]).astype(bf16)
    ql = jnp.dot(h, wdq_ref[...], preferred_element_type=f32)
    ql = ql * lax.rsqrt(jnp.mean(ql * ql, axis=-1, keepdims=True) + EPS) * qn_ref[...]
    q = jnp.dot(ql.astype(bf16), wuq_ref[...], preferred_element_type=f32)
    qg = qg_ref[...]
    c_exp = (1.0 / math.sqrt(QK)) * math.log2(math.e)
    for hd in range(HEADS):
        qh = q[:, hd * HEAD_PAD:(hd + 1) * HEAD_PAD]
        r = lax.rsqrt(jnp.sum(qh * qh, axis=-1, keepdims=True) / QK + EPS) * c_exp
        qh = qh * r * qg
        if rope:
            q_ref[:, hd * HEAD_PAD:hd * HEAD_PAD + NOPE] = qh[:, 0:NOPE].astype(bf16)
            q_ref[:, hd * HEAD_PAD + NOPE:(hd + 1) * HEAD_PAD] = _rope128(qh[:, NOPE:], cos, sin).astype(bf16)
        else:
            q_ref[:, hd * HEAD_PAD:(hd + 1) * HEAD_PAD] = qh.astype(bf16)
    dkv = jnp.dot(h, wdkv_ref[...], preferred_element_type=f32)
    c_raw = dkv[:, 0:KV_RANK]
    ckv = c_raw * lax.rsqrt(jnp.mean(c_raw * c_raw, axis=-1, keepdims=True) + EPS) * kvn_ref[...]
    kr_pad = dkv[:, KV_RANK:]
    if emit_cache:
        ckv_ref, kr_ref = refs[3:5]
        ckv_ref[...] = ckv
        kr_ref[...] = kr_pad[:, 0:ROPE]
    _expand_kv(ckv, kr_pad, wuk_ref, wuv_ref, kg_ref, cos, sin, k_ref, v_ref)


def _mla_front(x, mod, g, w, tabs, *, nb, seq, tm, emit_cache):
    t_rows, d = x.shape
    tpb = seq // tm
    rope = tabs is not None
    mrows = seq if mod.shape[0] > 1 else t_rows
    kern = functools.partial(_mla_front_kernel, rope=rope, emit_cache=emit_cache)
    full = lambda a: pl.BlockSpec(a.shape, lambda i: (0,) * a.ndim)
    wnames = ("w_dq", "q_norm", "w_uq", "q_gain", "w_dkv", "kv_norm", "w_uk", "w_uv", "k_gain")
    ins = [x, mod, g] + [w[n] for n in wnames]
    in_specs = [pl.BlockSpec((tm, d), lambda i: (i, 0)),
                pl.BlockSpec((1, 6, d), lambda i: ((i * tm) // mrows, 0, 0)),
                full(g)] + [full(w[n]) for n in wnames]
    if rope:
        ins += list(tabs)
        in_specs += [pl.BlockSpec((tm, LANES), lambda i: (i % tpb, 0))] * 2
    hp = HEADS * HEAD_PAD
    out_shape = [jax.ShapeDtypeStruct((t_rows, hp), bf16), jax.ShapeDtypeStruct((t_rows, hp), bf16),
                 jax.ShapeDtypeStruct((nb, HEADS * V_DIM, seq), bf16)]
    out_specs = [pl.BlockSpec((tm, hp), lambda i: (i, 0)), pl.BlockSpec((tm, hp), lambda i: (i, 0)),
                 pl.BlockSpec((1, HEADS * V_DIM, tm), lambda i: (i // tpb, 0, i % tpb))]
    if emit_cache:
        out_shape += [jax.ShapeDtypeStruct((t_rows, KV_RANK), f32), jax.ShapeDtypeStruct((t_rows, ROPE), f32)]
        out_specs += [pl.BlockSpec((tm, KV_RANK), lambda i: (i, 0)), pl.BlockSpec((tm, ROPE), lambda i: (i, 0))]
    return pl.pallas_call(
        kern, out_shape=tuple(out_shape), grid=(t_rows // tm,), in_specs=in_specs, out_specs=tuple(out_specs),
        compiler_params=_cparams("arbitrary"), name="mla_front",
    )(*ins)


def _cache_expand_kernel(ckv_ref, kr_ref, wuk_ref, wuv_ref, kg_ref, k_ref, v_ref):
    _expand_kv(ckv_ref[...], kr_ref[...], wuk_ref, wuv_ref, kg_ref, None, None, k_ref, v_ref)


def _cache_expand(ckv, kr_pad, w, *, nb, seq):
    t_rows = ckv.shape[0]
    tm = seq
    full = lambda a: pl.BlockSpec(a.shape, lambda i: (0,) * a.ndim)
    hp = HEADS * HEAD_PAD
    return pl.pallas_call(
        _cache_expand_kernel,
        out_shape=(jax.ShapeDtypeStruct((t_rows, hp), bf16), jax.ShapeDtypeStruct((nb, HEADS * V_DIM, seq), bf16)),
        grid=(t_rows // tm,),
        in_specs=[pl.BlockSpec((tm, KV_RANK), lambda i: (i, 0)), pl.BlockSpec((tm, LANES), lambda i: (i, 0)),
                  full(w["w_uk"]), full(w["w_uv"]), full(w["k_gain"])],
        out_specs=(pl.BlockSpec((tm, hp), lambda i: (i, 0)),
                   pl.BlockSpec((1, HEADS * V_DIM, tm), lambda i: (i, 0, 0))),
        compiler_params=_cparams("arbitrary"), name="mla_cache_expand",
    )(ckv, kr_pad, w["w_uk"], w["w_uv"], w["k_gain"])


def _attn_kernel(*refs, cached, tk):
    if cached:
        q_ref, k_ref, kc_ref, vt_ref, vct_ref, o_ref, s_scr, m_scr = refs
        parts = ((k_ref, vt_ref), (kc_ref, vct_ref))
    else:
        q_ref, k_ref, vt_ref, o_ref, s_scr, m_scr = refs
        parts = ((k_ref, vt_ref),)
    nt = (((1,), (1,)), ((), ()))

    @pl.when(pl.program_id(0) == 0)
    def _():
        s_scr[...] = jnp.zeros_like(s_scr)
        m_scr[...] = jnp.zeros_like(m_scr)

    q = q_ref[...]
    m_old = m_scr[...]
    ones = jnp.ones((16, tk), bf16)
    m_new = acc = None
    base = 0
    for kr, vtr in parts:
        for j in range(kr.shape[0] // tk):
            rows = slice(base + j * tk, base + (j + 1) * tk)
            p = jnp.exp2(s_scr[rows, :] - m_old).astype(bf16)
            lhs = jnp.concatenate([vtr[0, :, j * tk:(j + 1) * tk], ones], axis=0)
            part = jnp.dot(lhs, p, preferred_element_type=f32)
            acc = part if acc is None else acc + part
            s = lax.dot_general(kr[j * tk:(j + 1) * tk, :], q, nt, preferred_element_type=f32)
            s_scr[rows, :] = s
            cm = jnp.max(s, axis=0, keepdims=True)
            m_new = cm if m_new is None else jnp.maximum(m_new, cm)
        base += kr.shape[0]
    m_scr[...] = m_new
    o_ref[0] = (acc[0:V_DIM] / acc[V_DIM:V_DIM + 1]).astype(bf16)


def _attention(q, k, vt, kc, vct, *, nb, seq, tq):
    cached = kc is not None
    nq = seq // tq
    n_tiles = nb * HEADS * nq
    past = kc.shape[0] // nb if cached else 0
    tk = min(seq, TOEP)

    def cur(n):
        n = jnp.minimum(n, n_tiles - 1)
        return n // (HEADS * nq), (n // nq) % HEADS, n % nq

    def prev(n):
        n = jnp.maximum(n - 1, 0)
        return n // (HEADS * nq), (n // nq) % HEADS, n % nq

    def spec(shape, fn):
        return pl.BlockSpec(shape, fn)

    kern = functools.partial(_attn_kernel, cached=cached, tk=tk)
    q_spec = spec((tq, HEAD_PAD), lambda n: (cur(n)[0] * nq + cur(n)[2], cur(n)[1]))
    k_spec = lambda rows: spec((rows, HEAD_PAD), lambda n: (cur(n)[0], cur(n)[1]))
    v_spec = lambda cols: spec((1, V_DIM, cols), lambda n: (prev(n)[0], prev(n)[1], 0))
    if cached:
        ins = [q, k, kc, vt, vct]
        in_specs = [q_spec, k_spec(seq), k_spec(past), v_spec(seq), v_spec(past)]
    else:
        ins = [q, k, vt]
        in_specs = [q_spec, k_spec(seq), v_spec(seq)]
    return pl.pallas_call(
        kern,
        out_shape=jax.ShapeDtypeStruct((nb, HEADS * V_DIM, seq), bf16),
        grid=(n_tiles + 1,),
        in_specs=in_specs,
        out_specs=pl.BlockSpec((1, V_DIM, tq), lambda n: prev(n)),
        scratch_shapes=[pltpu.VMEM((seq + past, tq), f32), pltpu.VMEM((1, tq), f32)],
        compiler_params=_cparams("arbitrary"),
        name="mla_attention",
    )(*ins)


ROW_HALO = 16


def _mix_ffn_kernel(*refs, tm, tpb, chunks, kinds):
    x_ref, xp_ref, xn_ref, mod_ref, g_ref = refs[:5]
    wup_ref, cw_ref, cb_ref, wd_ref, o_ref = refs[5 + 4 * len(kinds):]
    i = pl.program_id(0)
    mod = mod_ref[0]
    tn = (((0,), (0,)), ((), ()))
    proj = [None, None, None]
    for n, kind in enumerate(kinds):
        main, prev, nxt, w_ref = refs[5 + 4 * n:9 + 4 * n]
        w = w_ref[...]
        if kind == "rows":
            parts = [jnp.dot(main[...], w, preferred_element_type=f32),
                     jnp.dot(prev[...], w, preferred_element_type=f32)[ROW_HALO - HALO:ROW_HALO],
                     jnp.dot(nxt[...], w, preferred_element_type=f32)[0:HALO]]
        else:
            ops = [main[0], prev[0][:, LANES - HALO:LANES], nxt[0][:, 0:HALO]]
            parts = [lax.dot_general(o.astype(bf16), w, tn, preferred_element_type=f32) for o in ops]
        proj = [p if q is None else q + p for q, p in zip(proj, parts)]
    g1 = mod[2:3]
    x1 = x_ref[...] + g1 * proj[0]
    h = _halo_ext(x1, xp_ref[...] + g1 * proj[1], xn_ref[...] + g1 * proj[2], g_ref[...], mod[4:5], mod[3:4],
                  (i % tpb) == 0, (i % tpb) == tpb - 1)
    dff = wd_ref.shape[0]
    acc = None
    for c0, c1 in chunks:
        halves = []
        for off in (c0, dff + c0):
            cols = slice(off, off + c1 - c0)
            y = jnp.dot(h, wup_ref[:, cols], preferred_element_type=f32)
            halves.append(_dwconv3(y, cw_ref[:, cols], cb_ref[:, cols], tm))
        gate, up = halves
        act = ((gate * jax.nn.sigmoid(gate)) * up).astype(bf16)
        part = jnp.dot(act, wd_ref[c0:c1, :], preferred_element_type=f32)
        acc = part if acc is None else acc + part
    o_ref[...] = x1 + mod[5:6] * acc


def _mix_ffn(x, mod, g, mixer_ops, w_up, conv_w, conv_b, w_down, *, seq, tm):
    t_rows, d = x.shape
    dff = w_down.shape[0]
    step = 6 * TOEP
    chunks = tuple((c, min(c + step, dff)) for c in range(0, dff, step))
    tpb = seq // tm
    mrows = seq if mod.shape[0] > 1 else t_rows
    kinds = tuple(k for k, _, _ in mixer_ops)
    kern = functools.partial(_mix_ffn_kernel, tm=tm, tpb=tpb, chunks=chunks, kinds=kinds)
    resident = lambda a: pl.BlockSpec(a.shape, lambda i: (0,) * a.ndim, pipeline_mode=pl.Buffered(1))
    ins, in_specs = [], []
    for kind, op, w in mixer_ops:
        if kind == "rows":
            c = op.shape[1]
            r, nblk = tm // ROW_HALO, t_rows // ROW_HALO
            in_specs += [pl.BlockSpec((tm, c), lambda i: (i, 0)),
                         pl.BlockSpec((ROW_HALO, c), lambda i, r=r: (jnp.maximum(i * r - 1, 0), 0)),
                         pl.BlockSpec((ROW_HALO, c), lambda i, r=r, nblk=nblk: (jnp.minimum((i + 1) * r, nblk - 1), 0))]
        else:
            c = op.shape[1]
            r, nblk = tm // LANES, seq // LANES
            in_specs += [pl.BlockSpec((1, c, tm), lambda i: (i // tpb, 0, i % tpb)),
                         pl.BlockSpec((1, c, LANES), lambda i, r=r: (i // tpb, 0, jnp.maximum((i % tpb) * r - 1, 0))),
                         pl.BlockSpec((1, c, LANES),
                                      lambda i, r=r, nblk=nblk: (i // tpb, 0, jnp.minimum((i % tpb + 1) * r, nblk - 1)))]
        in_specs.append(resident(w))
        ins += [op, op, op, w]
    return pl.pallas_call(
        kern,
        out_shape=jax.ShapeDtypeStruct((t_rows, d), f32),
        grid=(t_rows // tm,),
        in_specs=[*_row_specs(tm, d, t_rows),
                  pl.BlockSpec((1, 6, d), lambda i: ((i * tm) // mrows, 0, 0)),
                  resident(g), *in_specs,
                  resident(w_up), resident(conv_w), resident(conv_b), resident(w_down)],
        out_specs=pl.BlockSpec((tm, d), lambda i: (i, 0)),
        compiler_params=_cparams("arbitrary"),
        name="mix_ffn",
    )(x, x, x, mod, g, *ins, w_up, conv_w, conv_b, w_down)


def _rope_tables(seq):
    half = ROPE // 2
    rows = seq // GRID_W
    row = jnp.repeat(jnp.arange(rows), GRID_W)
    col = jnp.tile(jnp.arange(GRID_W), rows)
    inv = 1.0 / (ROPE_BASE ** (jnp.arange(0, half, 2, dtype=f32) / half))
    ar = row.astype(f32)[:, None] * inv[None]
    ac = col.astype(f32)[:, None] * inv[None]
    pad = jnp.zeros((seq, LANES - ROPE), f32)
    cos = jnp.concatenate([jnp.cos(ar), jnp.cos(ar), jnp.cos(ac), jnp.cos(ac), pad], axis=1)
    sin = jnp.concatenate([-jnp.sin(ar), jnp.sin(ar), -jnp.sin(ac), jnp.sin(ac), pad], axis=1)
    return cos, sin


def _pad_heads(a, width):
    lead = a.shape[:-1]
    a = a.reshape(*lead, HEADS, -1)
    a = jnp.pad(a, [(0, 0)] * len(lead) + [(0, 0), (0, width - a.shape[-1])])
    return a.reshape(*lead, HEADS * width)


def _mla_weights(j, mla_w_dq, mla_q_norm, mla_w_uq, mla_w_dkv, mla_kv_norm, mla_w_ukv, mla_q_head_norm,
                 mla_k_head_norm):
    ukv = mla_w_ukv[j].reshape(KV_RANK, HEADS, NOPE + V_DIM)
    kg = mla_k_head_norm[j]
    return {
        "w_dq": mla_w_dq[j].astype(bf16),
        "q_norm": mla_q_norm[j].reshape(1, -1),
        "w_uq": _pad_heads(mla_w_uq[j], HEAD_PAD).astype(bf16),
        "q_gain": jnp.pad(mla_q_head_norm[j], (0, HEAD_PAD - QK)).reshape(1, HEAD_PAD),
        "w_dkv": jnp.pad(mla_w_dkv[j], ((0, 0), (0, LANES - ROPE))).astype(bf16),
        "kv_norm": mla_kv_norm[j].reshape(1, -1),
        "w_uk": ukv[:, :, :NOPE].reshape(KV_RANK, HEADS * NOPE).astype(bf16),
        "w_uv": ukv[:, :, NOPE:].reshape(KV_RANK, HEADS * V_DIM).astype(bf16),
        "k_gain": jnp.pad(kg, (0, 2 * NOPE - QK)).reshape(1, 2 * NOPE),
    }


def kernel(x_prompt, x_sample, cache_ckv, cache_krope, c, c_ctx, ada_w, ada_b, norm_g, mix_w_in, sgu_w, sgu_b, hy_conv_w, hy_conv_b, hy_f_w1, hy_f_b1, hy_f_w2, hy_f_b2, hy_f_w3, hy_f_freq, hy_decay, hy_d, mix_w_out, mla_w_dq, mla_q_norm, mla_w_uq, mla_w_dkv, mla_kv_norm, mla_w_ukv, mla_q_head_norm, mla_k_head_norm, mla_w_o, ffn_w_up, ffn_conv_w, ffn_conv_b, ffn_w_down):
    depth = ada_w.shape[0]
    d = x_prompt.shape[-1]
    nbp, seqp, _ = x_prompt.shape
    nbs, seqs, _ = x_sample.shape
    a_width = A_GROUPS * sgu_w.shape[-1]

    cond = jnp.zeros((16, d), f32).at[0].set(c_ctx).at[1:1 + nbs].set(c)
    mods = _ada(cond, ada_w, ada_b).reshape(depth, 16, 6, d)

    w_in = mix_w_in.astype(bf16)
    sgu_wb = sgu_w.astype(bf16)
    sgu_bb = jnp.broadcast_to(sgu_b[..., None], sgu_b.shape + (sgu_w.shape[-1],))
    w_out = mix_w_out.astype(bf16)
    w_up = ffn_w_up.astype(bf16)
    w_down = ffn_w_down.astype(bf16)
    w_o = mla_w_o.astype(bf16)
    mla_w = [_mla_weights(j, mla_w_dq, mla_q_norm, mla_w_uq, mla_w_dkv, mla_kv_norm, mla_w_ukv,
                          mla_q_head_norm, mla_k_head_norm) for j in range(depth // 2)]

    def trunk(x3, mod_all, cache):
        nb, seq, _ = x3.shape
        x = x3.reshape(nb * seq, d)
        tm = min(seq, 512)
        latent = cache is not None
        tabs = _rope_tables(seq) if latent else None
        new_ckv, new_kr = [], []
        for l in range(depth):
            mod = mod_all[l]
            g1 = norm_g[l, 0].reshape(1, d)
            g2 = norm_g[l, 1].reshape(1, d)
            if l % 2 == 0:
                i = l // 2
                a, xbt = _even_front(x, mod, g1, w_in[i], sgu_wb[i], sgu_bb[i], hy_conv_w[i],
                                     hy_conv_b[i].reshape(1, -1), nb=nb, seq=seq, tm=tm)
                filt = _hyena_filters_t(seq, hy_f_w1[i], hy_f_b1[i], hy_f_w2[i], hy_f_b2[i], hy_f_w3[i],
                                        hy_f_freq[i], hy_decay[i])
                zt = _hyena(xbt, filt, hy_d[i].reshape(-1), nb=nb, seq=seq)
                mixer_ops = [("rows", a, w_out[i, :a_width]), ("cols", zt, w_out[i, a_width:])]
            else:
                j = l // 2
                outs = _mla_front(x, mod, g1, mla_w[j], tabs, nb=nb, seq=seq, tm=tm, emit_cache=not latent)
                q, k, v = outs[:3]
                if latent:
                    past = cache[0].shape[2]
                    ckv_c = cache[0][:, j].reshape(nb * past, KV_RANK)
                    kr_c = jnp.pad(cache[1][:, j].reshape(nb * past, ROPE), ((0, 0), (0, LANES - ROPE)))
                    kc, vc = _cache_expand(ckv_c, kr_c, mla_w[j], nb=nb, seq=past)
                else:
                    kc = vc = None
                    new_ckv.append(outs[3].reshape(nb, seq, KV_RANK))
                    new_kr.append(outs[4].reshape(nb, seq, ROPE))
                at = _attention(q, k, v, kc, vc, nb=nb, seq=seq, tq=min(seq, 1024))
                mixer_ops = [("cols", at, w_o[j])]
            x = _mix_ffn(x, mod, g2, mixer_ops, w_up[l], ffn_conv_w[l], ffn_conv_b[l].reshape(1, -1), w_down[l],
                         seq=seq, tm=tm)
        return x.reshape(nb, seq, d), new_ckv, new_kr

    y_prompt, ckv_list, kr_list = trunk(x_prompt, mods[:, 0:1], None)
    y_sample, _, _ = trunk(x_sample, mods[:, 1:1 + nbs], (cache_ckv, cache_krope))
    return (y_prompt, y_sample, jnp.stack(ckv_list, axis=1), jnp.stack(kr_list, axis=1))
```

```python
import functools
import math

import jax
import jax.numpy as jnp
from jax import lax
from jax.experimental import pallas as pl
from jax.experimental.pallas import tpu as pltpu

f32 = jnp.float32
bf16 = jnp.bfloat16

EPS = 1e-6
GRID_W = 64
CHUNK = 128
A_GROUPS = 4
HEADS = 8
NOPE = 128
ROPE = 64
QK = NOPE + ROPE
HEAD_PAD = 256
V_DIM = 128
KV_RANK = 256
ROPE_BASE = 10000.0
FILTER_BANDS = 16
LANES = 128
HALO = 8
TOEP = 256
VMEM_LIMIT = 56 * 1024 * 1024
HIGHEST = lax.Precision.HIGHEST


def _cparams(*sem):
    return pltpu.CompilerParams(dimension_semantics=sem, vmem_limit_bytes=VMEM_LIMIT)


def _norm_mod(x, g, sc, sh):
    y = x * lax.rsqrt(jnp.mean(x * x, axis=-1, keepdims=True) + EPS)
    return (y * g) * (1.0 + sc) + sh


def _halo_ext(x, xp, xn, g, sc, sh, first, last):
    hp = jnp.where(first, 0.0, _norm_mod(xp, g, sc, sh))
    hn = jnp.where(last, 0.0, _norm_mod(xn, g, sc, sh))
    h = _norm_mod(x, g, sc, sh)
    return jnp.concatenate([hp, h, hn], axis=0).astype(bf16)


def _dwconv3(y, w, b, tm):
    return (y[HALO - 1:HALO - 1 + tm] * w[0:1] + y[HALO:HALO + tm] * w[1:2]
            + y[HALO + 1:HALO + 1 + tm] * w[2:3] + b)


def _gelu_tanh(x):
    return 0.5 * x * (1.0 + jnp.tanh(math.sqrt(2.0 / math.pi) * (x + 0.044715 * (x * x * x))))


def _row_specs(tm, d, t_rows):
    r = tm // HALO
    nb = t_rows // HALO
    return (pl.BlockSpec((tm, d), lambda i, *_: (i, 0)),
            pl.BlockSpec((HALO, d), lambda i, *_: (jnp.maximum(i * r - 1, 0), 0)),
            pl.BlockSpec((HALO, d), lambda i, *_: (jnp.minimum((i + 1) * r, nb - 1), 0)))


def _ada_kernel(c_ref, w_ref, b_ref, o_ref):
    c = c_ref[...]
    s = c * jax.nn.sigmoid(c)
    o_ref[0] = jnp.dot(s, w_ref[0], precision=HIGHEST, preferred_element_type=f32) + b_ref[0]


def _ada(cond, ada_w, ada_b):
    depth, d, n = ada_w.shape
    tn = n // 4
    rows = cond.shape[0]
    return pl.pallas_call(
        _ada_kernel,
        out_shape=jax.ShapeDtypeStruct((depth, rows, n), f32),
        grid=(depth, n // tn),
        in_specs=[pl.BlockSpec((rows, d), lambda l, j: (0, 0)),
                  pl.BlockSpec((1, d, tn), lambda l, j: (l, 0, j)),
                  pl.BlockSpec((1, 1, tn), lambda l, j: (l, 0, j))],
        out_specs=pl.BlockSpec((1, rows, tn), lambda l, j: (l, 0, j)),
        compiler_params=_cparams("arbitrary", "arbitrary"),
        name="ada_mod",
    )(cond, ada_w, ada_b.reshape(depth, 1, n))


def _even_front_kernel(x_ref, xp_ref, xn_ref, mod_ref, g_ref, win_ref, sw_ref, sb_ref, cw_ref, cb_ref,
                       a_ref, xbt_ref, *, tm, tpb, a_width):
    i = pl.program_id(0)
    mod = mod_ref[0]
    h = _halo_ext(x_ref[...], xp_ref[...], xn_ref[...], g_ref[...], mod[1:2], mod[0:1],
                  (i % tpb) == 0, (i % tpb) == tpb - 1)
    p = jnp.dot(h, win_ref[...], preferred_element_type=f32)
    u = _gelu_tanh(p[HALO:HALO + tm, 0:a_width])
    v = _gelu_tanh(p[HALO:HALO + tm, a_width:2 * a_width]).astype(bf16)
    nch = tm // CHUNK
    a_ch = a_width // A_GROUPS
    for gi in range(A_GROUPS):
        cols = slice(gi * a_ch, (gi + 1) * a_ch)
        rhs = jnp.concatenate([v[n * CHUNK:(n + 1) * CHUNK, cols] for n in range(nch)], axis=1)
        s = jnp.dot(sw_ref[gi], rhs, preferred_element_type=f32)
        for n in range(nch):
            rows = slice(n * CHUNK, (n + 1) * CHUNK)
            sn = s[:, n * a_ch:(n + 1) * a_ch] + sb_ref[gi]
            a_ref[rows, cols] = (u[rows, cols] * sn).astype(bf16)
    xb = _dwconv3(p[:, 2 * a_width:], cw_ref[...], cb_ref[...], tm)
    xbt_ref[0] = xb.T


def _even_front(x, mod, g, w_in, sgu_w, sgu_b, conv_w, conv_b, *, nb, seq, tm):
    t_rows, d = x.shape
    tpb = seq // tm
    n_in = w_in.shape[1]
    a_width = A_GROUPS * sgu_w.shape[1]
    nxb = n_in - 2 * a_width
    mrows = seq if mod.shape[0] > 1 else t_rows
    kern = functools.partial(_even_front_kernel, tm=tm, tpb=tpb, a_width=a_width)
    full = lambda shape: pl.BlockSpec(shape, lambda i: (0,) * len(shape))
    return pl.pallas_call(
        kern,
        out_shape=(jax.ShapeDtypeStruct((t_rows, a_width), bf16),
                   jax.ShapeDtypeStruct((nb, nxb, seq), f32)),
        grid=(t_rows // tm,),
        in_specs=[*_row_specs(tm, d, t_rows),
                  pl.BlockSpec((1, 6, d), lambda i: ((i * tm) // mrows, 0, 0)),
                  full((1, d)), full(w_in.shape), full(sgu_w.shape), full(sgu_b.shape),
                  full(conv_w.shape), full(conv_b.shape)],
        out_specs=(pl.BlockSpec((tm, a_width), lambda i: (i, 0)),
                   pl.BlockSpec((1, nxb, tm), lambda i: (i // tpb, 0, i % tpb))),
        compiler_params=_cparams("arbitrary"),
        name="even_front",
    )(x, x, x, mod, g, w_in, sgu_w, sgu_b, conv_w, conv_b)


def _filter_kernel(w1t_ref, b1_ref, w2t_ref, b2_ref, fr_ref, w3t_ref, dec_ref, o_ref, hid_scr, *, seq):
    @pl.when(pl.program_id(0) == 0)
    def _():
        nfeat = w1t_ref.shape[1]
        t = lax.broadcasted_iota(jnp.int32, (nfeat, seq), 1).astype(f32)
        fi = lax.broadcasted_iota(jnp.int32, (nfeat, seq), 0)
        tn = t / seq
        band = jnp.where(fi <= FILTER_BANDS, fi, fi - FILTER_BANDS).astype(f32)
        ang = ((2.0 * math.pi) * tn) * band
        z = jnp.where(fi == 0, tn, jnp.where(fi <= FILTER_BANDS, jnp.sin(ang), jnp.cos(ang)))
        z = jnp.where(fi <= 2 * FILTER_BANDS, z, 0.0)
        fr = fr_ref[...]
        h1 = jnp.sin(fr * (jnp.dot(w1t_ref[...], z, precision=HIGHEST, preferred_element_type=f32) + b1_ref[...]))
        hid_scr[...] = jnp.sin(fr * (jnp.dot(w2t_ref[...], h1, precision=HIGHEST, preferred_element_type=f32)
                                     + b2_ref[...]))

    h = jnp.dot(w3t_ref[...], hid_scr[...], precision=HIGHEST, preferred_element_type=f32)
    tl = lax.broadcasted_iota(jnp.int32, (1, seq), 1).astype(f32)
    dist = jnp.abs(tl - (seq // 2)) / seq
    h = h * jnp.exp(-jnp.abs(dec_ref[...]) * dist)
    h = h / (jnp.sum(jnp.abs(h), axis=1, keepdims=True) + EPS)
    hb = h.astype(bf16).astype(f32)
    lane = lax.broadcasted_iota(jnp.int32, hb.shape, 1)
    rolled = pltpu.roll(hb, 1, 1)
    prev_bits = lax.bitcast_convert_type(rolled, jnp.int32)
    cur_bits = lax.shift_right_logical(lax.bitcast_convert_type(hb, jnp.int32), 16)
    o_ref[:, 0:seq] = cur_bits | jnp.where(lane == 0, 0, prev_bits)
    tail_lane = lax.broadcasted_iota(jnp.int32, (hb.shape[0], LANES), 1)
    wrapped = lax.bitcast_convert_type(rolled[:, 0:LANES], jnp.int32)
    o_ref[:, seq:seq + LANES] = jnp.where(tail_lane == 0, wrapped, 0)


def _hyena_filters_t(seq, w1, b1, w2, b2, w3, freq, decay):
    nfeat = 40
    hid = w1.shape[1]
    w1t = jnp.zeros((hid, nfeat), f32).at[:, :w1.shape[0]].set(w1.T)
    rows = w3.shape[1]
    rb = 256
    kern = functools.partial(_filter_kernel, seq=seq)
    full = lambda shape: pl.BlockSpec(shape, lambda i: (0,) * len(shape))
    return pl.pallas_call(
        kern,
        out_shape=jax.ShapeDtypeStruct((rows, seq + LANES), jnp.int32),
        grid=(rows // rb,),
        in_specs=[full((hid, nfeat)), full((hid, 1)), full((hid, hid)), full((hid, 1)), full((hid, 1)),
                  pl.BlockSpec((rb, hid), lambda i: (i, 0)), pl.BlockSpec((rb, 1), lambda i: (i, 0))],
        out_specs=pl.BlockSpec((rb, seq + LANES), lambda i: (i, 0)),
        scratch_shapes=[pltpu.VMEM((hid, seq), f32)],
        compiler_params=_cparams("arbitrary"),
        name="hyena_filter",
    )(w1t, b1.reshape(hid, 1), w2.T, b2.reshape(hid, 1), freq.reshape(hid, 1), w3.T, decay.reshape(rows, 1))


def _bank_steps(hrow, bank_ref, seq):
    nseg = seq // LANES
    hl = LANES // 2
    upper = lax.broadcasted_iota(jnp.int32, (hl, LANES), 1) >= 2 * lax.broadcasted_iota(jnp.int32, (hl, LANES), 0)
    zero = jnp.zeros((hl, LANES), jnp.int32)
    bank_ref[0] = zero
    bank_ref[nseg + 2] = zero
    prev = zero
    for k in range(nseg + 1):
        seg = jnp.broadcast_to(hrow[:, k * LANES:(k + 1) * LANES], (hl, LANES))
        cur = pltpu.roll(seg, 0, 1, stride=2, stride_axis=0)
        bank_ref[k + 1] = jnp.where(upper, cur, prev)
        prev = cur
        yield


def _conv_steps(u, bank_ref, upad_ref, out, *, nb, seq):
    nseg = seq // LANES
    nblk = seq // TOEP
    half = seq // 2
    upad_ref[:, 0:half] = jnp.zeros((nb, half), f32)
    upad_ref[:, half:half + seq] = u
    upad_ref[:, half + seq:2 * seq] = jnp.zeros((nb, half), f32)
    acc = None
    for kt in range(nblk + 1):
        k0 = nseg - 2 * kt
        lhs = jnp.concatenate([upad_ref[:, (i + kt) * TOEP:(i + kt + 1) * TOEP] for i in range(nblk)], axis=0)
        blk = lambda idx: pltpu.bitcast(bank_ref[idx], bf16)
        top = jnp.concatenate([blk(k0 + 1), blk(k0 + 2)], axis=1)
        bot = jnp.concatenate([blk(k0), blk(k0 + 1)], axis=1)
        w = jnp.concatenate([top, bot], axis=0)
        part = jnp.dot(lhs.astype(bf16), w, preferred_element_type=f32)
        acc = part if acc is None else acc + part
        yield
    out.append(jnp.concatenate([acc[i * nb:(i + 1) * nb] for i in range(nblk)], axis=1))


def _interleave(main, side, ratio):
    main, side = list(main), list(side)
    while main or side:
        main = [g for g in main if next(g, StopIteration) is not StopIteration]
        for _ in range(ratio if main else 1 << 30):
            side = [g for g in side if next(g, StopIteration) is not StopIteration]
            if not side:
                break


def _hyena_kernel(d_ref, vb_ref, x1_ref, x2_ref, h0_ref, h1_ref, z_ref,
                  u_scr, a_scr, b_scr, o_scr, bank00, bank01, bank10, bank11, upad0, upad1, *, nb, seq, cg, nchan):
    gidx = pl.program_id(0)
    bank_ref = ((bank00, bank01), (bank10, bank11))
    upad_ref = (upad0, upad1)
    u_scr[...] = jnp.swapaxes(vb_ref[...], 0, 1)
    a_scr[...] = jnp.swapaxes(x1_ref[...], 0, 1)
    b_scr[...] = jnp.swapaxes(x2_ref[...], 0, 1)

    def banks(c):
        gens = []
        if c < cg:
            gens.append(_bank_steps(h0_ref[c:c + 1, :], bank_ref[c % 2][0], seq))
        if 1 <= c <= cg:
            gens.append(_bank_steps(h1_ref[c - 1:c, :], bank_ref[c % 2][1], seq))
        return gens

    ratio = -(-(seq // LANES + 1) // (seq // TOEP + 1))
    _interleave([], banks(0), ratio)
    z_prev = None
    for c in range(cg + 1):
        convs, first, second = [], [], []
        if c < cg:
            u = u_scr[c]
            convs.append(_conv_steps(u, bank_ref[c % 2][0], upad_ref[0], first, nb=nb, seq=seq))
        if c >= 1:
            convs.append(_conv_steps(z_prev, bank_ref[c % 2][1], upad_ref[1], second, nb=nb, seq=seq))
        _interleave(convs, banks(c + 1), ratio)
        if c >= 1:
            o_scr[c - 1] = b_scr[c - 1] * (second[0] + z_prev * d_ref[nchan + gidx * cg + c - 1])
        if c < cg:
            z_prev = a_scr[c] * (first[0] + u * d_ref[gidx * cg + c])
    z_ref[...] = jnp.swapaxes(o_scr[...], 0, 1)


def _hyena(xbt, filt_t, d_flat, *, nb, seq):
    nchan = xbt.shape[1] // 3
    cg = 8
    ng = nchan // cg
    kern = functools.partial(_hyena_kernel, nb=nb, seq=seq, cg=cg, nchan=nchan)
    act = lambda off: pl.BlockSpec((nb, cg, seq), lambda i: (0, off + i, 0))
    wseq = seq + LANES
    return pl.pallas_call(
        kern,
        out_shape=jax.ShapeDtypeStruct((nb, nchan, seq), f32),
        grid=(ng,),
        in_specs=[pl.BlockSpec(memory_space=pltpu.SMEM),
                  act(0), act(ng), act(2 * ng),
                  pl.BlockSpec((cg, wseq), lambda i: (i, 0)),
                  pl.BlockSpec((cg, wseq), lambda i: (ng + i, 0))],
        out_specs=pl.BlockSpec((nb, cg, seq), lambda i: (0, i, 0)),
        scratch_shapes=[pltpu.VMEM((cg, nb, seq), f32), pltpu.VMEM((cg, nb, seq), f32),
                        pltpu.VMEM((cg, nb, seq), f32), pltpu.VMEM((cg, nb, seq), f32),
                        *[pltpu.VMEM((seq // LANES + 3, LANES // 2, LANES), jnp.int32) for _ in range(4)],
                        pltpu.VMEM((nb, 2 * seq), f32), pltpu.VMEM((nb, 2 * seq), f32)],
        compiler_params=_cparams("arbitrary"),
        name="hyena_conv",
    )(d_flat, xbt, xbt, xbt, filt_t, filt_t)


def _rope128(x, cos, sin):
    lane = lax.broadcasted_iota(jnp.int32, x.shape, 1)
    swapped = jnp.where((lane & 16) == 0, pltpu.roll(x, LANES - 16, 1), pltpu.roll(x, 16, 1))
    return x * cos + swapped * sin


def _expand_kv(ckv, kr_pad, wuk_ref, wuv_ref, kg_ref, cos, sin, k_ref, v_ref):
    cb = ckv.astype(bf16)
    kn = jnp.dot(cb, wuk_ref[...], preferred_element_type=f32)
    v = jnp.dot(cb, wuv_ref[...], preferred_element_type=f32)
    v_ref[0] = v.T.astype(bf16)
    kg = kg_ref[...]
    g_nope, g_rope = kg[:, 0:NOPE], kg[:, NOPE:2 * NOPE]
    kr_ss = jnp.sum(kr_pad * kr_pad, axis=-1, keepdims=True)
    krg = kr_pad * g_rope
    if cos is not None:
        krg = _rope128(krg, cos, sin)
    for hd in range(HEADS):
        kh = kn[:, hd * NOPE:(hd + 1) * NOPE]
        r = lax.rsqrt((jnp.sum(kh * kh, axis=-1, keepdims=True) + kr_ss) / QK + EPS)
        k_ref[:, hd * HEAD_PAD:hd * HEAD_PAD + NOPE] = (kh * r * g_nope).astype(bf16)
        k_ref[:, hd * HEAD_PAD + NOPE:(hd + 1) * HEAD_PAD] = (krg * r).astype(bf16)


def _mla_front_kernel(*refs, rope, emit_cache):
    (x_ref, mod_ref, g_ref, wdq_ref, qn_ref, wuq_ref, qg_ref, wdkv_ref, kvn_ref, wuk_ref, wuv_ref, kg_ref) = refs[:12]
    refs = refs[12:]
    if rope:
        cos_ref, sin_ref = refs[:2]
        refs = refs[2:]
        cos, sin = cos_ref[...], sin_ref[...]
    else:
        cos = sin = None
    q_ref, k_ref, v_ref = refs[:3]
    mod = mod_ref[0]
    h = _norm_mod(x_ref[...], g_ref[...], mod[1:2], mod[0:1]).astype(bf16)
    ql = jnp.dot(h, wdq_ref[...], preferred_element_type=f32)
    ql = ql * lax.rsqrt(jnp.mean(ql * ql, axis=-1, keepdims=True) + EPS) * qn_ref[...]
    q = jnp.dot(ql.astype(bf16), wuq_ref[...], preferred_element_type=f32)
    qg = qg_ref[...]
    c_exp = (1.0 / math.sqrt(QK)) * math.log2(math.e)
    for hd in range(HEADS):
        qh = q[:, hd * HEAD_PAD:(hd + 1) * HEAD_PAD]
        r = lax.rsqrt(jnp.sum(qh * qh, axis=-1, keepdims=True) / QK + EPS) * c_exp
        qh = qh * r * qg
        if rope:
            q_ref[:, hd * HEAD_PAD:hd * HEAD_PAD + NOPE] = qh[:, 0:NOPE].astype(bf16)
            q_ref[:, hd * HEAD_PAD + NOPE:(hd + 1) * HEAD_PAD] = _rope128(qh[:, NOPE:], cos, sin).astype(bf16)
        else:
            q_ref[:, hd * HEAD_PAD:(hd + 1) * HEAD_PAD] = qh.astype(bf16)
    dkv = jnp.dot(h, wdkv_ref[...], preferred_element_type=f32)
    c_raw = dkv[:, 0:KV_RANK]
    ckv = c_raw * lax.rsqrt(jnp.mean(c_raw * c_raw, axis=-1, keepdims=True) + EPS) * kvn_ref[...]
    kr_pad = dkv[:, KV_RANK:]
    if emit_cache:
        ckv_ref, kr_ref = refs[3:5]
        ckv_ref[...] = ckv
        kr_ref[...] = kr_pad[:, 0:ROPE]
    _expand_kv(ckv, kr_pad, wuk_ref, wuv_ref, kg_ref, cos, sin, k_ref, v_ref)


def _mla_front(x, mod, g, w, tabs, *, nb, seq, tm, emit_cache):
    t_rows, d = x.shape
    tpb = seq // tm
    rope = tabs is not None
    mrows = seq if mod.shape[0] > 1 else t_rows
    kern = functools.partial(_mla_front_kernel, rope=rope, emit_cache=emit_cache)
    full = lambda a: pl.BlockSpec(a.shape, lambda i: (0,) * a.ndim)
    wnames = ("w_dq", "q_norm", "w_uq", "q_gain", "w_dkv", "kv_norm", "w_uk", "w_uv", "k_gain")
    ins = [x, mod, g] + [w[n] for n in wnames]
    in_specs = [pl.BlockSpec((tm, d), lambda i: (i, 0)),
                pl.BlockSpec((1, 6, d), lambda i: ((i * tm) // mrows, 0, 0)),
                full(g)] + [full(w[n]) for n in wnames]
    if rope:
        ins += list(tabs)
        in_specs += [pl.BlockSpec((tm, LANES), lambda i: (i % tpb, 0))] * 2
    hp = HEADS * HEAD_PAD
    out_shape = [jax.ShapeDtypeStruct((t_rows, hp), bf16), jax.ShapeDtypeStruct((t_rows, hp), bf16),
                 jax.ShapeDtypeStruct((nb, HEADS * V_DIM, seq), bf16)]
    out_specs = [pl.BlockSpec((tm, hp), lambda i: (i, 0)), pl.BlockSpec((tm, hp), lambda i: (i, 0)),
                 pl.BlockSpec((1, HEADS * V_DIM, tm), lambda i: (i // tpb, 0, i % tpb))]
    if emit_cache:
        out_shape += [jax.ShapeDtypeStruct((t_rows, KV_RANK), f32), jax.ShapeDtypeStruct((t_rows, ROPE), f32)]
        out_specs += [pl.BlockSpec((tm, KV_RANK), lambda i: (i, 0)), pl.BlockSpec((tm, ROPE), lambda i: (i, 0))]
    return pl.pallas_call(
        kern, out_shape=tuple(out_shape), grid=(t_rows // tm,), in_specs=in_specs, out_specs=tuple(out_specs),
        compiler_params=_cparams("arbitrary"), name="mla_front",
    )(*ins)


def _cache_expand_kernel(ckv_ref, kr_ref, wuk_ref, wuv_ref, kg_ref, k_ref, v_ref):
    _expand_kv(ckv_ref[...], kr_ref[...], wuk_ref, wuv_ref, kg_ref, None, None, k_ref, v_ref)


def _cache_expand(ckv, kr_pad, w, *, nb, seq):
    t_rows = ckv.shape[0]
    tm = seq
    full = lambda a: pl.BlockSpec(a.shape, lambda i: (0,) * a.ndim)
    hp = HEADS * HEAD_PAD
    return pl.pallas_call(
        _cache_expand_kernel,
        out_shape=(jax.ShapeDtypeStruct((t_rows, hp), bf16), jax.ShapeDtypeStruct((nb, HEADS * V_DIM, seq), bf16)),
        grid=(t_rows // tm,),
        in_specs=[pl.BlockSpec((tm, KV_RANK), lambda i: (i, 0)), pl.BlockSpec((tm, LANES), lambda i: (i, 0)),
                  full(w["w_uk"]), full(w["w_uv"]), full(w["k_gain"])],
        out_specs=(pl.BlockSpec((tm, hp), lambda i: (i, 0)),
                   pl.BlockSpec((1, HEADS * V_DIM, tm), lambda i: (i, 0, 0))),
        compiler_params=_cparams("arbitrary"), name="mla_cache_expand",
    )(ckv, kr_pad, w["w_uk"], w["w_uv"], w["k_gain"])


def _attn_kernel(*refs, cached, tk):
    if cached:
        q_ref, k_ref, kc_ref, vt_ref, vct_ref, o_ref, s_scr, m_scr = refs
        parts = ((k_ref, vt_ref), (kc_ref, vct_ref))
    else:
        q_ref, k_ref, vt_ref, o_ref, s_scr, m_scr = refs
        parts = ((k_ref, vt_ref),)
    nt = (((1,), (1,)), ((), ()))

    @pl.when(pl.program_id(0) == 0)
    def _():
        s_scr[...] = jnp.zeros_like(s_scr)
        m_scr[...] = jnp.zeros_like(m_scr)

    q = q_ref[...]
    m_old = m_scr[...]
    m_new = acc = None
    base = 0
    for kr, vtr in parts:
        ck = min(tk, kr.shape[0])
        ones = jnp.ones((16, ck), bf16)
        for j in range(kr.shape[0] // ck):
            rows = slice(base + j * ck, base + (j + 1) * ck)
            p = jnp.exp2(s_scr[rows, :] - m_old).astype(bf16)
            lhs = jnp.concatenate([vtr[0, :, j * ck:(j + 1) * ck], ones], axis=0)
            part = jnp.dot(lhs, p, preferred_element_type=f32)
            acc = part if acc is None else acc + part
            s = lax.dot_general(kr[j * ck:(j + 1) * ck, :], q, nt, preferred_element_type=f32)
            s_scr[rows, :] = s
            cm = jnp.max(s, axis=0, keepdims=True)
            m_new = cm if m_new is None else jnp.maximum(m_new, cm)
        base += kr.shape[0]
    m_scr[...] = m_new
    o_ref[0] = (acc[0:V_DIM] / acc[V_DIM:V_DIM + 1]).astype(bf16)


def _attention(q, k, vt, kc, vct, *, nb, seq, tq, tk=TOEP):
    cached = kc is not None
    nq = seq // tq
    n_tiles = nb * HEADS * nq
    past = kc.shape[0] // nb if cached else 0
    tk = min(seq, tk)

    def cur(n):
        n = jnp.minimum(n, n_tiles - 1)
        return n // (HEADS * nq), (n // nq) % HEADS, n % nq

    def prev(n):
        n = jnp.maximum(n - 1, 0)
        return n // (HEADS * nq), (n // nq) % HEADS, n % nq

    def spec(shape, fn):
        return pl.BlockSpec(shape, fn)

    kern = functools.partial(_attn_kernel, cached=cached, tk=tk)
    q_spec = spec((tq, HEAD_PAD), lambda n: (cur(n)[0] * nq + cur(n)[2], cur(n)[1]))
    k_spec = lambda rows: spec((rows, HEAD_PAD), lambda n: (cur(n)[0], cur(n)[1]))
    v_spec = lambda cols: spec((1, V_DIM, cols), lambda n: (prev(n)[0], prev(n)[1], 0))
    if cached:
        ins = [q, k, kc, vt, vct]
        in_specs = [q_spec, k_spec(seq), k_spec(past), v_spec(seq), v_spec(past)]
    else:
        ins = [q, k, vt]
        in_specs = [q_spec, k_spec(seq), v_spec(seq)]
    return pl.pallas_call(
        kern,
        out_shape=jax.ShapeDtypeStruct((nb, HEADS * V_DIM, seq), bf16),
        grid=(n_tiles + 1,),
        in_specs=in_specs,
        out_specs=pl.BlockSpec((1, V_DIM, tq), lambda n: prev(n)),
        scratch_shapes=[pltpu.VMEM((seq + past, tq), f32), pltpu.VMEM((1, tq), f32)],
        compiler_params=_cparams("arbitrary"),
        name="mla_attention",
    )(*ins)


def _attn_short_kernel(q_ref, k_ref, vt_ref, o_ref):
    nt = (((1,), (1,)), ((), ()))
    seq = q_ref.shape[0]
    ones = jnp.ones((16, seq), bf16)
    for hd in range(HEADS):
        cols = slice(hd * HEAD_PAD, (hd + 1) * HEAD_PAD)
        s = lax.dot_general(k_ref[:, cols], q_ref[:, cols], nt, preferred_element_type=f32)
        p = jnp.exp2(s - jnp.max(s, axis=0, keepdims=True)).astype(bf16)
        rows = slice(hd * V_DIM, (hd + 1) * V_DIM)
        acc = jnp.dot(jnp.concatenate([vt_ref[0, rows, :], ones], axis=0), p, preferred_element_type=f32)
        o_ref[0, rows, :] = (acc[0:V_DIM] / acc[V_DIM:V_DIM + 1]).astype(bf16)


def _attention_short(q, k, vt, *, nb, seq):
    return pl.pallas_call(
        _attn_short_kernel,
        out_shape=jax.ShapeDtypeStruct((nb, HEADS * V_DIM, seq), bf16),
        grid=(nb,),
        in_specs=[pl.BlockSpec((seq, HEADS * HEAD_PAD), lambda b: (b, 0)),
                  pl.BlockSpec((seq, HEADS * HEAD_PAD), lambda b: (b, 0)),
                  pl.BlockSpec((1, HEADS * V_DIM, seq), lambda b: (b, 0, 0))],
        out_specs=pl.BlockSpec((1, HEADS * V_DIM, seq), lambda b: (b, 0, 0)),
        compiler_params=_cparams("arbitrary"),
        name="mla_attention_short",
    )(q, k, vt)


ROW_HALO = 16


def _mix_ffn_kernel(*refs, tm, tpb, chunks, kinds):
    x_ref, xp_ref, xn_ref, mod_ref, g_ref = refs[:5]
    wup_ref, cw_ref, cb_ref, wd_ref, o_ref = refs[5 + 4 * len(kinds):]
    i = pl.program_id(0)
    mod = mod_ref[0]
    tn = (((0,), (0,)), ((), ()))
    proj = [None, None, None]
    for n, kind in enumerate(kinds):
        main, prev, nxt, w_ref = refs[5 + 4 * n:9 + 4 * n]
        w = w_ref[...]
        if kind == "rows":
            parts = [jnp.dot(main[...], w, preferred_element_type=f32),
                     jnp.dot(prev[...], w, preferred_element_type=f32)[ROW_HALO - HALO:ROW_HALO],
                     jnp.dot(nxt[...], w, preferred_element_type=f32)[0:HALO]]
        else:
            ops = [main[0], prev[0][:, LANES - HALO:LANES], nxt[0][:, 0:HALO]]
            parts = [lax.dot_general(o.astype(bf16), w, tn, preferred_element_type=f32) for o in ops]
        proj = [p if q is None else q + p for q, p in zip(proj, parts)]
    g1 = mod[2:3]
    x1 = x_ref[...] + g1 * proj[0]
    h = _halo_ext(x1, xp_ref[...] + g1 * proj[1], xn_ref[...] + g1 * proj[2], g_ref[...], mod[4:5], mod[3:4],
                  (i % tpb) == 0, (i % tpb) == tpb - 1)
    dff = wd_ref.shape[0]
    acc = None
    for c0, c1 in chunks:
        halves = []
        for off in (c0, dff + c0):
            cols = slice(off, off + c1 - c0)
            y = jnp.dot(h, wup_ref[:, cols], preferred_element_type=f32)
            halves.append(_dwconv3(y, cw_ref[:, cols], cb_ref[:, cols], tm))
        gate, up = halves
        act = ((gate * jax.nn.sigmoid(gate)) * up).astype(bf16)
        part = jnp.dot(act, wd_ref[c0:c1, :], preferred_element_type=f32)
        acc = part if acc is None else acc + part
    o_ref[...] = x1 + mod[5:6] * acc


def _mix_ffn(x, mod, g, mixer_ops, w_up, conv_w, conv_b, w_down, *, seq, tm, chunk_tiles=6):
    t_rows, d = x.shape
    dff = w_down.shape[0]
    step = chunk_tiles * TOEP
    chunks = tuple((c, min(c + step, dff)) for c in range(0, dff, step))
    tpb = seq // tm
    mrows = seq if mod.shape[0] > 1 else t_rows
    kinds = tuple(k for k, _, _ in mixer_ops)
    kern = functools.partial(_mix_ffn_kernel, tm=tm, tpb=tpb, chunks=chunks, kinds=kinds)
    resident = lambda a: pl.BlockSpec(a.shape, lambda i: (0,) * a.ndim, pipeline_mode=pl.Buffered(1))
    ins, in_specs = [], []
    for kind, op, w in mixer_ops:
        if kind == "rows":
            c = op.shape[1]
            r, nblk = tm // ROW_HALO, t_rows // ROW_HALO
            in_specs += [pl.BlockSpec((tm, c), lambda i: (i, 0)),
                         pl.BlockSpec((ROW_HALO, c), lambda i, r=r: (jnp.maximum(i * r - 1, 0), 0)),
                         pl.BlockSpec((ROW_HALO, c), lambda i, r=r, nblk=nblk: (jnp.minimum((i + 1) * r, nblk - 1), 0))]
        else:
            c = op.shape[1]
            r, nblk = tm // LANES, seq // LANES
            in_specs += [pl.BlockSpec((1, c, tm), lambda i: (i // tpb, 0, i % tpb)),
                         pl.BlockSpec((1, c, LANES), lambda i, r=r: (i // tpb, 0, jnp.maximum((i % tpb) * r - 1, 0))),
                         pl.BlockSpec((1, c, LANES),
                                      lambda i, r=r, nblk=nblk: (i // tpb, 0, jnp.minimum((i % tpb + 1) * r, nblk - 1)))]
        in_specs.append(resident(w))
        ins += [op, op, op, w]
    return pl.pallas_call(
        kern,
        out_shape=jax.ShapeDtypeStruct((t_rows, d), f32),
        grid=(t_rows // tm,),
        in_specs=[*_row_specs(tm, d, t_rows),
                  pl.BlockSpec((1, 6, d), lambda i: ((i * tm) // mrows, 0, 0)),
                  resident(g), *in_specs,
                  resident(w_up), resident(conv_w), resident(conv_b), resident(w_down)],
        out_specs=pl.BlockSpec((tm, d), lambda i: (i, 0)),
        compiler_params=_cparams("arbitrary"),
        name="mix_ffn",
    )(x, x, x, mod, g, *ins, w_up, conv_w, conv_b, w_down)


def _rope_tables(seq):
    half = ROPE // 2
    rows = seq // GRID_W
    row = jnp.repeat(jnp.arange(rows), GRID_W)
    col = jnp.tile(jnp.arange(GRID_W), rows)
    inv = 1.0 / (ROPE_BASE ** (jnp.arange(0, half, 2, dtype=f32) / half))
    ar = row.astype(f32)[:, None] * inv[None]
    ac = col.astype(f32)[:, None] * inv[None]
    pad = jnp.zeros((seq, LANES - ROPE), f32)
    cos = jnp.concatenate([jnp.cos(ar), jnp.cos(ar), jnp.cos(ac), jnp.cos(ac), pad], axis=1)
    sin = jnp.concatenate([-jnp.sin(ar), jnp.sin(ar), -jnp.sin(ac), jnp.sin(ac), pad], axis=1)
    return cos, sin


def _pad_heads(a, width):
    lead = a.shape[:-1]
    a = a.reshape(*lead, HEADS, -1)
    a = jnp.pad(a, [(0, 0)] * len(lead) + [(0, 0), (0, width - a.shape[-1])])
    return a.reshape(*lead, HEADS * width)


def _mla_weights(j, mla_w_dq, mla_q_norm, mla_w_uq, mla_w_dkv, mla_kv_norm, mla_w_ukv, mla_q_head_norm,
                 mla_k_head_norm):
    ukv = mla_w_ukv[j].reshape(KV_RANK, HEADS, NOPE + V_DIM)
    kg = mla_k_head_norm[j]
    return {
        "w_dq": mla_w_dq[j].astype(bf16),
        "q_norm": mla_q_norm[j].reshape(1, -1),
        "w_uq": _pad_heads(mla_w_uq[j], HEAD_PAD).astype(bf16),
        "q_gain": jnp.pad(mla_q_head_norm[j], (0, HEAD_PAD - QK)).reshape(1, HEAD_PAD),
        "w_dkv": jnp.pad(mla_w_dkv[j], ((0, 0), (0, LANES - ROPE))).astype(bf16),
        "kv_norm": mla_kv_norm[j].reshape(1, -1),
        "w_uk": ukv[:, :, :NOPE].reshape(KV_RANK, HEADS * NOPE).astype(bf16),
        "w_uv": ukv[:, :, NOPE:].reshape(KV_RANK, HEADS * V_DIM).astype(bf16),
        "k_gain": jnp.pad(kg, (0, 2 * NOPE - QK)).reshape(1, 2 * NOPE),
    }


def kernel(x_prompt, x_sample, cache_ckv, cache_krope, c, c_ctx, ada_w, ada_b, norm_g, mix_w_in, sgu_w, sgu_b, hy_conv_w, hy_conv_b, hy_f_w1, hy_f_b1, hy_f_w2, hy_f_b2, hy_f_w3, hy_f_freq, hy_decay, hy_d, mix_w_out, mla_w_dq, mla_q_norm, mla_w_uq, mla_w_dkv, mla_kv_norm, mla_w_ukv, mla_q_head_norm, mla_k_head_norm, mla_w_o, ffn_w_up, ffn_conv_w, ffn_conv_b, ffn_w_down):
    depth = ada_w.shape[0]
    d = x_prompt.shape[-1]
    nbp, seqp, _ = x_prompt.shape
    nbs, seqs, _ = x_sample.shape
    a_width = A_GROUPS * sgu_w.shape[-1]

    cond = jnp.zeros((16, d), f32).at[0].set(c_ctx).at[1:1 + nbs].set(c)
    mods = _ada(cond, ada_w, ada_b).reshape(depth, 16, 6, d)

    w_in = mix_w_in.astype(bf16)
    sgu_wb = sgu_w.astype(bf16)
    sgu_bb = jnp.broadcast_to(sgu_b[..., None], sgu_b.shape + (sgu_w.shape[-1],))
    w_out = mix_w_out.astype(bf16)
    w_up = ffn_w_up.astype(bf16)
    w_down = ffn_w_down.astype(bf16)
    w_o = mla_w_o.astype(bf16)
    mla_w = [_mla_weights(j, mla_w_dq, mla_q_norm, mla_w_uq, mla_w_dkv, mla_kv_norm, mla_w_ukv,
                          mla_q_head_norm, mla_k_head_norm) for j in range(depth // 2)]

    def trunk(x3, mod_all, cache):
        nb, seq, _ = x3.shape
        x = x3.reshape(nb * seq, d)
        tm = min(seq, 512)
        latent = cache is not None
        tabs = _rope_tables(seq) if latent else None
        new_ckv, new_kr = [], []
        for l in range(depth):
            mod = mod_all[l]
            g1 = norm_g[l, 0].reshape(1, d)
            g2 = norm_g[l, 1].reshape(1, d)
            if l % 2 == 0:
                i = l // 2
                a, xbt = _even_front(x, mod, g1, w_in[i], sgu_wb[i], sgu_bb[i], hy_conv_w[i],
                                     hy_conv_b[i].reshape(1, -1), nb=nb, seq=seq, tm=tm)
                filt = _hyena_filters_t(seq, hy_f_w1[i], hy_f_b1[i], hy_f_w2[i], hy_f_b2[i], hy_f_w3[i],
                                        hy_f_freq[i], hy_decay[i])
                zt = _hyena(xbt, filt, hy_d[i].reshape(-1), nb=nb, seq=seq)
                mixer_ops = [("rows", a, w_out[i, :a_width]), ("cols", zt, w_out[i, a_width:])]
            else:
                j = l // 2
                outs = _mla_front(x, mod, g1, mla_w[j], tabs, nb=nb, seq=seq, tm=tm, emit_cache=not latent)
                q, k, v = outs[:3]
                if latent:
                    past = cache[0].shape[2]
                    ckv_c = cache[0][:, j].reshape(nb * past, KV_RANK)
                    kr_c = jnp.pad(cache[1][:, j].reshape(nb * past, ROPE), ((0, 0), (0, LANES - ROPE)))
                    kc, vc = _cache_expand(ckv_c, kr_c, mla_w[j], nb=nb, seq=past)
                else:
                    kc = vc = None
                    new_ckv.append(outs[3].reshape(nb, seq, KV_RANK))
                    new_kr.append(outs[4].reshape(nb, seq, ROPE))
                if kc is None and seq <= TOEP:
                    at = _attention_short(q, k, v, nb=nb, seq=seq)
                else:
                    at = _attention(q, k, v, kc, vc, nb=nb, seq=seq, tq=min(seq, 1024))
                mixer_ops = [("cols", at, w_o[j])]
            x = _mix_ffn(x, mod, g2, mixer_ops, w_up[l], ffn_conv_w[l], ffn_conv_b[l].reshape(1, -1), w_down[l],
                         seq=seq, tm=tm)
        return x.reshape(nb, seq, d), new_ckv, new_kr

    y_prompt, ckv_list, kr_list = trunk(x_prompt, mods[:, 0:1], None)
    y_sample, _, _ = trunk(x_sample, mods[:, 1:1 + nbs], (cache_ckv, cache_krope))
    return (y_prompt, y_sample, jnp.stack(ckv_list, axis=1), jnp.stack(kr_list, axis=1))
```

```python
import functools
import math

import jax
import jax.numpy as jnp
import numpy as np
from jax import lax
from jax.experimental import pallas as pl
from jax.experimental.pallas import tpu as pltpu

f32 = jnp.float32
bf16 = jnp.bfloat16

EPS = 1e-6
GRID_W = 64
CHUNK = 128
A_GROUPS = 4
HEADS = 8
NOPE = 128
ROPE = 64
QK = NOPE + ROPE
HEAD_PAD = 256
V_DIM = 128
KV_RANK = 256
ROPE_BASE = 10000.0
FILTER_BANDS = 16
LANES = 128
HALO = 8
TOEP = 256
VMEM_LIMIT = 56 * 1024 * 1024
HIGHEST = lax.Precision.HIGHEST


def _cparams(*sem):
    return pltpu.CompilerParams(dimension_semantics=sem, vmem_limit_bytes=VMEM_LIMIT)


def _norm_mod(x, g, sc, sh):
    y = x * lax.rsqrt(jnp.mean(x * x, axis=-1, keepdims=True) + EPS)
    return (y * g) * (1.0 + sc) + sh


def _halo_ext(x, xp, xn, g, sc, sh, first, last):
    hp = jnp.where(first, 0.0, _norm_mod(xp, g, sc, sh))
    hn = jnp.where(last, 0.0, _norm_mod(xn, g, sc, sh))
    h = _norm_mod(x, g, sc, sh)
    return jnp.concatenate([hp, h, hn], axis=0).astype(bf16)


def _dwconv3(y, w, b, tm):
    return (y[HALO - 1:HALO - 1 + tm] * w[0:1] + y[HALO:HALO + tm] * w[1:2]
            + y[HALO + 1:HALO + 1 + tm] * w[2:3] + b)


def _gelu_tanh(x):
    return 0.5 * x * (1.0 + jnp.tanh(math.sqrt(2.0 / math.pi) * (x + 0.044715 * (x * x * x))))


def _row_specs(tm, d, t_rows):
    r = tm // HALO
    nb = t_rows // HALO
    return (pl.BlockSpec((tm, d), lambda i, *_: (i, 0)),
            pl.BlockSpec((HALO, d), lambda i, *_: (jnp.maximum(i * r - 1, 0), 0)),
            pl.BlockSpec((HALO, d), lambda i, *_: (jnp.minimum((i + 1) * r, nb - 1), 0)))


def _ada_kernel(c_ref, w_ref, b_ref, o_ref):
    c = c_ref[...]
    s = c * jax.nn.sigmoid(c)
    o_ref[0] = jnp.dot(s, w_ref[0], precision=HIGHEST, preferred_element_type=f32) + b_ref[0]


def _ada(cond, ada_w, ada_b):
    depth, d, n = ada_w.shape
    tn = n // 4
    rows = cond.shape[0]
    return pl.pallas_call(
        _ada_kernel,
        out_shape=jax.ShapeDtypeStruct((depth, rows, n), f32),
        grid=(depth, n // tn),
        in_specs=[pl.BlockSpec((rows, d), lambda l, j: (0, 0)),
                  pl.BlockSpec((1, d, tn), lambda l, j: (l, 0, j)),
                  pl.BlockSpec((1, 1, tn), lambda l, j: (l, 0, j))],
        out_specs=pl.BlockSpec((1, rows, tn), lambda l, j: (l, 0, j)),
        compiler_params=_cparams("arbitrary", "arbitrary"),
        name="ada_mod",
    )(cond, ada_w, ada_b.reshape(depth, 1, n))


def _even_front_kernel(x_ref, xp_ref, xn_ref, mod_ref, g_ref, win_ref, sw_ref, sb_ref, cw_ref, cb_ref,
                       a_ref, xbt_ref, *, tm, tpb, a_width):
    i = pl.program_id(0)
    mod = mod_ref[0]
    h = _halo_ext(x_ref[...], xp_ref[...], xn_ref[...], g_ref[...], mod[1:2], mod[0:1],
                  (i % tpb) == 0, (i % tpb) == tpb - 1)
    p = jnp.dot(h, win_ref[...], preferred_element_type=f32)
    u = _gelu_tanh(p[HALO:HALO + tm, 0:a_width])
    v = _gelu_tanh(p[HALO:HALO + tm, a_width:2 * a_width]).astype(bf16)
    nch = tm // CHUNK
    a_ch = a_width // A_GROUPS
    for gi in range(A_GROUPS):
        cols = slice(gi * a_ch, (gi + 1) * a_ch)
        rhs = jnp.concatenate([v[n * CHUNK:(n + 1) * CHUNK, cols] for n in range(nch)], axis=1)
        s = jnp.dot(sw_ref[gi], rhs, preferred_element_type=f32)
        for n in range(nch):
            rows = slice(n * CHUNK, (n + 1) * CHUNK)
            sn = s[:, n * a_ch:(n + 1) * a_ch] + sb_ref[gi]
            a_ref[rows, cols] = (u[rows, cols] * sn).astype(bf16)
    xb = _dwconv3(p[:, 2 * a_width:], cw_ref[...], cb_ref[...], tm)
    xbt_ref[0] = xb.T


def _even_front(x, mod, g, w_in, sgu_w, sgu_b, conv_w, conv_b, *, nb, seq, tm):
    t_rows, d = x.shape
    tpb = seq // tm
    n_in = w_in.shape[1]
    a_width = A_GROUPS * sgu_w.shape[1]
    nxb = n_in - 2 * a_width
    mrows = seq if mod.shape[0] > 1 else t_rows
    kern = functools.partial(_even_front_kernel, tm=tm, tpb=tpb, a_width=a_width)
    full = lambda shape: pl.BlockSpec(shape, lambda i: (0,) * len(shape))
    return pl.pallas_call(
        kern,
        out_shape=(jax.ShapeDtypeStruct((t_rows, a_width), bf16),
                   jax.ShapeDtypeStruct((nb, nxb, seq), f32)),
        grid=(t_rows // tm,),
        in_specs=[*_row_specs(tm, d, t_rows),
                  pl.BlockSpec((1, 6, d), lambda i: ((i * tm) // mrows, 0, 0)),
                  full((1, d)), full(w_in.shape), full(sgu_w.shape), full(sgu_b.shape),
                  full(conv_w.shape), full(conv_b.shape)],
        out_specs=(pl.BlockSpec((tm, a_width), lambda i: (i, 0)),
                   pl.BlockSpec((1, nxb, tm), lambda i: (i // tpb, 0, i % tpb))),
        compiler_params=_cparams("arbitrary"),
        name="even_front",
    )(x, x, x, mod, g, w_in, sgu_w, sgu_b, conv_w, conv_b)


def _filter_kernel(w1t_ref, b1_ref, w2t_ref, b2_ref, fr_ref, w3t_ref, dec_ref, o_ref, hid_scr, *, seq):
    @pl.when(pl.program_id(0) == 0)
    def _():
        nfeat = w1t_ref.shape[1]
        t = lax.broadcasted_iota(jnp.int32, (nfeat, seq), 1).astype(f32)
        fi = lax.broadcasted_iota(jnp.int32, (nfeat, seq), 0)
        tn = t / seq
        band = jnp.where(fi <= FILTER_BANDS, fi, fi - FILTER_BANDS).astype(f32)
        ang = ((2.0 * math.pi) * tn) * band
        z = jnp.where(fi == 0, tn, jnp.where(fi <= FILTER_BANDS, jnp.sin(ang), jnp.cos(ang)))
        z = jnp.where(fi <= 2 * FILTER_BANDS, z, 0.0)
        fr = fr_ref[...]
        h1 = jnp.sin(fr * (jnp.dot(w1t_ref[...], z, precision=HIGHEST, preferred_element_type=f32) + b1_ref[...]))
        hid_scr[...] = jnp.sin(fr * (jnp.dot(w2t_ref[...], h1, precision=HIGHEST, preferred_element_type=f32)
                                     + b2_ref[...]))

    h = jnp.dot(w3t_ref[...], hid_scr[...], precision=HIGHEST, preferred_element_type=f32)
    tl = lax.broadcasted_iota(jnp.int32, (1, seq), 1).astype(f32)
    dist = jnp.abs(tl - (seq // 2)) / seq
    h = h * jnp.exp(-jnp.abs(dec_ref[...]) * dist)
    h = h / (jnp.sum(jnp.abs(h), axis=1, keepdims=True) + EPS)
    hb = h.astype(bf16).astype(f32)
    lane = lax.broadcasted_iota(jnp.int32, hb.shape, 1)
    rolled = pltpu.roll(hb, 1, 1)
    prev_bits = lax.bitcast_convert_type(rolled, jnp.int32)
    cur_bits = lax.shift_right_logical(lax.bitcast_convert_type(hb, jnp.int32), 16)
    o_ref[:, 0:seq] = cur_bits | jnp.where(lane == 0, 0, prev_bits)
    tail_lane = lax.broadcasted_iota(jnp.int32, (hb.shape[0], LANES), 1)
    wrapped = lax.bitcast_convert_type(rolled[:, 0:LANES], jnp.int32)
    o_ref[:, seq:seq + LANES] = jnp.where(tail_lane == 0, wrapped, 0)


def _hyena_filters_t(seq, w1, b1, w2, b2, w3, freq, decay):
    nfeat = 40
    hid = w1.shape[1]
    w1t = jnp.zeros((hid, nfeat), f32).at[:, :w1.shape[0]].set(w1.T)
    rows = w3.shape[1]
    rb = 256
    kern = functools.partial(_filter_kernel, seq=seq)
    full = lambda shape: pl.BlockSpec(shape, lambda i: (0,) * len(shape))
    return pl.pallas_call(
        kern,
        out_shape=jax.ShapeDtypeStruct((rows, seq + LANES), jnp.int32),
        grid=(rows // rb,),
        in_specs=[full((hid, nfeat)), full((hid, 1)), full((hid, hid)), full((hid, 1)), full((hid, 1)),
                  pl.BlockSpec((rb, hid), lambda i: (i, 0)), pl.BlockSpec((rb, 1), lambda i: (i, 0))],
        out_specs=pl.BlockSpec((rb, seq + LANES), lambda i: (i, 0)),
        scratch_shapes=[pltpu.VMEM((hid, seq), f32)],
        compiler_params=_cparams("arbitrary"),
        name="hyena_filter",
    )(w1t, b1.reshape(hid, 1), w2.T, b2.reshape(hid, 1), freq.reshape(hid, 1), w3.T, decay.reshape(rows, 1))


def _bank_steps(hrow, bank_ref, seq):
    nseg = seq // LANES
    hl = LANES // 2
    upper = lax.broadcasted_iota(jnp.int32, (hl, LANES), 1) >= 2 * lax.broadcasted_iota(jnp.int32, (hl, LANES), 0)
    zero = jnp.zeros((hl, LANES), jnp.int32)
    bank_ref[0] = zero
    bank_ref[nseg + 2] = zero
    prev = zero
    for k in range(nseg + 1):
        seg = jnp.broadcast_to(hrow[:, k * LANES:(k + 1) * LANES], (hl, LANES))
        cur = pltpu.roll(seg, 0, 1, stride=2, stride_axis=0)
        bank_ref[k + 1] = jnp.where(upper, cur, prev)
        prev = cur
        yield


def _conv_steps(u, bank_ref, upad_ref, out, *, nb, seq):
    nseg = seq // LANES
    nblk = seq // TOEP
    half = seq // 2
    upad_ref[:, 0:half] = jnp.zeros((nb, half), f32)
    upad_ref[:, half:half + seq] = u
    upad_ref[:, half + seq:2 * seq] = jnp.zeros((nb, half), f32)
    acc = None
    for kt in range(nblk + 1):
        k0 = nseg - 2 * kt
        lhs = jnp.concatenate([upad_ref[:, (i + kt) * TOEP:(i + kt + 1) * TOEP] for i in range(nblk)], axis=0)
        blk = lambda idx: pltpu.bitcast(bank_ref[idx], bf16)
        top = jnp.concatenate([blk(k0 + 1), blk(k0 + 2)], axis=1)
        bot = jnp.concatenate([blk(k0), blk(k0 + 1)], axis=1)
        w = jnp.concatenate([top, bot], axis=0)
        part = jnp.dot(lhs.astype(bf16), w, preferred_element_type=f32)
        acc = part if acc is None else acc + part
        yield
    out.append(jnp.concatenate([acc[i * nb:(i + 1) * nb] for i in range(nblk)], axis=1))


def _interleave(main, side, ratio):
    main, side = list(main), list(side)
    while main or side:
        main = [g for g in main if next(g, StopIteration) is not StopIteration]
        for _ in range(ratio if main else 1 << 30):
            side = [g for g in side if next(g, StopIteration) is not StopIteration]
            if not side:
                break


def _hyena_kernel(d_ref, vb_ref, x1_ref, x2_ref, h0_ref, h1_ref, z_ref,
                  u_scr, a_scr, b_scr, o_scr, bank00, bank01, bank10, bank11, upad0, upad1, *, nb, seq, cg, nchan):
    gidx = pl.program_id(0)
    bank_ref = ((bank00, bank01), (bank10, bank11))
    upad_ref = (upad0, upad1)
    u_scr[...] = jnp.swapaxes(vb_ref[...], 0, 1)
    a_scr[...] = jnp.swapaxes(x1_ref[...], 0, 1)
    b_scr[...] = jnp.swapaxes(x2_ref[...], 0, 1)

    def banks(c):
        gens = []
        if c < cg:
            gens.append(_bank_steps(h0_ref[c:c + 1, :], bank_ref[c % 2][0], seq))
        if 1 <= c <= cg:
            gens.append(_bank_steps(h1_ref[c - 1:c, :], bank_ref[c % 2][1], seq))
        return gens

    ratio = -(-(seq // LANES + 1) // (seq // TOEP + 1))
    _interleave([], banks(0), ratio)
    z_prev = None
    for c in range(cg + 1):
        convs, first, second = [], [], []
        if c < cg:
            u = u_scr[c]
            convs.append(_conv_steps(u, bank_ref[c % 2][0], upad_ref[0], first, nb=nb, seq=seq))
        if c >= 1:
            convs.append(_conv_steps(z_prev, bank_ref[c % 2][1], upad_ref[1], second, nb=nb, seq=seq))
        _interleave(convs, banks(c + 1), ratio)
        if c >= 1:
            o_scr[c - 1] = b_scr[c - 1] * (second[0] + z_prev * d_ref[nchan + gidx * cg + c - 1])
        if c < cg:
            z_prev = a_scr[c] * (first[0] + u * d_ref[gidx * cg + c])
    z_ref[...] = jnp.swapaxes(o_scr[...], 0, 1)


def _hyena(xbt, filt_t, d_flat, *, nb, seq):
    nchan = xbt.shape[1] // 3
    cg = 8
    ng = nchan // cg
    kern = functools.partial(_hyena_kernel, nb=nb, seq=seq, cg=cg, nchan=nchan)
    act = lambda off: pl.BlockSpec((nb, cg, seq), lambda i: (0, off + i, 0))
    wseq = seq + LANES
    return pl.pallas_call(
        kern,
        out_shape=jax.ShapeDtypeStruct((nb, nchan, seq), f32),
        grid=(ng,),
        in_specs=[pl.BlockSpec(memory_space=pltpu.SMEM),
                  act(0), act(ng), act(2 * ng),
                  pl.BlockSpec((cg, wseq), lambda i: (i, 0)),
                  pl.BlockSpec((cg, wseq), lambda i: (ng + i, 0))],
        out_specs=pl.BlockSpec((nb, cg, seq), lambda i: (0, i, 0)),
        scratch_shapes=[pltpu.VMEM((cg, nb, seq), f32), pltpu.VMEM((cg, nb, seq), f32),
                        pltpu.VMEM((cg, nb, seq), f32), pltpu.VMEM((cg, nb, seq), f32),
                        *[pltpu.VMEM((seq // LANES + 3, LANES // 2, LANES), jnp.int32) for _ in range(4)],
                        pltpu.VMEM((nb, 2 * seq), f32), pltpu.VMEM((nb, 2 * seq), f32)],
        compiler_params=_cparams("arbitrary"),
        name="hyena_conv",
    )(d_flat, xbt, xbt, xbt, filt_t, filt_t)


def _rope128(x, cos, sin):
    lane = lax.broadcasted_iota(jnp.int32, x.shape, 1)
    swapped = jnp.where((lane & 16) == 0, pltpu.roll(x, LANES - 16, 1), pltpu.roll(x, 16, 1))
    return x * cos + swapped * sin


def _expand_kv(ckv, kr_pad, wuk_ref, wuv_ref, kg_ref, cos, sin, k_ref, v_ref):
    cb = ckv.astype(bf16)
    kn = jnp.dot(cb, wuk_ref[...], preferred_element_type=f32)
    v = jnp.dot(cb, wuv_ref[...], preferred_element_type=f32)
    v_ref[0] = v.T.astype(bf16)
    kg = kg_ref[...]
    g_nope, g_rope = kg[:, 0:NOPE], kg[:, NOPE:2 * NOPE]
    kr_ss = jnp.sum(kr_pad * kr_pad, axis=-1, keepdims=True)
    krg = kr_pad * g_rope
    if cos is not None:
        krg = _rope128(krg, cos, sin)
    for hd in range(HEADS):
        kh = kn[:, hd * NOPE:(hd + 1) * NOPE]
        r = lax.rsqrt((jnp.sum(kh * kh, axis=-1, keepdims=True) + kr_ss) / QK + EPS)
        k_ref[:, hd * HEAD_PAD:hd * HEAD_PAD + NOPE] = (kh * r * g_nope).astype(bf16)
        k_ref[:, hd * HEAD_PAD + NOPE:(hd + 1) * HEAD_PAD] = (krg * r).astype(bf16)


def _mla_front_kernel(*refs, rope, emit_cache):
    (x_ref, mod_ref, g_ref, wdq_ref, qn_ref, wuq_ref, qg_ref, wdkv_ref, kvn_ref, wuk_ref, wuv_ref, kg_ref) = refs[:12]
    refs = refs[12:]
    if rope:
        cos_ref, sin_ref, qtab_ref = refs[:3]
        refs = refs[3:]
        cos, sin, qtab = cos_ref[...], sin_ref[...], qtab_ref[...]
    else:
        cos = sin = None
    q_ref, k_ref, v_ref = refs[:3]
    mod = mod_ref[0]
    h = _norm_mod(x_ref[...], g_ref[...], mod[1:2], mod[0:1]).astype(bf16)
    ql = jnp.dot(h, wdq_ref[...], preferred_element_type=f32)
    ql = ql * lax.rsqrt(jnp.mean(ql * ql, axis=-1, keepdims=True) + EPS) * qn_ref[...]
    q = jnp.dot(ql.astype(bf16), wuq_ref[...], preferred_element_type=f32)
    qg = qg_ref[...]
    c_exp = (1.0 / math.sqrt(QK)) * math.log2(math.e)
    for hd in range(HEADS):
        qh = q[:, hd * HEAD_PAD:(hd + 1) * HEAD_PAD]
        sq = qh * qh
        ssq = jnp.sum(sq[:, 0:NOPE] + 0.5 * sq[:, NOPE:], axis=-1, keepdims=True)
        qh = qh * (lax.rsqrt(ssq / QK + EPS) * c_exp) * qg
        if rope:
            y = qh[:, NOPE:] * qtab
            q_ref[:, hd * HEAD_PAD:hd * HEAD_PAD + NOPE] = qh[:, 0:NOPE].astype(bf16)
            q_ref[:, hd * HEAD_PAD + NOPE:(hd + 1) * HEAD_PAD] = (y + pltpu.roll(y, ROPE, 1)).astype(bf16)
        else:
            q_ref[:, hd * HEAD_PAD:(hd + 1) * HEAD_PAD] = qh.astype(bf16)
    dkv = jnp.dot(h, wdkv_ref[...], preferred_element_type=f32)
    c_raw = dkv[:, 0:KV_RANK]
    ckv = c_raw * lax.rsqrt(jnp.mean(c_raw * c_raw, axis=-1, keepdims=True) + EPS) * kvn_ref[...]
    kr_pad = dkv[:, KV_RANK:]
    if emit_cache:
        ckv_ref, kr_ref = refs[3:5]
        ckv_ref[...] = ckv
        kr_ref[...] = kr_pad[:, 0:ROPE]
    _expand_kv(ckv, kr_pad, wuk_ref, wuv_ref, kg_ref, cos, sin, k_ref, v_ref)


def _mla_front(x, mod, g, w, tabs, *, nb, seq, tm, emit_cache):
    t_rows, d = x.shape
    tpb = seq // tm
    rope = tabs is not None
    mrows = seq if mod.shape[0] > 1 else t_rows
    kern = functools.partial(_mla_front_kernel, rope=rope, emit_cache=emit_cache)
    full = lambda a: pl.BlockSpec(a.shape, lambda i: (0,) * a.ndim)
    wnames = ("w_dq", "q_norm", "w_uq", "q_gain", "w_dkv", "kv_norm", "w_uk", "w_uv", "k_gain")
    ins = [x, mod, g] + [w[n] for n in wnames]
    in_specs = [pl.BlockSpec((tm, d), lambda i: (i, 0)),
                pl.BlockSpec((1, 6, d), lambda i: ((i * tm) // mrows, 0, 0)),
                full(g)] + [full(w[n]) for n in wnames]
    if rope:
        ins += list(tabs)
        in_specs += [pl.BlockSpec((tm, LANES), lambda i: (i % tpb, 0))] * len(tabs)
    hp = HEADS * HEAD_PAD
    out_shape = [jax.ShapeDtypeStruct((t_rows, hp), bf16), jax.ShapeDtypeStruct((t_rows, hp), bf16),
                 jax.ShapeDtypeStruct((nb, HEADS * V_DIM, seq), bf16)]
    out_specs = [pl.BlockSpec((tm, hp), lambda i: (i, 0)), pl.BlockSpec((tm, hp), lambda i: (i, 0)),
                 pl.BlockSpec((1, HEADS * V_DIM, tm), lambda i: (i // tpb, 0, i % tpb))]
    if emit_cache:
        out_shape += [jax.ShapeDtypeStruct((t_rows, KV_RANK), f32), jax.ShapeDtypeStruct((t_rows, ROPE), f32)]
        out_specs += [pl.BlockSpec((tm, KV_RANK), lambda i: (i, 0)), pl.BlockSpec((tm, ROPE), lambda i: (i, 0))]
    return pl.pallas_call(
        kern, out_shape=tuple(out_shape), grid=(t_rows // tm,), in_specs=in_specs, out_specs=tuple(out_specs),
        compiler_params=_cparams("arbitrary"), name="mla_front",
    )(*ins)


def _cache_expand_kernel(ckv_ref, kr_ref, wuk_ref, wuv_ref, kg_ref, k_ref, v_ref):
    _expand_kv(ckv_ref[...], kr_ref[...], wuk_ref, wuv_ref, kg_ref, None, None, k_ref, v_ref)


def _cache_expand(ckv, kr_pad, w, *, nb, seq):
    t_rows = ckv.shape[0]
    tm = seq
    full = lambda a: pl.BlockSpec(a.shape, lambda i: (0,) * a.ndim)
    hp = HEADS * HEAD_PAD
    return pl.pallas_call(
        _cache_expand_kernel,
        out_shape=(jax.ShapeDtypeStruct((t_rows, hp), bf16), jax.ShapeDtypeStruct((nb, HEADS * V_DIM, seq), bf16)),
        grid=(t_rows // tm,),
        in_specs=[pl.BlockSpec((tm, KV_RANK), lambda i: (i, 0)), pl.BlockSpec((tm, LANES), lambda i: (i, 0)),
                  full(w["w_uk"]), full(w["w_uv"]), full(w["k_gain"])],
        out_specs=(pl.BlockSpec((tm, hp), lambda i: (i, 0)),
                   pl.BlockSpec((1, HEADS * V_DIM, tm), lambda i: (i, 0, 0))),
        compiler_params=_cparams("arbitrary"), name="mla_cache_expand",
    )(ckv, kr_pad, w["w_uk"], w["w_uv"], w["k_gain"])


def _attn_kernel(*refs, cached, tk):
    if cached:
        q_ref, k_ref, kc_ref, vt_ref, vct_ref, o_ref, s_scr, m_scr = refs
        parts = ((k_ref, vt_ref), (kc_ref, vct_ref))
    else:
        q_ref, k_ref, vt_ref, o_ref, s_scr, m_scr = refs
        parts = ((k_ref, vt_ref),)
    nt = (((1,), (1,)), ((), ()))

    @pl.when(pl.program_id(0) == 0)
    def _():
        s_scr[...] = jnp.zeros_like(s_scr)
        m_scr[...] = jnp.zeros_like(m_scr)

    q = q_ref[...]
    m_old = m_scr[...]
    m_new = acc = None
    base = 0
    for kr, vtr in parts:
        ck = min(tk, kr.shape[0])
        ones = jnp.ones((16, ck), bf16)
        for j in range(kr.shape[0] // ck):
            rows = slice(base + j * ck, base + (j + 1) * ck)
            p = jnp.exp2(s_scr[rows, :] - m_old).astype(bf16)
            lhs = jnp.concatenate([vtr[0, :, j * ck:(j + 1) * ck], ones], axis=0)
            part = jnp.dot(lhs, p, preferred_element_type=f32)
            acc = part if acc is None else acc + part
            s = lax.dot_general(kr[j * ck:(j + 1) * ck, :], q, nt, preferred_element_type=f32)
            s_scr[rows, :] = s
            cm = jnp.max(s, axis=0, keepdims=True)
            m_new = cm if m_new is None else jnp.maximum(m_new, cm)
        base += kr.shape[0]
    m_scr[...] = m_new
    o_ref[0] = (acc[0:V_DIM] / acc[V_DIM:V_DIM + 1]).astype(bf16)


def _attention(q, k, vt, kc, vct, *, nb, seq, tq, tk=TOEP):
    cached = kc is not None
    nq = seq // tq
    n_tiles = nb * HEADS * nq
    past = kc.shape[0] // nb if cached else 0
    tk = min(seq, tk)

    def cur(n):
        n = jnp.minimum(n, n_tiles - 1)
        return n // (HEADS * nq), (n // nq) % HEADS, n % nq

    def prev(n):
        n = jnp.maximum(n - 1, 0)
        return n // (HEADS * nq), (n // nq) % HEADS, n % nq

    def spec(shape, fn):
        return pl.BlockSpec(shape, fn)

    kern = functools.partial(_attn_kernel, cached=cached, tk=tk)
    q_spec = spec((tq, HEAD_PAD), lambda n: (cur(n)[0] * nq + cur(n)[2], cur(n)[1]))
    k_spec = lambda rows: spec((rows, HEAD_PAD), lambda n: (cur(n)[0], cur(n)[1]))
    v_spec = lambda cols: spec((1, V_DIM, cols), lambda n: (prev(n)[0], prev(n)[1], 0))
    if cached:
        ins = [q, k, kc, vt, vct]
        in_specs = [q_spec, k_spec(seq), k_spec(past), v_spec(seq), v_spec(past)]
    else:
        ins = [q, k, vt]
        in_specs = [q_spec, k_spec(seq), v_spec(seq)]
    return pl.pallas_call(
        kern,
        out_shape=jax.ShapeDtypeStruct((nb, HEADS * V_DIM, seq), bf16),
        grid=(n_tiles + 1,),
        in_specs=in_specs,
        out_specs=pl.BlockSpec((1, V_DIM, tq), lambda n: prev(n)),
        scratch_shapes=[pltpu.VMEM((seq + past, tq), f32), pltpu.VMEM((1, tq), f32)],
        compiler_params=_cparams("arbitrary"),
        name="mla_attention",
    )(*ins)


def _attn_short_kernel(q_ref, k_ref, vt_ref, o_ref):
    nt = (((1,), (1,)), ((), ()))
    seq = q_ref.shape[0]
    ones = jnp.ones((16, seq), bf16)
    for hd in range(HEADS):
        cols = slice(hd * HEAD_PAD, (hd + 1) * HEAD_PAD)
        s = lax.dot_general(k_ref[:, cols], q_ref[:, cols], nt, preferred_element_type=f32)
        p = jnp.exp2(s - jnp.max(s, axis=0, keepdims=True)).astype(bf16)
        rows = slice(hd * V_DIM, (hd + 1) * V_DIM)
        acc = jnp.dot(jnp.concatenate([vt_ref[0, rows, :], ones], axis=0), p, preferred_element_type=f32)
        o_ref[0, rows, :] = (acc[0:V_DIM] / acc[V_DIM:V_DIM + 1]).astype(bf16)


def _attention_short(q, k, vt, *, nb, seq):
    return pl.pallas_call(
        _attn_short_kernel,
        out_shape=jax.ShapeDtypeStruct((nb, HEADS * V_DIM, seq), bf16),
        grid=(nb,),
        in_specs=[pl.BlockSpec((seq, HEADS * HEAD_PAD), lambda b: (b, 0)),
                  pl.BlockSpec((seq, HEADS * HEAD_PAD), lambda b: (b, 0)),
                  pl.BlockSpec((1, HEADS * V_DIM, seq), lambda b: (b, 0, 0))],
        out_specs=pl.BlockSpec((1, HEADS * V_DIM, seq), lambda b: (b, 0, 0)),
        compiler_params=_cparams("arbitrary"),
        name="mla_attention_short",
    )(q, k, vt)


ROW_HALO = 16


def _mix_ffn_kernel(*refs, tm, tpb, chunks, kinds):
    x_ref, xp_ref, xn_ref, mod_ref, g_ref = refs[:5]
    wup_ref, cw_ref, cb_ref, wd_ref, o_ref = refs[5 + 4 * len(kinds):]
    i = pl.program_id(0)
    mod = mod_ref[0]
    tn = (((0,), (0,)), ((), ()))
    proj = [None, None, None]
    for n, kind in enumerate(kinds):
        main, prev, nxt, w_ref = refs[5 + 4 * n:9 + 4 * n]
        w = w_ref[...]
        if kind == "rows":
            parts = [jnp.dot(main[...], w, preferred_element_type=f32),
                     jnp.dot(prev[...], w, preferred_element_type=f32)[ROW_HALO - HALO:ROW_HALO],
                     jnp.dot(nxt[...], w, preferred_element_type=f32)[0:HALO]]
        else:
            ops = [main[0], prev[0][:, LANES - HALO:LANES], nxt[0][:, 0:HALO]]
            parts = [lax.dot_general(o.astype(bf16), w, tn, preferred_element_type=f32) for o in ops]
        proj = [p if q is None else q + p for q, p in zip(proj, parts)]
    g1 = mod[2:3]
    x1 = x_ref[...] + g1 * proj[0]
    h = _halo_ext(x1, xp_ref[...] + g1 * proj[1], xn_ref[...] + g1 * proj[2], g_ref[...], mod[4:5], mod[3:4],
                  (i % tpb) == 0, (i % tpb) == tpb - 1)
    dff = wd_ref.shape[0]
    acc = None
    for c0, c1 in chunks:
        halves = []
        for off in (c0, dff + c0):
            cols = slice(off, off + c1 - c0)
            y = jnp.dot(h, wup_ref[:, cols], preferred_element_type=f32)
            halves.append(_dwconv3(y, cw_ref[:, cols], cb_ref[:, cols], tm))
        gate, up = halves
        act = ((gate * jax.nn.sigmoid(gate)) * up).astype(bf16)
        part = jnp.dot(act, wd_ref[c0:c1, :], preferred_element_type=f32)
        acc = part if acc is None else acc + part
    o_ref[...] = x1 + mod[5:6] * acc


def _mix_ffn(x, mod, g, mixer_ops, w_up, conv_w, conv_b, w_down, *, seq, tm, chunk_tiles=6):
    t_rows, d = x.shape
    dff = w_down.shape[0]
    step = chunk_tiles * TOEP
    chunks = tuple((c, min(c + step, dff)) for c in range(0, dff, step))
    tpb = seq // tm
    mrows = seq if mod.shape[0] > 1 else t_rows
    kinds = tuple(k for k, _, _ in mixer_ops)
    kern = functools.partial(_mix_ffn_kernel, tm=tm, tpb=tpb, chunks=chunks, kinds=kinds)
    resident = lambda a: pl.BlockSpec(a.shape, lambda i: (0,) * a.ndim, pipeline_mode=pl.Buffered(1))
    ins, in_specs = [], []
    for kind, op, w in mixer_ops:
        c = op.shape[1]
        if kind == "rows":
            r, nblk = tm // ROW_HALO, t_rows // ROW_HALO
            in_specs += [pl.BlockSpec((tm, c), lambda i: (i, 0)),
                         pl.BlockSpec((ROW_HALO, c), lambda i, r=r: (jnp.maximum(i * r - 1, 0), 0)),
                         pl.BlockSpec((ROW_HALO, c), lambda i, r=r, nblk=nblk: (jnp.minimum((i + 1) * r, nblk - 1), 0))]
        else:
            r, nblk = tm // LANES, seq // LANES
            in_specs += [pl.BlockSpec((1, c, tm), lambda i: (i // tpb, 0, i % tpb)),
                         pl.BlockSpec((1, c, LANES), lambda i, r=r: (i // tpb, 0, jnp.maximum((i % tpb) * r - 1, 0))),
                         pl.BlockSpec((1, c, LANES),
                                      lambda i, r=r, nblk=nblk: (i // tpb, 0, jnp.minimum((i % tpb + 1) * r, nblk - 1)))]
        in_specs.append(resident(w))
        ins += [op, op, op, w]
    return pl.pallas_call(
        kern,
        out_shape=jax.ShapeDtypeStruct((t_rows, d), f32),
        grid=(t_rows // tm,),
        in_specs=[*_row_specs(tm, d, t_rows),
                  pl.BlockSpec((1, 6, d), lambda i: ((i * tm) // mrows, 0, 0)),
                  resident(g), *in_specs,
                  resident(w_up), resident(conv_w), resident(conv_b), resident(w_down)],
        out_specs=pl.BlockSpec((tm, d), lambda i: (i, 0)),
        compiler_params=_cparams("arbitrary"),
        name="mix_ffn",
    )(x, x, x, mod, g, *ins, w_up, conv_w, conv_b, w_down)


def _rope_tables(seq):
    half = ROPE // 2
    rows = seq // GRID_W
    row = np.repeat(np.arange(rows), GRID_W).astype(np.float32)
    col = np.tile(np.arange(GRID_W), rows).astype(np.float32)
    inv = (1.0 / (ROPE_BASE ** (np.arange(0, half, 2, dtype=np.float32) / half))).astype(np.float32)
    ar = (row[:, None] * inv[None]).astype(np.float64)
    ac = (col[:, None] * inv[None]).astype(np.float64)
    pad = np.zeros((seq, LANES - ROPE))
    cos64 = np.concatenate([np.cos(ar), np.cos(ar), np.cos(ac), np.cos(ac)], axis=1)
    sin64 = np.concatenate([-np.sin(ar), np.sin(ar), -np.sin(ac), np.sin(ac)], axis=1)
    as_f32 = lambda a: jnp.asarray(a.astype(np.float32))
    return (as_f32(np.concatenate([cos64, pad], axis=1)), as_f32(np.concatenate([sin64, pad], axis=1)),
            as_f32(np.concatenate([cos64, sin64], axis=1)))


_ROPE_SWAP = np.array([l + 16 if (l & 16) == 0 else l - 16 for l in range(ROPE)])


def _pad_heads_swapped(a):
    lead = a.shape[:-1]
    a = a.reshape(*lead, HEADS, QK)
    a = jnp.concatenate([a, a[..., NOPE:][..., _ROPE_SWAP]], axis=-1)
    return a.reshape(*lead, HEADS * HEAD_PAD)


def _mla_weights(j, mla_w_dq, mla_q_norm, mla_w_uq, mla_w_dkv, mla_kv_norm, mla_w_ukv, mla_q_head_norm,
                 mla_k_head_norm):
    ukv = mla_w_ukv[j].reshape(KV_RANK, HEADS, NOPE + V_DIM)
    kg = mla_k_head_norm[j]
    return {
        "w_dq": mla_w_dq[j].astype(bf16),
        "q_norm": mla_q_norm[j].reshape(1, -1),
        "w_uq": _pad_heads_swapped(mla_w_uq[j]).astype(bf16),
        "q_gain": jnp.concatenate([mla_q_head_norm[j], mla_q_head_norm[j][NOPE:][_ROPE_SWAP]]).reshape(1, HEAD_PAD),
        "w_dkv": jnp.pad(mla_w_dkv[j], ((0, 0), (0, LANES - ROPE))).astype(bf16),
        "kv_norm": mla_kv_norm[j].reshape(1, -1),
        "w_uk": ukv[:, :, :NOPE].reshape(KV_RANK, HEADS * NOPE).astype(bf16),
        "w_uv": ukv[:, :, NOPE:].reshape(KV_RANK, HEADS * V_DIM).astype(bf16),
        "k_gain": jnp.pad(kg, (0, 2 * NOPE - QK)).reshape(1, 2 * NOPE),
    }


def kernel(x_prompt, x_sample, cache_ckv, cache_krope, c, c_ctx, ada_w, ada_b, norm_g, mix_w_in, sgu_w, sgu_b, hy_conv_w, hy_conv_b, hy_f_w1, hy_f_b1, hy_f_w2, hy_f_b2, hy_f_w3, hy_f_freq, hy_decay, hy_d, mix_w_out, mla_w_dq, mla_q_norm, mla_w_uq, mla_w_dkv, mla_kv_norm, mla_w_ukv, mla_q_head_norm, mla_k_head_norm, mla_w_o, ffn_w_up, ffn_conv_w, ffn_conv_b, ffn_w_down):
    depth = ada_w.shape[0]
    d = x_prompt.shape[-1]
    nbp, seqp, _ = x_prompt.shape
    nbs, seqs, _ = x_sample.shape
    a_width = A_GROUPS * sgu_w.shape[-1]

    cond = jnp.zeros((16, d), f32).at[0].set(c_ctx).at[1:1 + nbs].set(c)
    mods = _ada(cond, ada_w, ada_b).reshape(depth, 16, 6, d)

    n_even = mix_w_in.shape[0]
    w_in = [mix_w_in[i].astype(bf16) for i in range(n_even)]
    sgu_wb = sgu_w.astype(bf16)
    sgu_bb = jnp.broadcast_to(sgu_b[..., None], sgu_b.shape + (sgu_w.shape[-1],))
    w_out_a = [mix_w_out[i, :a_width].astype(bf16) for i in range(n_even)]
    w_out_z = [mix_w_out[i, a_width:].astype(bf16) for i in range(n_even)]
    w_up = [ffn_w_up[l].astype(bf16) for l in range(depth)]
    w_down = [ffn_w_down[l].astype(bf16) for l in range(depth)]
    w_o = [mla_w_o[j].astype(bf16) for j in range(depth // 2)]
    mla_w = [_mla_weights(j, mla_w_dq, mla_q_norm, mla_w_uq, mla_w_dkv, mla_kv_norm, mla_w_ukv,
                          mla_q_head_norm, mla_k_head_norm) for j in range(depth // 2)]

    def trunk(x3, mod_all, cache):
        nb, seq, _ = x3.shape
        x = x3.reshape(nb * seq, d)
        tm = min(seq, 512)
        latent = cache is not None
        tabs = _rope_tables(seq) if latent else None
        new_ckv, new_kr = [], []
        for l in range(depth):
            mod = mod_all[l]
            g1 = norm_g[l, 0].reshape(1, d)
            g2 = norm_g[l, 1].reshape(1, d)
            if l % 2 == 0:
                i = l // 2
                a, xbt = _even_front(x, mod, g1, w_in[i], sgu_wb[i], sgu_bb[i], hy_conv_w[i],
                                     hy_conv_b[i].reshape(1, -1), nb=nb, seq=seq, tm=tm)
                filt = _hyena_filters_t(seq, hy_f_w1[i], hy_f_b1[i], hy_f_w2[i], hy_f_b2[i], hy_f_w3[i],
                                        hy_f_freq[i], hy_decay[i])
                zt = _hyena(xbt, filt, hy_d[i].reshape(-1), nb=nb, seq=seq)
                mixer_ops = [("rows", a, w_out_a[i]), ("cols", zt, w_out_z[i])]
            else:
                j = l // 2
                outs = _mla_front(x, mod, g1, mla_w[j], tabs, nb=nb, seq=seq, tm=tm, emit_cache=not latent)
                q, k, v = outs[:3]
                if latent:
                    past = cache[0].shape[2]
                    ckv_c = cache[0][:, j].reshape(nb * past, KV_RANK)
                    kr_c = jnp.pad(cache[1][:, j].reshape(nb * past, ROPE), ((0, 0), (0, LANES - ROPE)))
                    kc, vc = _cache_expand(ckv_c, kr_c, mla_w[j], nb=nb, seq=past)
                else:
                    kc = vc = None
                    new_ckv.append(outs[3].reshape(nb, seq, KV_RANK))
                    new_kr.append(outs[4].reshape(nb, seq, ROPE))
                if kc is None and seq <= TOEP:
                    at = _attention_short(q, k, v, nb=nb, seq=seq)
                else:
                    at = _attention(q, k, v, kc, vc, nb=nb, seq=seq, tq=min(seq, 1024))
                mixer_ops = [("cols", at, w_o[j])]
            x = _mix_ffn(x, mod, g2, mixer_ops, w_up[l], ffn_conv_w[l], ffn_conv_b[l].reshape(1, -1), w_down[l],
                         seq=seq, tm=tm)
        return x.reshape(nb, seq, d), new_ckv, new_kr

    y_prompt, ckv_list, kr_list = trunk(x_prompt, mods[:, 0:1], None)
    y_sample, _, _ = trunk(x_sample, mods[:, 1:1 + nbs], (cache_ckv, cache_krope))
    return (y_prompt, y_sample, jnp.stack(ckv_list, axis=1), jnp.stack(kr_list, axis=1))
```

```python
import functools
import math

import jax
import jax.numpy as jnp
import numpy as np
from jax import lax
from jax.experimental import pallas as pl
from jax.experimental.pallas import tpu as pltpu

f32 = jnp.float32
bf16 = jnp.bfloat16

EPS = 1e-6
GRID_W = 64
CHUNK = 128
A_GROUPS = 4
HEADS = 8
NOPE = 128
ROPE = 64
QK = NOPE + ROPE
HEAD_PAD = 256
V_DIM = 128
KV_RANK = 256
ROPE_BASE = 10000.0
FILTER_BANDS = 16
LANES = 128
HALO = 8
TOEP = 256
VMEM_LIMIT = 56 * 1024 * 1024
HIGHEST = lax.Precision.HIGHEST


def _cparams(*sem):
    return pltpu.CompilerParams(dimension_semantics=sem, vmem_limit_bytes=VMEM_LIMIT)


def _norm_mod(x, g, sc, sh):
    y = x * lax.rsqrt(jnp.mean(x * x, axis=-1, keepdims=True) + EPS)
    return (y * g) * (1.0 + sc) + sh


def _halo_ext(x, xp, xn, g, sc, sh, first, last):
    hp = jnp.where(first, 0.0, _norm_mod(xp, g, sc, sh))
    hn = jnp.where(last, 0.0, _norm_mod(xn, g, sc, sh))
    h = _norm_mod(x, g, sc, sh)
    return jnp.concatenate([hp, h, hn], axis=0).astype(bf16)


def _dwconv3(y, w, b, tm):
    return (y[HALO - 1:HALO - 1 + tm] * w[0:1] + y[HALO:HALO + tm] * w[1:2]
            + y[HALO + 1:HALO + 1 + tm] * w[2:3] + b)


def _gelu_tanh(x):
    return 0.5 * x * (1.0 + jnp.tanh(math.sqrt(2.0 / math.pi) * (x + 0.044715 * (x * x * x))))


def _row_specs(tm, d, t_rows):
    r = tm // HALO
    nb = t_rows // HALO
    return (pl.BlockSpec((tm, d), lambda i, *_: (i, 0)),
            pl.BlockSpec((HALO, d), lambda i, *_: (jnp.maximum(i * r - 1, 0), 0)),
            pl.BlockSpec((HALO, d), lambda i, *_: (jnp.minimum((i + 1) * r, nb - 1), 0)))


def _ada_kernel(c_ref, w_ref, b_ref, o_ref):
    c = c_ref[...]
    s = c * jax.nn.sigmoid(c)
    o_ref[0] = jnp.dot(s, w_ref[0], precision=HIGHEST, preferred_element_type=f32) + b_ref[0]


def _ada(cond, ada_w, ada_b):
    depth, d, n = ada_w.shape
    tn = n // 4
    rows = cond.shape[0]
    return pl.pallas_call(
        _ada_kernel,
        out_shape=jax.ShapeDtypeStruct((depth, rows, n), f32),
        grid=(depth, n // tn),
        in_specs=[pl.BlockSpec((rows, d), lambda l, j: (0, 0)),
                  pl.BlockSpec((1, d, tn), lambda l, j: (l, 0, j)),
                  pl.BlockSpec((1, 1, tn), lambda l, j: (l, 0, j))],
        out_specs=pl.BlockSpec((1, rows, tn), lambda l, j: (l, 0, j)),
        compiler_params=_cparams("arbitrary", "arbitrary"),
        name="ada_mod",
    )(cond, ada_w, ada_b.reshape(depth, 1, n))


def _even_front_kernel(x_ref, xp_ref, xn_ref, mod_ref, g_ref, win_ref, sw_ref, sb_ref, cw_ref, cb_ref,
                       a_ref, xbt_ref, *, tm, tpb, a_width):
    i = pl.program_id(0)
    mod = mod_ref[0]
    h = _halo_ext(x_ref[...], xp_ref[...], xn_ref[...], g_ref[...], mod[1:2], mod[0:1],
                  (i % tpb) == 0, (i % tpb) == tpb - 1)
    p = jnp.dot(h, win_ref[...], preferred_element_type=f32)
    u = _gelu_tanh(p[HALO:HALO + tm, 0:a_width])
    v = _gelu_tanh(p[HALO:HALO + tm, a_width:2 * a_width]).astype(bf16)
    nch = tm // CHUNK
    a_ch = a_width // A_GROUPS
    for gi in range(A_GROUPS):
        cols = slice(gi * a_ch, (gi + 1) * a_ch)
        rhs = jnp.concatenate([v[n * CHUNK:(n + 1) * CHUNK, cols] for n in range(nch)], axis=1)
        s = jnp.dot(sw_ref[gi], rhs, preferred_element_type=f32)
        for n in range(nch):
            rows = slice(n * CHUNK, (n + 1) * CHUNK)
            sn = s[:, n * a_ch:(n + 1) * a_ch] + sb_ref[gi]
            a_ref[rows, cols] = (u[rows, cols] * sn).astype(bf16)
    xb = _dwconv3(p[:, 2 * a_width:], cw_ref[...], cb_ref[...], tm)
    xbt_ref[0] = xb.T


def _even_front(x, mod, g, w_in, sgu_w, sgu_b, conv_w, conv_b, *, nb, seq, tm):
    t_rows, d = x.shape
    tpb = seq // tm
    n_in = w_in.shape[1]
    a_width = A_GROUPS * sgu_w.shape[1]
    nxb = n_in - 2 * a_width
    mrows = seq if mod.shape[0] > 1 else t_rows
    kern = functools.partial(_even_front_kernel, tm=tm, tpb=tpb, a_width=a_width)
    full = lambda shape: pl.BlockSpec(shape, lambda i: (0,) * len(shape))
    return pl.pallas_call(
        kern,
        out_shape=(jax.ShapeDtypeStruct((t_rows, a_width), bf16),
                   jax.ShapeDtypeStruct((nb, nxb, seq), f32)),
        grid=(t_rows // tm,),
        in_specs=[*_row_specs(tm, d, t_rows),
                  pl.BlockSpec((1, 6, d), lambda i: ((i * tm) // mrows, 0, 0)),
                  full((1, d)), full(w_in.shape), full(sgu_w.shape), full(sgu_b.shape),
                  full(conv_w.shape), full(conv_b.shape)],
        out_specs=(pl.BlockSpec((tm, a_width), lambda i: (i, 0)),
                   pl.BlockSpec((1, nxb, tm), lambda i: (i // tpb, 0, i % tpb))),
        compiler_params=_cparams("arbitrary"),
        name="even_front",
    )(x, x, x, mod, g, w_in, sgu_w, sgu_b, conv_w, conv_b)


def _filter_kernel(w1t_ref, b1_ref, w2t_ref, b2_ref, fr_ref, w3t_ref, dec_ref, o_ref, hid_scr, *, seq):
    @pl.when(pl.program_id(0) == 0)
    def _():
        nfeat = w1t_ref.shape[1]
        t = lax.broadcasted_iota(jnp.int32, (nfeat, seq), 1).astype(f32)
        fi = lax.broadcasted_iota(jnp.int32, (nfeat, seq), 0)
        tn = t / seq
        band = jnp.where(fi <= FILTER_BANDS, fi, fi - FILTER_BANDS).astype(f32)
        ang = ((2.0 * math.pi) * tn) * band
        z = jnp.where(fi == 0, tn, jnp.where(fi <= FILTER_BANDS, jnp.sin(ang), jnp.cos(ang)))
        z = jnp.where(fi <= 2 * FILTER_BANDS, z, 0.0)
        fr = fr_ref[...]
        h1 = jnp.sin(fr * (jnp.dot(w1t_ref[...], z, precision=HIGHEST, preferred_element_type=f32) + b1_ref[...]))
        hid_scr[...] = jnp.sin(fr * (jnp.dot(w2t_ref[...], h1, precision=HIGHEST, preferred_element_type=f32)
                                     + b2_ref[...]))

    h = jnp.dot(w3t_ref[...], hid_scr[...], precision=HIGHEST, preferred_element_type=f32)
    tl = lax.broadcasted_iota(jnp.int32, (1, seq), 1).astype(f32)
    dist = jnp.abs(tl - (seq // 2)) / seq
    h = h * jnp.exp(-jnp.abs(dec_ref[...]) * dist)
    h = h / (jnp.sum(jnp.abs(h), axis=1, keepdims=True) + EPS)
    hb = h.astype(bf16).astype(f32)
    lane = lax.broadcasted_iota(jnp.int32, hb.shape, 1)
    rolled = pltpu.roll(hb, 1, 1)
    prev_bits = lax.bitcast_convert_type(rolled, jnp.int32)
    cur_bits = lax.shift_right_logical(lax.bitcast_convert_type(hb, jnp.int32), 16)
    o_ref[:, 0:seq] = cur_bits | jnp.where(lane == 0, 0, prev_bits)
    tail_lane = lax.broadcasted_iota(jnp.int32, (hb.shape[0], LANES), 1)
    wrapped = lax.bitcast_convert_type(rolled[:, 0:LANES], jnp.int32)
    o_ref[:, seq:seq + LANES] = jnp.where(tail_lane == 0, wrapped, 0)


def _hyena_filters_t(seq, w1, b1, w2, b2, w3, freq, decay):
    nfeat = 40
    hid = w1.shape[1]
    w1t = jnp.zeros((hid, nfeat), f32).at[:, :w1.shape[0]].set(w1.T)
    rows = w3.shape[1]
    rb = 256
    kern = functools.partial(_filter_kernel, seq=seq)
    full = lambda shape: pl.BlockSpec(shape, lambda i: (0,) * len(shape))
    return pl.pallas_call(
        kern,
        out_shape=jax.ShapeDtypeStruct((rows, seq + LANES), jnp.int32),
        grid=(rows // rb,),
        in_specs=[full((hid, nfeat)), full((hid, 1)), full((hid, hid)), full((hid, 1)), full((hid, 1)),
                  pl.BlockSpec((rb, hid), lambda i: (i, 0)), pl.BlockSpec((rb, 1), lambda i: (i, 0))],
        out_specs=pl.BlockSpec((rb, seq + LANES), lambda i: (i, 0)),
        scratch_shapes=[pltpu.VMEM((hid, seq), f32)],
        compiler_params=_cparams("arbitrary"),
        name="hyena_filter",
    )(w1t, b1.reshape(hid, 1), w2.T, b2.reshape(hid, 1), freq.reshape(hid, 1), w3.T, decay.reshape(rows, 1))


def _bank_steps(hrow, bank_ref, seq):
    nseg = seq // LANES
    hl = LANES // 2
    upper = lax.broadcasted_iota(jnp.int32, (hl, LANES), 1) >= 2 * lax.broadcasted_iota(jnp.int32, (hl, LANES), 0)
    zero = jnp.zeros((hl, LANES), jnp.int32)
    bank_ref[0] = zero
    bank_ref[nseg + 2] = zero
    prev = zero
    for k in range(nseg + 1):
        seg = jnp.broadcast_to(hrow[:, k * LANES:(k + 1) * LANES], (hl, LANES))
        cur = pltpu.roll(seg, 0, 1, stride=2, stride_axis=0)
        bank_ref[k + 1] = jnp.where(upper, cur, prev)
        prev = cur
        yield


def _conv_steps(u, bank_ref, upad_ref, out, *, nb, seq):
    nseg = seq // LANES
    nblk = seq // TOEP
    half = seq // 2
    upad_ref[:, 0:half] = jnp.zeros((nb, half), f32)
    upad_ref[:, half:half + seq] = u
    upad_ref[:, half + seq:2 * seq] = jnp.zeros((nb, half), f32)
    acc = None
    for kt in range(nblk + 1):
        k0 = nseg - 2 * kt
        lhs = jnp.concatenate([upad_ref[:, (i + kt) * TOEP:(i + kt + 1) * TOEP] for i in range(nblk)], axis=0)
        blk = lambda idx: pltpu.bitcast(bank_ref[idx], bf16)
        top = jnp.concatenate([blk(k0 + 1), blk(k0 + 2)], axis=1)
        bot = jnp.concatenate([blk(k0), blk(k0 + 1)], axis=1)
        w = jnp.concatenate([top, bot], axis=0)
        part = jnp.dot(lhs.astype(bf16), w, preferred_element_type=f32)
        acc = part if acc is None else acc + part
        yield
    out.append(jnp.concatenate([acc[i * nb:(i + 1) * nb] for i in range(nblk)], axis=1))


def _interleave(main, side, ratio):
    main, side = list(main), list(side)
    while main or side:
        main = [g for g in main if next(g, StopIteration) is not StopIteration]
        for _ in range(ratio if main else 1 << 30):
            side = [g for g in side if next(g, StopIteration) is not StopIteration]
            if not side:
                break


def _hyena_kernel(d_ref, vb_ref, x1_ref, x2_ref, h0_ref, h1_ref, z_ref,
                  u_scr, a_scr, b_scr, o_scr, bank00, bank01, bank10, bank11, upad0, upad1, *, nb, seq, cg, nchan):
    gidx = pl.program_id(0)
    bank_ref = ((bank00, bank01), (bank10, bank11))
    upad_ref = (upad0, upad1)
    u_scr[...] = jnp.swapaxes(vb_ref[...], 0, 1)
    a_scr[...] = jnp.swapaxes(x1_ref[...], 0, 1)
    b_scr[...] = jnp.swapaxes(x2_ref[...], 0, 1)

    def banks(c):
        gens = []
        if c < cg:
            gens.append(_bank_steps(h0_ref[c:c + 1, :], bank_ref[c % 2][0], seq))
        if 1 <= c <= cg:
            gens.append(_bank_steps(h1_ref[c - 1:c, :], bank_ref[c % 2][1], seq))
        return gens

    ratio = -(-(seq // LANES + 1) // (seq // TOEP + 1))
    _interleave([], banks(0), ratio)
    z_prev = None
    for c in range(cg + 1):
        convs, first, second = [], [], []
        if c < cg:
            u = u_scr[c]
            convs.append(_conv_steps(u, bank_ref[c % 2][0], upad_ref[0], first, nb=nb, seq=seq))
        if c >= 1:
            convs.append(_conv_steps(z_prev, bank_ref[c % 2][1], upad_ref[1], second, nb=nb, seq=seq))
        _interleave(convs, banks(c + 1), ratio)
        if c >= 1:
            o_scr[c - 1] = b_scr[c - 1] * (second[0] + z_prev * d_ref[nchan + gidx * cg + c - 1])
        if c < cg:
            z_prev = a_scr[c] * (first[0] + u * d_ref[gidx * cg + c])
    z_ref[...] = jnp.swapaxes(o_scr[...], 0, 1)


def _hyena(xbt, filt_t, d_flat, *, nb, seq, cg=8):
    nchan = xbt.shape[1] // 3
    ng = nchan // cg
    kern = functools.partial(_hyena_kernel, nb=nb, seq=seq, cg=cg, nchan=nchan)
    act = lambda off: pl.BlockSpec((nb, cg, seq), lambda i: (0, off + i, 0))
    wseq = seq + LANES
    return pl.pallas_call(
        kern,
        out_shape=jax.ShapeDtypeStruct((nb, nchan, seq), f32),
        grid=(ng,),
        in_specs=[pl.BlockSpec(memory_space=pltpu.SMEM),
                  act(0), act(ng), act(2 * ng),
                  pl.BlockSpec((cg, wseq), lambda i: (i, 0)),
                  pl.BlockSpec((cg, wseq), lambda i: (ng + i, 0))],
        out_specs=pl.BlockSpec((nb, cg, seq), lambda i: (0, i, 0)),
        scratch_shapes=[pltpu.VMEM((cg, nb, seq), f32), pltpu.VMEM((cg, nb, seq), f32),
                        pltpu.VMEM((cg, nb, seq), f32), pltpu.VMEM((cg, nb, seq), f32),
                        *[pltpu.VMEM((seq // LANES + 3, LANES // 2, LANES), jnp.int32) for _ in range(4)],
                        pltpu.VMEM((nb, 2 * seq), f32), pltpu.VMEM((nb, 2 * seq), f32)],
        compiler_params=_cparams("arbitrary"),
        name="hyena_conv",
    )(d_flat, xbt, xbt, xbt, filt_t, filt_t)


def _rope128(x, cos, sin):
    lane = lax.broadcasted_iota(jnp.int32, x.shape, 1)
    swapped = jnp.where((lane & 16) == 0, pltpu.roll(x, LANES - 16, 1), pltpu.roll(x, 16, 1))
    return x * cos + swapped * sin


def _expand_kv(ckv, kr_pad, wuk_ref, wuv_ref, kg_ref, cos, sin, k_ref, v_ref):
    cb = ckv.astype(bf16)
    kn = jnp.dot(cb, wuk_ref[...], preferred_element_type=f32)
    v_ref[0] = jnp.dot(wuv_ref[...], ckv.T.astype(bf16), preferred_element_type=f32).astype(bf16)
    kg = kg_ref[...]
    g_nope, g_rope = kg[:, 0:NOPE], kg[:, NOPE:2 * NOPE]
    kr_ss = jnp.sum(kr_pad * kr_pad, axis=-1, keepdims=True)
    krg = kr_pad * g_rope
    if cos is not None:
        krg = _rope128(krg, cos, sin)
    for hd in range(HEADS):
        kh = kn[:, hd * NOPE:(hd + 1) * NOPE]
        r = lax.rsqrt((jnp.sum(kh * kh, axis=-1, keepdims=True) + kr_ss) / QK + EPS)
        k_ref[:, hd * HEAD_PAD:hd * HEAD_PAD + NOPE] = (kh * r * g_nope).astype(bf16)
        k_ref[:, hd * HEAD_PAD + NOPE:(hd + 1) * HEAD_PAD] = (krg * r).astype(bf16)


def _mla_front_kernel(*refs, rope, emit_cache):
    (x_ref, mod_ref, g_ref, wdq_ref, qn_ref, wuq_ref, qg_ref, wdkv_ref, kvn_ref, wuk_ref, wuv_ref, kg_ref) = refs[:12]
    refs = refs[12:]
    if rope:
        cos_ref, sin_ref, qcos_ref, qsin_ref = refs[:4]
        refs = refs[4:]
        cos, sin, qcos, qsin = cos_ref[...], sin_ref[...], qcos_ref[...], qsin_ref[...]
    else:
        cos = sin = None
    q_ref, k_ref, v_ref = refs[:3]
    mod = mod_ref[0]
    h = _norm_mod(x_ref[...], g_ref[...], mod[1:2], mod[0:1]).astype(bf16)
    ql = jnp.dot(h, wdq_ref[...], preferred_element_type=f32)
    ql = ql * lax.rsqrt(jnp.mean(ql * ql, axis=-1, keepdims=True) + EPS) * qn_ref[...]
    qt = jnp.dot(wuq_ref[...], ql.T.astype(bf16), preferred_element_type=f32)
    qg = qg_ref[...]
    c_exp = (1.0 / math.sqrt(QK)) * math.log2(math.e)
    g16 = ROPE // 4
    for hd in range(HEADS):
        qh = qt[hd * HEAD_PAD:(hd + 1) * HEAD_PAD]
        ssq = jnp.sum(qh * qh, axis=0, keepdims=True)
        qh = qh * (lax.rsqrt(ssq / QK + EPS) * c_exp) * qg
        if rope:
            x = qh[NOPE:QK]
            swapped = jnp.concatenate([x[g16:2 * g16], x[0:g16], x[3 * g16:], x[2 * g16:3 * g16]], axis=0)
            qh = jnp.concatenate([qh[0:NOPE], x * qcos + swapped * qsin, qh[QK:]], axis=0)
        q_ref[0, hd * HEAD_PAD:(hd + 1) * HEAD_PAD, :] = qh.astype(bf16)
    dkv = jnp.dot(h, wdkv_ref[...], preferred_element_type=f32)
    c_raw = dkv[:, 0:KV_RANK]
    ckv = c_raw * lax.rsqrt(jnp.mean(c_raw * c_raw, axis=-1, keepdims=True) + EPS) * kvn_ref[...]
    kr_pad = dkv[:, KV_RANK:]
    if emit_cache:
        ckv_ref, kr_ref = refs[3:5]
        ckv_ref[...] = ckv
        kr_ref[...] = kr_pad[:, 0:ROPE]
    _expand_kv(ckv, kr_pad, wuk_ref, wuv_ref, kg_ref, cos, sin, k_ref, v_ref)


def _mla_front(x, mod, g, w, tabs, *, nb, seq, tm, emit_cache):
    t_rows, d = x.shape
    tpb = seq // tm
    rope = tabs is not None
    mrows = seq if mod.shape[0] > 1 else t_rows
    kern = functools.partial(_mla_front_kernel, rope=rope, emit_cache=emit_cache)
    full = lambda a: pl.BlockSpec(a.shape, lambda i: (0,) * a.ndim)
    wnames = ("w_dq", "q_norm", "w_uq", "q_gain", "w_dkv", "kv_norm", "w_uk", "w_uv", "k_gain")
    w = dict(w, q_gain=jnp.broadcast_to(w["q_gain"].reshape(HEAD_PAD, 1), (HEAD_PAD, tm)))
    ins = [x, mod, g] + [w[n] for n in wnames]
    in_specs = [pl.BlockSpec((tm, d), lambda i: (i, 0)),
                pl.BlockSpec((1, 6, d), lambda i: ((i * tm) // mrows, 0, 0)),
                full(g)] + [full(w[n]) for n in wnames]
    if rope:
        ins += list(tabs)
        in_specs += [pl.BlockSpec((tm, LANES), lambda i: (i % tpb, 0))] * 2
        in_specs += [pl.BlockSpec((ROPE, tm), lambda i: (0, i % tpb))] * 2
    hp = HEADS * HEAD_PAD
    out_shape = [jax.ShapeDtypeStruct((nb, hp, seq), bf16), jax.ShapeDtypeStruct((t_rows, hp), bf16),
                 jax.ShapeDtypeStruct((nb, HEADS * V_DIM, seq), bf16)]
    out_specs = [pl.BlockSpec((1, hp, tm), lambda i: (i // tpb, 0, i % tpb)), pl.BlockSpec((tm, hp), lambda i: (i, 0)),
                 pl.BlockSpec((1, HEADS * V_DIM, tm), lambda i: (i // tpb, 0, i % tpb))]
    if emit_cache:
        out_shape += [jax.ShapeDtypeStruct((t_rows, KV_RANK), f32), jax.ShapeDtypeStruct((t_rows, ROPE), f32)]
        out_specs += [pl.BlockSpec((tm, KV_RANK), lambda i: (i, 0)), pl.BlockSpec((tm, ROPE), lambda i: (i, 0))]
    return pl.pallas_call(
        kern, out_shape=tuple(out_shape), grid=(t_rows // tm,), in_specs=in_specs, out_specs=tuple(out_specs),
        compiler_params=_cparams("arbitrary"), name="mla_front",
    )(*ins)


def _cache_expand_kernel(ckv_ref, kr_ref, wuk_ref, wuv_ref, kg_ref, k_ref, v_ref):
    _expand_kv(ckv_ref[...], kr_ref[...], wuk_ref, wuv_ref, kg_ref, None, None, k_ref, v_ref)


def _cache_expand(ckv, kr_pad, w, *, nb, seq):
    t_rows = ckv.shape[0]
    tm = seq
    full = lambda a: pl.BlockSpec(a.shape, lambda i: (0,) * a.ndim)
    hp = HEADS * HEAD_PAD
    return pl.pallas_call(
        _cache_expand_kernel,
        out_shape=(jax.ShapeDtypeStruct((t_rows, hp), bf16), jax.ShapeDtypeStruct((nb, HEADS * V_DIM, seq), bf16)),
        grid=(t_rows // tm,),
        in_specs=[pl.BlockSpec((tm, KV_RANK), lambda i: (i, 0)), pl.BlockSpec((tm, LANES), lambda i: (i, 0)),
                  full(w["w_uk"]), full(w["w_uv"]), full(w["k_gain"])],
        out_specs=(pl.BlockSpec((tm, hp), lambda i: (i, 0)),
                   pl.BlockSpec((1, HEADS * V_DIM, tm), lambda i: (i, 0, 0))),
        compiler_params=_cparams("arbitrary"), name="mla_cache_expand",
    )(ckv, kr_pad, w["w_uk"], w["w_uv"], w["k_gain"])


def _attn_kernel(*refs, cached, tk):
    if cached:
        q_ref, k_ref, kc_ref, vt_ref, vct_ref, o_ref, s_scr, m_scr = refs
        parts = ((k_ref, vt_ref), (kc_ref, vct_ref))
    else:
        q_ref, k_ref, vt_ref, o_ref, s_scr, m_scr = refs
        parts = ((k_ref, vt_ref),)
    @pl.when(pl.program_id(0) == 0)
    def _():
        s_scr[...] = jnp.zeros_like(s_scr)
        m_scr[...] = jnp.zeros_like(m_scr)

    q = q_ref[0]
    m_old = m_scr[...]
    m_new = acc = None
    base = 0
    for kr, vtr in parts:
        ck = min(tk, kr.shape[0])
        ones = jnp.ones((16, ck), bf16)
        for j in range(kr.shape[0] // ck):
            rows = slice(base + j * ck, base + (j + 1) * ck)
            p = jnp.exp2(s_scr[rows, :] - m_old).astype(bf16)
            lhs = jnp.concatenate([vtr[0, :, j * ck:(j + 1) * ck], ones], axis=0)
            part = jnp.dot(lhs, p, preferred_element_type=f32)
            acc = part if acc is None else acc + part
            s = jnp.dot(kr[j * ck:(j + 1) * ck, :], q, preferred_element_type=f32)
            s_scr[rows, :] = s
            cm = jnp.max(s, axis=0, keepdims=True)
            m_new = cm if m_new is None else jnp.maximum(m_new, cm)
        base += kr.shape[0]
    m_scr[...] = m_new
    o_ref[0] = (acc[0:V_DIM] / acc[V_DIM:V_DIM + 1]).astype(bf16)


def _attention(q, k, vt, kc, vct, *, nb, seq, tq, tk=TOEP):
    cached = kc is not None
    nq = seq // tq
    n_tiles = nb * HEADS * nq
    past = kc.shape[0] // nb if cached else 0
    tk = min(seq, tk)

    def cur(n):
        n = jnp.minimum(n, n_tiles - 1)
        return n // (HEADS * nq), (n // nq) % HEADS, n % nq

    def prev(n):
        n = jnp.maximum(n - 1, 0)
        return n // (HEADS * nq), (n // nq) % HEADS, n % nq

    def spec(shape, fn):
        return pl.BlockSpec(shape, fn)

    kern = functools.partial(_attn_kernel, cached=cached, tk=tk)
    q_spec = spec((1, HEAD_PAD, tq), lambda n: cur(n))
    k_spec = lambda rows: spec((rows, HEAD_PAD), lambda n: (cur(n)[0], cur(n)[1]))
    v_spec = lambda cols: spec((1, V_DIM, cols), lambda n: (prev(n)[0], prev(n)[1], 0))
    if cached:
        ins = [q, k, kc, vt, vct]
        in_specs = [q_spec, k_spec(seq), k_spec(past), v_spec(seq), v_spec(past)]
    else:
        ins = [q, k, vt]
        in_specs = [q_spec, k_spec(seq), v_spec(seq)]
    return pl.pallas_call(
        kern,
        out_shape=jax.ShapeDtypeStruct((nb, HEADS * V_DIM, seq), bf16),
        grid=(n_tiles + 1,),
        in_specs=in_specs,
        out_specs=pl.BlockSpec((1, V_DIM, tq), lambda n: prev(n)),
        scratch_shapes=[pltpu.VMEM((seq + past, tq), f32), pltpu.VMEM((1, tq), f32)],
        compiler_params=_cparams("arbitrary"),
        name="mla_attention",
    )(*ins)


def _attn_short_kernel(q_ref, k_ref, vt_ref, o_ref):
    seq = k_ref.shape[0]
    ones = jnp.ones((16, seq), bf16)
    for hd in range(HEADS):
        cols = slice(hd * HEAD_PAD, (hd + 1) * HEAD_PAD)
        s = jnp.dot(k_ref[:, cols], q_ref[0, cols, :], preferred_element_type=f32)
        p = jnp.exp2(s - jnp.max(s, axis=0, keepdims=True)).astype(bf16)
        rows = slice(hd * V_DIM, (hd + 1) * V_DIM)
        acc = jnp.dot(jnp.concatenate([vt_ref[0, rows, :], ones], axis=0), p, preferred_element_type=f32)
        o_ref[0, rows, :] = (acc[0:V_DIM] / acc[V_DIM:V_DIM + 1]).astype(bf16)


def _attention_short(q, k, vt, *, nb, seq):
    return pl.pallas_call(
        _attn_short_kernel,
        out_shape=jax.ShapeDtypeStruct((nb, HEADS * V_DIM, seq), bf16),
        grid=(nb,),
        in_specs=[pl.BlockSpec((1, HEADS * HEAD_PAD, seq), lambda b: (b, 0, 0)),
                  pl.BlockSpec((seq, HEADS * HEAD_PAD), lambda b: (b, 0)),
                  pl.BlockSpec((1, HEADS * V_DIM, seq), lambda b: (b, 0, 0))],
        out_specs=pl.BlockSpec((1, HEADS * V_DIM, seq), lambda b: (b, 0, 0)),
        compiler_params=_cparams("arbitrary"),
        name="mla_attention_short",
    )(q, k, vt)


ROW_HALO = 16


def _mix_ffn_kernel(*refs, tm, tpb, chunks, kinds):
    x_ref, xp_ref, xn_ref, mod_ref, g_ref = refs[:5]
    wup_ref, cw_ref, cb_ref, wd_ref, o_ref = refs[5 + 4 * len(kinds):]
    i = pl.program_id(0)
    mod = mod_ref[0]
    tn = (((0,), (0,)), ((), ()))
    proj = [None, None, None]
    for n, kind in enumerate(kinds):
        main, prev, nxt, w_ref = refs[5 + 4 * n:9 + 4 * n]
        w = w_ref[...]
        if kind == "rows":
            parts = [jnp.dot(main[...], w, preferred_element_type=f32),
                     jnp.dot(prev[...], w, preferred_element_type=f32)[ROW_HALO - HALO:ROW_HALO],
                     jnp.dot(nxt[...], w, preferred_element_type=f32)[0:HALO]]
        else:
            ops = [main[0], prev[0][:, LANES - HALO:LANES], nxt[0][:, 0:HALO]]
            parts = [lax.dot_general(o.astype(bf16), w, tn, preferred_element_type=f32) for o in ops]
        proj = [p if q is None else q + p for q, p in zip(proj, parts)]
    g1 = mod[2:3]
    x1 = x_ref[...] + g1 * proj[0]
    h = _halo_ext(x1, xp_ref[...] + g1 * proj[1], xn_ref[...] + g1 * proj[2], g_ref[...], mod[4:5], mod[3:4],
                  (i % tpb) == 0, (i % tpb) == tpb - 1)
    dff = wd_ref.shape[0]
    acc = None
    for c0, c1 in chunks:
        halves = []
        for off in (c0, dff + c0):
            cols = slice(off, off + c1 - c0)
            y = jnp.dot(h, wup_ref[:, cols], preferred_element_type=f32)
            halves.append(_dwconv3(y, cw_ref[:, cols], cb_ref[:, cols], tm))
        gate, up = halves
        act = ((gate * jax.nn.sigmoid(gate)) * up).astype(bf16)
        part = jnp.dot(act, wd_ref[c0:c1, :], preferred_element_type=f32)
        acc = part if acc is None else acc + part
    o_ref[...] = x1 + mod[5:6] * acc


def _mix_ffn(x, mod, g, mixer_ops, w_up, conv_w, conv_b, w_down, *, seq, tm, chunk_tiles=6):
    t_rows, d = x.shape
    dff = w_down.shape[0]
    step = chunk_tiles * TOEP
    chunks = tuple((c, min(c + step, dff)) for c in range(0, dff, step))
    tpb = seq // tm
    mrows = seq if mod.shape[0] > 1 else t_rows
    kinds = tuple(k for k, _, _ in mixer_ops)
    kern = functools.partial(_mix_ffn_kernel, tm=tm, tpb=tpb, chunks=chunks, kinds=kinds)
    resident = lambda a: pl.BlockSpec(a.shape, lambda i: (0,) * a.ndim, pipeline_mode=pl.Buffered(1))
    ins, in_specs = [], []
    for kind, op, w in mixer_ops:
        c = op.shape[1]
        if kind == "rows":
            r, nblk = tm // ROW_HALO, t_rows // ROW_HALO
            in_specs += [pl.BlockSpec((tm, c), lambda i: (i, 0)),
                         pl.BlockSpec((ROW_HALO, c), lambda i, r=r: (jnp.maximum(i * r - 1, 0), 0)),
                         pl.BlockSpec((ROW_HALO, c), lambda i, r=r, nblk=nblk: (jnp.minimum((i + 1) * r, nblk - 1), 0))]
        else:
            r, nblk = tm // LANES, seq // LANES
            in_specs += [pl.BlockSpec((1, c, tm), lambda i: (i // tpb, 0, i % tpb)),
                         pl.BlockSpec((1, c, LANES), lambda i, r=r: (i // tpb, 0, jnp.maximum((i % tpb) * r - 1, 0))),
                         pl.BlockSpec((1, c, LANES),
                                      lambda i, r=r, nblk=nblk: (i // tpb, 0, jnp.minimum((i % tpb + 1) * r, nblk - 1)))]
        in_specs.append(resident(w))
        ins += [op, op, op, w]
    return pl.pallas_call(
        kern,
        out_shape=jax.ShapeDtypeStruct((t_rows, d), f32),
        grid=(t_rows // tm,),
        in_specs=[*_row_specs(tm, d, t_rows),
                  pl.BlockSpec((1, 6, d), lambda i: ((i * tm) // mrows, 0, 0)),
                  resident(g), *in_specs,
                  resident(w_up), resident(conv_w), resident(conv_b), resident(w_down)],
        out_specs=pl.BlockSpec((tm, d), lambda i: (i, 0)),
        compiler_params=_cparams("arbitrary"),
        name="mix_ffn",
    )(x, x, x, mod, g, *ins, w_up, conv_w, conv_b, w_down)


def _rope_tables(seq):
    half = ROPE // 2
    rows = seq // GRID_W
    row = np.repeat(np.arange(rows), GRID_W).astype(np.float32)
    col = np.tile(np.arange(GRID_W), rows).astype(np.float32)
    inv = (1.0 / (ROPE_BASE ** (np.arange(0, half, 2, dtype=np.float32) / half))).astype(np.float32)
    ar = (row[:, None] * inv[None]).astype(np.float64)
    ac = (col[:, None] * inv[None]).astype(np.float64)
    pad = np.zeros((seq, LANES - ROPE))
    cos64 = np.concatenate([np.cos(ar), np.cos(ar), np.cos(ac), np.cos(ac)], axis=1)
    sin64 = np.concatenate([-np.sin(ar), np.sin(ar), -np.sin(ac), np.sin(ac)], axis=1)
    as_f32 = lambda a: jnp.asarray(a.astype(np.float32))
    return (as_f32(np.concatenate([cos64, pad], axis=1)), as_f32(np.concatenate([sin64, pad], axis=1)),
            as_f32(cos64.T), as_f32(sin64.T))


def _pad_heads(a, width):
    lead = a.shape[:-1]
    a = a.reshape(*lead, HEADS, -1)
    a = jnp.pad(a, [(0, 0)] * len(lead) + [(0, 0), (0, width - a.shape[-1])])
    return a.reshape(*lead, HEADS * width)


def _mla_weights(j, mla_w_dq, mla_q_norm, mla_w_uq, mla_w_dkv, mla_kv_norm, mla_w_ukv, mla_q_head_norm,
                 mla_k_head_norm):
    ukv = mla_w_ukv[j].reshape(KV_RANK, HEADS, NOPE + V_DIM)
    kg = mla_k_head_norm[j]
    return {
        "w_dq": mla_w_dq[j].astype(bf16),
        "q_norm": mla_q_norm[j].reshape(1, -1),
        "w_uq": _pad_heads(mla_w_uq[j], HEAD_PAD).T.astype(bf16),
        "q_gain": jnp.pad(mla_q_head_norm[j], (0, HEAD_PAD - QK)),
        "w_dkv": jnp.pad(mla_w_dkv[j], ((0, 0), (0, LANES - ROPE))).astype(bf16),
        "kv_norm": mla_kv_norm[j].reshape(1, -1),
        "w_uk": ukv[:, :, :NOPE].reshape(KV_RANK, HEADS * NOPE).astype(bf16),
        "w_uv": ukv[:, :, NOPE:].reshape(KV_RANK, HEADS * V_DIM).T.astype(bf16),
        "k_gain": jnp.pad(kg, (0, 2 * NOPE - QK)).reshape(1, 2 * NOPE),
    }


def kernel(x_prompt, x_sample, cache_ckv, cache_krope, c, c_ctx, ada_w, ada_b, norm_g, mix_w_in, sgu_w, sgu_b, hy_conv_w, hy_conv_b, hy_f_w1, hy_f_b1, hy_f_w2, hy_f_b2, hy_f_w3, hy_f_freq, hy_decay, hy_d, mix_w_out, mla_w_dq, mla_q_norm, mla_w_uq, mla_w_dkv, mla_kv_norm, mla_w_ukv, mla_q_head_norm, mla_k_head_norm, mla_w_o, ffn_w_up, ffn_conv_w, ffn_conv_b, ffn_w_down):
    depth = ada_w.shape[0]
    d = x_prompt.shape[-1]
    nbp, seqp, _ = x_prompt.shape
    nbs, seqs, _ = x_sample.shape
    a_width = A_GROUPS * sgu_w.shape[-1]

    cond = jnp.zeros((16, d), f32).at[0].set(c_ctx).at[1:1 + nbs].set(c)
    mods = _ada(cond, ada_w, ada_b).reshape(depth, 16, 6, d)

    n_even = mix_w_in.shape[0]
    w_in = [mix_w_in[i].astype(bf16) for i in range(n_even)]
    sgu_wb = sgu_w.astype(bf16)
    sgu_bb = jnp.broadcast_to(sgu_b[..., None], sgu_b.shape + (sgu_w.shape[-1],))
    w_out_a = [mix_w_out[i, :a_width].astype(bf16) for i in range(n_even)]
    w_out_z = [mix_w_out[i, a_width:].astype(bf16) for i in range(n_even)]
    w_up = [ffn_w_up[l].astype(bf16) for l in range(depth)]
    w_down = [ffn_w_down[l].astype(bf16) for l in range(depth)]
    w_o = [mla_w_o[j].astype(bf16) for j in range(depth // 2)]
    mla_w = [_mla_weights(j, mla_w_dq, mla_q_norm, mla_w_uq, mla_w_dkv, mla_kv_norm, mla_w_ukv,
                          mla_q_head_norm, mla_k_head_norm) for j in range(depth // 2)]

    def trunk(x3, mod_all, cache):
        nb, seq, _ = x3.shape
        x = x3.reshape(nb * seq, d)
        tm = min(seq, 512)
        latent = cache is not None
        tabs = _rope_tables(seq) if latent else None
        new_ckv, new_kr = [], []
        for l in range(depth):
            mod = mod_all[l]
            g1 = norm_g[l, 0].reshape(1, d)
            g2 = norm_g[l, 1].reshape(1, d)
            if l % 2 == 0:
                i = l // 2
                a, xbt = _even_front(x, mod, g1, w_in[i], sgu_wb[i], sgu_bb[i], hy_conv_w[i],
                                     hy_conv_b[i].reshape(1, -1), nb=nb, seq=seq, tm=tm)
                filt = _hyena_filters_t(seq, hy_f_w1[i], hy_f_b1[i], hy_f_w2[i], hy_f_b2[i], hy_f_w3[i],
                                        hy_f_freq[i], hy_decay[i])
                zt = _hyena(xbt, filt, hy_d[i].reshape(-1), nb=nb, seq=seq)
                mixer_ops = [("rows", a, w_out_a[i]), ("cols", zt, w_out_z[i])]
            else:
                j = l // 2
                outs = _mla_front(x, mod, g1, mla_w[j], tabs, nb=nb, seq=seq, tm=tm, emit_cache=not latent)
                q, k, v = outs[:3]
                if latent:
                    past = cache[0].shape[2]
                    ckv_c = cache[0][:, j].reshape(nb * past, KV_RANK)
                    kr_c = jnp.pad(cache[1][:, j].reshape(nb * past, ROPE), ((0, 0), (0, LANES - ROPE)))
                    kc, vc = _cache_expand(ckv_c, kr_c, mla_w[j], nb=nb, seq=past)
                else:
                    kc = vc = None
                    new_ckv.append(outs[3].reshape(nb, seq, KV_RANK))
                    new_kr.append(outs[4].reshape(nb, seq, ROPE))
                if kc is None and seq <= TOEP:
                    at = _attention_short(q, k, v, nb=nb, seq=seq)
                else:
                    at = _attention(q, k, v, kc, vc, nb=nb, seq=seq, tq=min(seq, 1024))
                mixer_ops = [("cols", at, w_o[j])]
            x = _mix_ffn(x, mod, g2, mixer_ops, w_up[l], ffn_conv_w[l], ffn_conv_b[l].reshape(1, -1), w_down[l],
                         seq=seq, tm=tm)
        return x.reshape(nb, seq, d), new_ckv, new_kr

    y_prompt, ckv_list, kr_list = trunk(x_prompt, mods[:, 0:1], None)
    y_sample, _, _ = trunk(x_sample, mods[:, 1:1 + nbs], (cache_ckv, cache_krope))
    return (y_prompt, y_sample, jnp.stack(ckv_list, axis=1), jnp.stack(kr_list, axis=1))
```

```python
import functools
import math

import jax
import jax.numpy as jnp
import numpy as np
from jax import lax
from jax.experimental import pallas as pl
from jax.experimental.pallas import tpu as pltpu

f32 = jnp.float32
bf16 = jnp.bfloat16

EPS = 1e-6
GRID_W = 64
CHUNK = 128
A_GROUPS = 4
HEADS = 8
NOPE = 128
ROPE = 64
QK = NOPE + ROPE
HEAD_PAD = 256
V_DIM = 128
KV_RANK = 256
ROPE_BASE = 10000.0
FILTER_BANDS = 16
LANES = 128
HALO = 8
TOEP = 256
VMEM_LIMIT = 56 * 1024 * 1024
HIGHEST = lax.Precision.HIGHEST


def _cparams(*sem):
    return pltpu.CompilerParams(dimension_semantics=sem, vmem_limit_bytes=VMEM_LIMIT)


def _norm_mod(x, g, sc, sh):
    y = x * lax.rsqrt(jnp.mean(x * x, axis=-1, keepdims=True) + EPS)
    return (y * g) * (1.0 + sc) + sh


def _halo_ext(x, xp, xn, g, sc, sh, first, last):
    hp = jnp.where(first, 0.0, _norm_mod(xp, g, sc, sh))
    hn = jnp.where(last, 0.0, _norm_mod(xn, g, sc, sh))
    h = _norm_mod(x, g, sc, sh)
    return jnp.concatenate([hp, h, hn], axis=0).astype(bf16)


def _dwconv3(y, w, b, tm):
    return (y[HALO - 1:HALO - 1 + tm] * w[0:1] + y[HALO:HALO + tm] * w[1:2]
            + y[HALO + 1:HALO + 1 + tm] * w[2:3] + b)


def _gelu_tanh(x):
    return 0.5 * x * (1.0 + jnp.tanh(math.sqrt(2.0 / math.pi) * (x + 0.044715 * (x * x * x))))


def _row_specs(tm, d, t_rows):
    r = tm // HALO
    nb = t_rows // HALO
    return (pl.BlockSpec((tm, d), lambda i, *_: (i, 0)),
            pl.BlockSpec((HALO, d), lambda i, *_: (jnp.maximum(i * r - 1, 0), 0)),
            pl.BlockSpec((HALO, d), lambda i, *_: (jnp.minimum((i + 1) * r, nb - 1), 0)))


def _ada_kernel(c_ref, w_ref, b_ref, o_ref):
    c = c_ref[...]
    s = c * jax.nn.sigmoid(c)
    o_ref[0] = jnp.dot(s, w_ref[0], precision=HIGHEST, preferred_element_type=f32) + b_ref[0]


def _ada(cond, ada_w, ada_b):
    depth, d, n = ada_w.shape
    tn = n // 4
    rows = cond.shape[0]
    return pl.pallas_call(
        _ada_kernel,
        out_shape=jax.ShapeDtypeStruct((depth, rows, n), f32),
        grid=(depth, n // tn),
        in_specs=[pl.BlockSpec((rows, d), lambda l, j: (0, 0)),
                  pl.BlockSpec((1, d, tn), lambda l, j: (l, 0, j)),
                  pl.BlockSpec((1, 1, tn), lambda l, j: (l, 0, j))],
        out_specs=pl.BlockSpec((1, rows, tn), lambda l, j: (l, 0, j)),
        compiler_params=_cparams("arbitrary", "arbitrary"),
        name="ada_mod",
    )(cond, ada_w, ada_b.reshape(depth, 1, n))


def _even_front_kernel(x_ref, xp_ref, xn_ref, mod_ref, g_ref, win_ref, sw_ref, sb_ref, cw_ref, cb_ref,
                       a_ref, xbt_ref, *, tm, tpb, a_width):
    i = pl.program_id(0)
    mod = mod_ref[0]
    h = _halo_ext(x_ref[...], xp_ref[...], xn_ref[...], g_ref[...], mod[1:2], mod[0:1],
                  (i % tpb) == 0, (i % tpb) == tpb - 1)
    p = jnp.dot(h, win_ref[...], preferred_element_type=f32)
    u = _gelu_tanh(p[HALO:HALO + tm, 0:a_width])
    v = _gelu_tanh(p[HALO:HALO + tm, a_width:2 * a_width]).astype(bf16)
    nch = tm // CHUNK
    a_ch = a_width // A_GROUPS
    for gi in range(A_GROUPS):
        cols = slice(gi * a_ch, (gi + 1) * a_ch)
        rhs = jnp.concatenate([v[n * CHUNK:(n + 1) * CHUNK, cols] for n in range(nch)], axis=1)
        s = jnp.dot(sw_ref[gi], rhs, preferred_element_type=f32)
        for n in range(nch):
            rows = slice(n * CHUNK, (n + 1) * CHUNK)
            sn = s[:, n * a_ch:(n + 1) * a_ch] + sb_ref[gi]
            a_ref[rows, cols] = (u[rows, cols] * sn).astype(bf16)
    xb = _dwconv3(p[:, 2 * a_width:], cw_ref[...], cb_ref[...], tm)
    xbt_ref[0] = xb.T


def _even_front(x, mod, g, w_in, sgu_w, sgu_b, conv_w, conv_b, *, nb, seq, tm):
    t_rows, d = x.shape
    tpb = seq // tm
    n_in = w_in.shape[1]
    a_width = A_GROUPS * sgu_w.shape[1]
    nxb = n_in - 2 * a_width
    mrows = seq if mod.shape[0] > 1 else t_rows
    kern = functools.partial(_even_front_kernel, tm=tm, tpb=tpb, a_width=a_width)
    full = lambda shape: pl.BlockSpec(shape, lambda i: (0,) * len(shape))
    return pl.pallas_call(
        kern,
        out_shape=(jax.ShapeDtypeStruct((t_rows, a_width), bf16),
                   jax.ShapeDtypeStruct((nb, nxb, seq), f32)),
        grid=(t_rows // tm,),
        in_specs=[*_row_specs(tm, d, t_rows),
                  pl.BlockSpec((1, 6, d), lambda i: ((i * tm) // mrows, 0, 0)),
                  full((1, d)), full(w_in.shape), full(sgu_w.shape), full(sgu_b.shape),
                  full(conv_w.shape), full(conv_b.shape)],
        out_specs=(pl.BlockSpec((tm, a_width), lambda i: (i, 0)),
                   pl.BlockSpec((1, nxb, tm), lambda i: (i // tpb, 0, i % tpb))),
        compiler_params=_cparams("arbitrary"),
        name="even_front",
    )(x, x, x, mod, g, w_in, sgu_w, sgu_b, conv_w, conv_b)


def _filter_kernel(w1t_ref, b1_ref, w2t_ref, b2_ref, fr_ref, w3t_ref, dec_ref, o_ref, hid_scr, *, seq):
    @pl.when(pl.program_id(0) == 0)
    def _():
        nfeat = w1t_ref.shape[1]
        t = lax.broadcasted_iota(jnp.int32, (nfeat, seq), 1).astype(f32)
        fi = lax.broadcasted_iota(jnp.int32, (nfeat, seq), 0)
        tn = t / seq
        band = jnp.where(fi <= FILTER_BANDS, fi, fi - FILTER_BANDS).astype(f32)
        ang = ((2.0 * math.pi) * tn) * band
        z = jnp.where(fi == 0, tn, jnp.where(fi <= FILTER_BANDS, jnp.sin(ang), jnp.cos(ang)))
        z = jnp.where(fi <= 2 * FILTER_BANDS, z, 0.0)
        fr = fr_ref[...]
        h1 = jnp.sin(fr * (jnp.dot(w1t_ref[...], z, precision=HIGHEST, preferred_element_type=f32) + b1_ref[...]))
        hid_scr[...] = jnp.sin(fr * (jnp.dot(w2t_ref[...], h1, precision=HIGHEST, preferred_element_type=f32)
                                     + b2_ref[...]))

    h = jnp.dot(w3t_ref[...], hid_scr[...], precision=HIGHEST, preferred_element_type=f32)
    tl = lax.broadcasted_iota(jnp.int32, (1, seq), 1).astype(f32)
    dist = jnp.abs(tl - (seq // 2)) / seq
    h = h * jnp.exp(-jnp.abs(dec_ref[...]) * dist)
    h = h / (jnp.sum(jnp.abs(h), axis=1, keepdims=True) + EPS)
    hb = h.astype(bf16).astype(f32)
    lane = lax.broadcasted_iota(jnp.int32, hb.shape, 1)
    rolled = pltpu.roll(hb, 1, 1)
    prev_bits = lax.bitcast_convert_type(rolled, jnp.int32)
    cur_bits = lax.shift_right_logical(lax.bitcast_convert_type(hb, jnp.int32), 16)
    o_ref[:, 0:seq] = cur_bits | jnp.where(lane == 0, 0, prev_bits)
    tail_lane = lax.broadcasted_iota(jnp.int32, (hb.shape[0], LANES), 1)
    wrapped = lax.bitcast_convert_type(rolled[:, 0:LANES], jnp.int32)
    o_ref[:, seq:seq + LANES] = jnp.where(tail_lane == 0, wrapped, 0)


def _hyena_filters_t(seq, w1, b1, w2, b2, w3, freq, decay):
    nfeat = 40
    hid = w1.shape[1]
    w1t = jnp.zeros((hid, nfeat), f32).at[:, :w1.shape[0]].set(w1.T)
    rows = w3.shape[1]
    rb = 256
    kern = functools.partial(_filter_kernel, seq=seq)
    full = lambda shape: pl.BlockSpec(shape, lambda i: (0,) * len(shape))
    return pl.pallas_call(
        kern,
        out_shape=jax.ShapeDtypeStruct((rows, seq + LANES), jnp.int32),
        grid=(rows // rb,),
        in_specs=[full((hid, nfeat)), full((hid, 1)), full((hid, hid)), full((hid, 1)), full((hid, 1)),
                  pl.BlockSpec((rb, hid), lambda i: (i, 0)), pl.BlockSpec((rb, 1), lambda i: (i, 0))],
        out_specs=pl.BlockSpec((rb, seq + LANES), lambda i: (i, 0)),
        scratch_shapes=[pltpu.VMEM((hid, seq), f32)],
        compiler_params=_cparams("arbitrary"),
        name="hyena_filter",
    )(w1t, b1.reshape(hid, 1), w2.T, b2.reshape(hid, 1), freq.reshape(hid, 1), w3.T, decay.reshape(rows, 1))


def _bank_steps(hrow, bank_ref, seq):
    nseg = seq // LANES
    hl = LANES // 2
    upper = lax.broadcasted_iota(jnp.int32, (hl, LANES), 1) >= 2 * lax.broadcasted_iota(jnp.int32, (hl, LANES), 0)
    zero = jnp.zeros((hl, LANES), jnp.int32)
    bank_ref[0] = zero
    bank_ref[nseg + 2] = zero
    prev = zero
    for k in range(nseg + 1):
        seg = jnp.broadcast_to(hrow[:, k * LANES:(k + 1) * LANES], (hl, LANES))
        cur = pltpu.roll(seg, 0, 1, stride=2, stride_axis=0)
        bank_ref[k + 1] = jnp.where(upper, cur, prev)
        prev = cur
        yield


def _conv_steps(u, bank_ref, upad_ref, out, *, nb, seq):
    nseg = seq // LANES
    nblk = seq // TOEP
    half = seq // 2
    upad_ref[:, 0:half] = jnp.zeros((nb, half), f32)
    upad_ref[:, half:half + seq] = u
    upad_ref[:, half + seq:2 * seq] = jnp.zeros((nb, half), f32)
    acc = None
    for kt in range(nblk + 1):
        k0 = nseg - 2 * kt
        lhs = jnp.concatenate([upad_ref[:, (i + kt) * TOEP:(i + kt + 1) * TOEP] for i in range(nblk)], axis=0)
        blk = lambda idx: pltpu.bitcast(bank_ref[idx], bf16)
        top = jnp.concatenate([blk(k0 + 1), blk(k0 + 2)], axis=1)
        bot = jnp.concatenate([blk(k0), blk(k0 + 1)], axis=1)
        w = jnp.concatenate([top, bot], axis=0)
        part = jnp.dot(lhs.astype(bf16), w, preferred_element_type=f32)
        acc = part if acc is None else acc + part
        yield
    out.append(jnp.concatenate([acc[i * nb:(i + 1) * nb] for i in range(nblk)], axis=1))


def _interleave(main, side, ratio):
    main, side = list(main), list(side)
    while main or side:
        main = [g for g in main if next(g, StopIteration) is not StopIteration]
        for _ in range(ratio if main else 1 << 30):
            side = [g for g in side if next(g, StopIteration) is not StopIteration]
            if not side:
                break


def _hyena_kernel(d_ref, vb_ref, x1_ref, x2_ref, h0_ref, h1_ref, z_ref,
                  u_scr, a_scr, b_scr, o_scr, bank00, bank01, bank10, bank11, upad0, upad1, *, nb, seq, cg, nchan):
    gidx = pl.program_id(0)
    bank_ref = ((bank00, bank01), (bank10, bank11))
    upad_ref = (upad0, upad1)
    u_scr[...] = jnp.swapaxes(vb_ref[...], 0, 1)
    a_scr[...] = jnp.swapaxes(x1_ref[...], 0, 1)
    b_scr[...] = jnp.swapaxes(x2_ref[...], 0, 1)

    def banks(c):
        gens = []
        if c < cg:
            gens.append(_bank_steps(h0_ref[c:c + 1, :], bank_ref[c % 2][0], seq))
        if 1 <= c <= cg:
            gens.append(_bank_steps(h1_ref[c - 1:c, :], bank_ref[c % 2][1], seq))
        return gens

    ratio = -(-(seq // LANES + 1) // (seq // TOEP + 1))
    _interleave([], banks(0), ratio)
    z_prev = None
    for c in range(cg + 1):
        convs, first, second = [], [], []
        if c < cg:
            u = u_scr[c]
            convs.append(_conv_steps(u, bank_ref[c % 2][0], upad_ref[0], first, nb=nb, seq=seq))
        if c >= 1:
            convs.append(_conv_steps(z_prev, bank_ref[c % 2][1], upad_ref[1], second, nb=nb, seq=seq))
        _interleave(convs, banks(c + 1), ratio)
        if c >= 1:
            o_scr[c - 1] = b_scr[c - 1] * (second[0] + z_prev * d_ref[nchan + gidx * cg + c - 1])
        if c < cg:
            z_prev = a_scr[c] * (first[0] + u * d_ref[gidx * cg + c])
    z_ref[...] = jnp.swapaxes(o_scr[...], 0, 1)


def _hyena(xbt, filt_t, d_flat, *, nb, seq, cg=8):
    nchan = xbt.shape[1] // 3
    ng = nchan // cg
    kern = functools.partial(_hyena_kernel, nb=nb, seq=seq, cg=cg, nchan=nchan)
    act = lambda off: pl.BlockSpec((nb, cg, seq), lambda i: (0, off + i, 0))
    wseq = seq + LANES
    return pl.pallas_call(
        kern,
        out_shape=jax.ShapeDtypeStruct((nb, nchan, seq), f32),
        grid=(ng,),
        in_specs=[pl.BlockSpec(memory_space=pltpu.SMEM),
                  act(0), act(ng), act(2 * ng),
                  pl.BlockSpec((cg, wseq), lambda i: (i, 0)),
                  pl.BlockSpec((cg, wseq), lambda i: (ng + i, 0))],
        out_specs=pl.BlockSpec((nb, cg, seq), lambda i: (0, i, 0)),
        scratch_shapes=[pltpu.VMEM((cg, nb, seq), f32), pltpu.VMEM((cg, nb, seq), f32),
                        pltpu.VMEM((cg, nb, seq), f32), pltpu.VMEM((cg, nb, seq), f32),
                        *[pltpu.VMEM((seq // LANES + 3, LANES // 2, LANES), jnp.int32) for _ in range(4)],
                        pltpu.VMEM((nb, 2 * seq), f32), pltpu.VMEM((nb, 2 * seq), f32)],
        compiler_params=_cparams("arbitrary"),
        name="hyena_conv",
    )(d_flat, xbt, xbt, xbt, filt_t, filt_t)


def _rope128(x, cos, sin):
    lane = lax.broadcasted_iota(jnp.int32, x.shape, 1)
    swapped = jnp.where((lane & 16) == 0, pltpu.roll(x, LANES - 16, 1), pltpu.roll(x, 16, 1))
    return x * cos + swapped * sin


def _expand_kv(ckv, kr_pad, wuk_ref, wuv_ref, kg_ref, cos, sin, k_ref, v_ref, rows=slice(None)):
    cb = ckv.astype(bf16)
    kn = jnp.dot(cb, wuk_ref[...], preferred_element_type=f32)
    v_ref[0, :, rows] = jnp.dot(wuv_ref[...], ckv.T.astype(bf16), preferred_element_type=f32).astype(bf16)
    kg = kg_ref[...]
    g_nope, g_rope = kg[:, 0:NOPE], kg[:, NOPE:2 * NOPE]
    kr_ss = jnp.sum(kr_pad * kr_pad, axis=-1, keepdims=True)
    krg = kr_pad * g_rope
    if cos is not None:
        krg = _rope128(krg, cos, sin)
    for hd in range(HEADS):
        kh = kn[:, hd * NOPE:(hd + 1) * NOPE]
        r = lax.rsqrt((jnp.sum(kh * kh, axis=-1, keepdims=True) + kr_ss) / QK + EPS)
        k_ref[rows, hd * HEAD_PAD:hd * HEAD_PAD + NOPE] = (kh * r * g_nope).astype(bf16)
        k_ref[rows, hd * HEAD_PAD + NOPE:(hd + 1) * HEAD_PAD] = (krg * r).astype(bf16)


def _mla_front_kernel(*refs, rope, emit_cache, sub):
    (x_ref, mod_ref, g_ref, wdq_ref, qn_ref, wuq_ref, qg_ref, wdkv_ref, kvn_ref, wuk_ref, wuv_ref, kg_ref) = refs[:12]
    refs = refs[12:]
    if rope:
        cos_ref, sin_ref, qcos_ref, qsin_ref = refs[:4]
        refs = refs[4:]
    q_ref, k_ref, v_ref = refs[:3]
    mod = mod_ref[0]
    c_exp = (1.0 / math.sqrt(QK)) * math.log2(math.e)
    g16 = ROPE // 4
    tm = x_ref.shape[0]
    for r0 in range(0, tm, sub):
        rows = slice(r0, r0 + sub)
        cos = sin = None
        if rope:
            cos, sin, qcos, qsin = cos_ref[rows, :], sin_ref[rows, :], qcos_ref[:, rows], qsin_ref[:, rows]
        h = _norm_mod(x_ref[rows, :], g_ref[...], mod[1:2], mod[0:1]).astype(bf16)
        ql = jnp.dot(h, wdq_ref[...], preferred_element_type=f32)
        ql = ql * lax.rsqrt(jnp.mean(ql * ql, axis=-1, keepdims=True) + EPS) * qn_ref[...]
        qt = jnp.dot(wuq_ref[...], ql.T.astype(bf16), preferred_element_type=f32)
        qg = qg_ref[:, rows]
        for hd in range(HEADS):
            qh = qt[hd * HEAD_PAD:(hd + 1) * HEAD_PAD]
            ssq = jnp.sum(qh * qh, axis=0, keepdims=True)
            qh = qh * (lax.rsqrt(ssq / QK + EPS) * c_exp) * qg
            if rope:
                x = qh[NOPE:QK]
                swapped = jnp.concatenate([x[g16:2 * g16], x[0:g16], x[3 * g16:], x[2 * g16:3 * g16]], axis=0)
                qh = jnp.concatenate([qh[0:NOPE], x * qcos + swapped * qsin, qh[QK:]], axis=0)
            q_ref[0, hd * HEAD_PAD:(hd + 1) * HEAD_PAD, rows] = qh.astype(bf16)
        dkv = jnp.dot(h, wdkv_ref[...], preferred_element_type=f32)
        c_raw = dkv[:, 0:KV_RANK]
        ckv = c_raw * lax.rsqrt(jnp.mean(c_raw * c_raw, axis=-1, keepdims=True) + EPS) * kvn_ref[...]
        kr_pad = dkv[:, KV_RANK:]
        if emit_cache:
            ckv_ref, kr_ref = refs[3:5]
            ckv_ref[rows, :] = ckv
            kr_ref[rows, :] = kr_pad[:, 0:ROPE]
        _expand_kv(ckv, kr_pad, wuk_ref, wuv_ref, kg_ref, cos, sin, k_ref, v_ref, rows)


def _mla_front(x, mod, g, w, tabs, *, nb, seq, tm, emit_cache, sub=256):
    t_rows, d = x.shape
    tpb = seq // tm
    rope = tabs is not None
    mrows = seq if mod.shape[0] > 1 else t_rows
    kern = functools.partial(_mla_front_kernel, rope=rope, emit_cache=emit_cache, sub=min(sub, tm))
    full = lambda a: pl.BlockSpec(a.shape, lambda i: (0,) * a.ndim)
    wnames = ("w_dq", "q_norm", "w_uq", "q_gain", "w_dkv", "kv_norm", "w_uk", "w_uv", "k_gain")
    w = dict(w, q_gain=jnp.broadcast_to(w["q_gain"].reshape(HEAD_PAD, 1), (HEAD_PAD, tm)))
    ins = [x, mod, g] + [w[n] for n in wnames]
    in_specs = [pl.BlockSpec((tm, d), lambda i: (i, 0)),
                pl.BlockSpec((1, 6, d), lambda i: ((i * tm) // mrows, 0, 0)),
                full(g)] + [full(w[n]) for n in wnames]
    if rope:
        ins += list(tabs)
        in_specs += [pl.BlockSpec((tm, LANES), lambda i: (i % tpb, 0))] * 2
        in_specs += [pl.BlockSpec((ROPE, tm), lambda i: (0, i % tpb))] * 2
    hp = HEADS * HEAD_PAD
    out_shape = [jax.ShapeDtypeStruct((nb, hp, seq), bf16), jax.ShapeDtypeStruct((t_rows, hp), bf16),
                 jax.ShapeDtypeStruct((nb, HEADS * V_DIM, seq), bf16)]
    out_specs = [pl.BlockSpec((1, hp, tm), lambda i: (i // tpb, 0, i % tpb)), pl.BlockSpec((tm, hp), lambda i: (i, 0)),
                 pl.BlockSpec((1, HEADS * V_DIM, tm), lambda i: (i // tpb, 0, i % tpb))]
    if emit_cache:
        out_shape += [jax.ShapeDtypeStruct((t_rows, KV_RANK), f32), jax.ShapeDtypeStruct((t_rows, ROPE), f32)]
        out_specs += [pl.BlockSpec((tm, KV_RANK), lambda i: (i, 0)), pl.BlockSpec((tm, ROPE), lambda i: (i, 0))]
    return pl.pallas_call(
        kern, out_shape=tuple(out_shape), grid=(t_rows // tm,), in_specs=in_specs, out_specs=tuple(out_specs),
        compiler_params=_cparams("arbitrary"), name="mla_front",
    )(*ins)


def _cache_expand_kernel(ckv_ref, kr_ref, wuk_ref, wuv_ref, kg_ref, k_ref, v_ref):
    _expand_kv(ckv_ref[...], kr_ref[...], wuk_ref, wuv_ref, kg_ref, None, None, k_ref, v_ref)


def _cache_expand(ckv, kr_pad, w, *, nb, seq):
    t_rows = ckv.shape[0]
    tm = seq
    full = lambda a: pl.BlockSpec(a.shape, lambda i: (0,) * a.ndim)
    hp = HEADS * HEAD_PAD
    return pl.pallas_call(
        _cache_expand_kernel,
        out_shape=(jax.ShapeDtypeStruct((t_rows, hp), bf16), jax.ShapeDtypeStruct((nb, HEADS * V_DIM, seq), bf16)),
        grid=(t_rows // tm,),
        in_specs=[pl.BlockSpec((tm, KV_RANK), lambda i: (i, 0)), pl.BlockSpec((tm, LANES), lambda i: (i, 0)),
                  full(w["w_uk"]), full(w["w_uv"]), full(w["k_gain"])],
        out_specs=(pl.BlockSpec((tm, hp), lambda i: (i, 0)),
                   pl.BlockSpec((1, HEADS * V_DIM, tm), lambda i: (i, 0, 0))),
        compiler_params=_cparams("arbitrary"), name="mla_cache_expand",
    )(ckv, kr_pad, w["w_uk"], w["w_uv"], w["k_gain"])


def _attn_kernel(*refs, cached, tk):
    if cached:
        q_ref, k_ref, kc_ref, vt_ref, vct_ref, o_ref, s_scr, m_scr = refs
        parts = ((k_ref, vt_ref), (kc_ref, vct_ref))
    else:
        q_ref, k_ref, vt_ref, o_ref, s_scr, m_scr = refs
        parts = ((k_ref, vt_ref),)
    @pl.when(pl.program_id(0) == 0)
    def _():
        s_scr[...] = jnp.zeros_like(s_scr)
        m_scr[...] = jnp.zeros_like(m_scr)

    q = q_ref[0]
    m_old = m_scr[...]
    m_new = acc = None
    base = 0
    for kr, vtr in parts:
        ck = min(tk, kr.shape[0])
        ones = jnp.ones((16, ck), bf16)
        for j in range(kr.shape[0] // ck):
            rows = slice(base + j * ck, base + (j + 1) * ck)
            p = jnp.exp2(s_scr[rows, :] - m_old).astype(bf16)
            lhs = jnp.concatenate([vtr[0, :, j * ck:(j + 1) * ck], ones], axis=0)
            part = jnp.dot(lhs, p, preferred_element_type=f32)
            acc = part if acc is None else acc + part
            s = jnp.dot(kr[j * ck:(j + 1) * ck, :], q, preferred_element_type=f32)
            s_scr[rows, :] = s
            cm = jnp.max(s, axis=0, keepdims=True)
            m_new = cm if m_new is None else jnp.maximum(m_new, cm)
        base += kr.shape[0]
    m_scr[...] = m_new
    o_ref[0] = (acc[0:V_DIM] / acc[V_DIM:V_DIM + 1]).astype(bf16)


def _attention(q, k, vt, kc, vct, *, nb, seq, tq, tk=TOEP):
    cached = kc is not None
    nq = seq // tq
    n_tiles = nb * HEADS * nq
    past = kc.shape[0] // nb if cached else 0
    tk = min(seq, tk)

    def cur(n):
        n = jnp.minimum(n, n_tiles - 1)
        return n // (HEADS * nq), (n // nq) % HEADS, n % nq

    def prev(n):
        n = jnp.maximum(n - 1, 0)
        return n // (HEADS * nq), (n // nq) % HEADS, n % nq

    def spec(shape, fn):
        return pl.BlockSpec(shape, fn)

    kern = functools.partial(_attn_kernel, cached=cached, tk=tk)
    q_spec = spec((1, HEAD_PAD, tq), lambda n: cur(n))
    k_spec = lambda rows: spec((rows, HEAD_PAD), lambda n: (cur(n)[0], cur(n)[1]))
    v_spec = lambda cols: spec((1, V_DIM, cols), lambda n: (prev(n)[0], prev(n)[1], 0))
    if cached:
        ins = [q, k, kc, vt, vct]
        in_specs = [q_spec, k_spec(seq), k_spec(past), v_spec(seq), v_spec(past)]
    else:
        ins = [q, k, vt]
        in_specs = [q_spec, k_spec(seq), v_spec(seq)]
    return pl.pallas_call(
        kern,
        out_shape=jax.ShapeDtypeStruct((nb, HEADS * V_DIM, seq), bf16),
        grid=(n_tiles + 1,),
        in_specs=in_specs,
        out_specs=pl.BlockSpec((1, V_DIM, tq), lambda n: prev(n)),
        scratch_shapes=[pltpu.VMEM((seq + past, tq), f32), pltpu.VMEM((1, tq), f32)],
        compiler_params=_cparams("arbitrary"),
        name="mla_attention",
    )(*ins)


def _attn_short_kernel(q_ref, k_ref, vt_ref, o_ref):
    seq = k_ref.shape[0]
    ones = jnp.ones((16, seq), bf16)
    for hd in range(HEADS):
        cols = slice(hd * HEAD_PAD, (hd + 1) * HEAD_PAD)
        s = jnp.dot(k_ref[:, cols], q_ref[0, cols, :], preferred_element_type=f32)
        p = jnp.exp2(s - jnp.max(s, axis=0, keepdims=True)).astype(bf16)
        rows = slice(hd * V_DIM, (hd + 1) * V_DIM)
        acc = jnp.dot(jnp.concatenate([vt_ref[0, rows, :], ones], axis=0), p, preferred_element_type=f32)
        o_ref[0, rows, :] = (acc[0:V_DIM] / acc[V_DIM:V_DIM + 1]).astype(bf16)


def _attention_short(q, k, vt, *, nb, seq):
    return pl.pallas_call(
        _attn_short_kernel,
        out_shape=jax.ShapeDtypeStruct((nb, HEADS * V_DIM, seq), bf16),
        grid=(nb,),
        in_specs=[pl.BlockSpec((1, HEADS * HEAD_PAD, seq), lambda b: (b, 0, 0)),
                  pl.BlockSpec((seq, HEADS * HEAD_PAD), lambda b: (b, 0)),
                  pl.BlockSpec((1, HEADS * V_DIM, seq), lambda b: (b, 0, 0))],
        out_specs=pl.BlockSpec((1, HEADS * V_DIM, seq), lambda b: (b, 0, 0)),
        compiler_params=_cparams("arbitrary"),
        name="mla_attention_short",
    )(q, k, vt)


ROW_HALO = 16


def _mix_ffn_kernel(*refs, tm, tpb, chunks, kinds, sub):
    x_ref, xp_ref, xn_ref, mod_ref, g_ref = refs[:5]
    wup_ref, cw_ref, cb_ref, wd_ref, o_ref = refs[5 + 4 * len(kinds):]
    i = pl.program_id(0)
    mod = mod_ref[0]
    tn = (((0,), (0,)), ((), ()))
    proj = [None, None, None]
    for n, kind in enumerate(kinds):
        main, prev, nxt, w_ref = refs[5 + 4 * n:9 + 4 * n]
        w = w_ref[...]
        if kind == "rows":
            parts = [jnp.dot(main[...], w, preferred_element_type=f32),
                     jnp.dot(prev[...], w, preferred_element_type=f32)[ROW_HALO - HALO:ROW_HALO],
                     jnp.dot(nxt[...], w, preferred_element_type=f32)[0:HALO]]
        else:
            ops = [main[0], prev[0][:, LANES - HALO:LANES], nxt[0][:, 0:HALO]]
            parts = [lax.dot_general(o.astype(bf16), w, tn, preferred_element_type=f32) for o in ops]
        proj = [p if q is None else q + p for q, p in zip(proj, parts)]
    g1 = mod[2:3]
    x1 = x_ref[...] + g1 * proj[0]
    h = _halo_ext(x1, xp_ref[...] + g1 * proj[1], xn_ref[...] + g1 * proj[2], g_ref[...], mod[4:5], mod[3:4],
                  (i % tpb) == 0, (i % tpb) == tpb - 1)
    dff = wd_ref.shape[0]
    for r0 in range(0, tm, sub):
        hs = h[r0:r0 + sub + 2 * HALO]
        acc = None
        for c0, c1 in chunks:
            halves = []
            for off in (c0, dff + c0):
                cols = slice(off, off + c1 - c0)
                y = jnp.dot(hs, wup_ref[:, cols], preferred_element_type=f32)
                halves.append(_dwconv3(y, cw_ref[:, cols], cb_ref[:, cols], sub))
            gate, up = halves
            act = ((gate * jax.nn.sigmoid(gate)) * up).astype(bf16)
            part = jnp.dot(act, wd_ref[c0:c1, :], preferred_element_type=f32)
            acc = part if acc is None else acc + part
        o_ref[r0:r0 + sub, :] = x1[r0:r0 + sub] + mod[5:6] * acc


def _mix_ffn(x, mod, g, mixer_ops, w_up, conv_w, conv_b, w_down, *, seq, tm, chunk_tiles=6, sub=512):
    t_rows, d = x.shape
    dff = w_down.shape[0]
    step = chunk_tiles * TOEP
    chunks = tuple((c, min(c + step, dff)) for c in range(0, dff, step))
    tpb = seq // tm
    mrows = seq if mod.shape[0] > 1 else t_rows
    kinds = tuple(k for k, _, _ in mixer_ops)
    kern = functools.partial(_mix_ffn_kernel, tm=tm, tpb=tpb, chunks=chunks, kinds=kinds, sub=min(sub, tm))
    resident = lambda a: pl.BlockSpec(a.shape, lambda i: (0,) * a.ndim, pipeline_mode=pl.Buffered(1))
    ins, in_specs = [], []
    for kind, op, w in mixer_ops:
        c = op.shape[1]
        if kind == "rows":
            r, nblk = tm // ROW_HALO, t_rows // ROW_HALO
            in_specs += [pl.BlockSpec((tm, c), lambda i: (i, 0)),
                         pl.BlockSpec((ROW_HALO, c), lambda i, r=r: (jnp.maximum(i * r - 1, 0), 0)),
                         pl.BlockSpec((ROW_HALO, c), lambda i, r=r, nblk=nblk: (jnp.minimum((i + 1) * r, nblk - 1), 0))]
        else:
            r, nblk = tm // LANES, seq // LANES
            in_specs += [pl.BlockSpec((1, c, tm), lambda i: (i // tpb, 0, i % tpb)),
                         pl.BlockSpec((1, c, LANES), lambda i, r=r: (i // tpb, 0, jnp.maximum((i % tpb) * r - 1, 0))),
                         pl.BlockSpec((1, c, LANES),
                                      lambda i, r=r, nblk=nblk: (i // tpb, 0, jnp.minimum((i % tpb + 1) * r, nblk - 1)))]
        in_specs.append(resident(w))
        ins += [op, op, op, w]
    return pl.pallas_call(
        kern,
        out_shape=jax.ShapeDtypeStruct((t_rows, d), f32),
        grid=(t_rows // tm,),
        in_specs=[*_row_specs(tm, d, t_rows),
                  pl.BlockSpec((1, 6, d), lambda i: ((i * tm) // mrows, 0, 0)),
                  resident(g), *in_specs,
                  resident(w_up), resident(conv_w), resident(conv_b), resident(w_down)],
        out_specs=pl.BlockSpec((tm, d), lambda i: (i, 0)),
        compiler_params=_cparams("arbitrary"),
        name="mix_ffn",
    )(x, x, x, mod, g, *ins, w_up, conv_w, conv_b, w_down)


def _rope_tables(seq):
    half = ROPE // 2
    rows = seq // GRID_W
    row = np.repeat(np.arange(rows), GRID_W).astype(np.float32)
    col = np.tile(np.arange(GRID_W), rows).astype(np.float32)
    inv = (1.0 / (ROPE_BASE ** (np.arange(0, half, 2, dtype=np.float32) / half))).astype(np.float32)
    ar = (row[:, None] * inv[None]).astype(np.float64)
    ac = (col[:, None] * inv[None]).astype(np.float64)
    pad = np.zeros((seq, LANES - ROPE))
    cos64 = np.concatenate([np.cos(ar), np.cos(ar), np.cos(ac), np.cos(ac)], axis=1)
    sin64 = np.concatenate([-np.sin(ar), np.sin(ar), -np.sin(ac), np.sin(ac)], axis=1)
    as_f32 = lambda a: jnp.asarray(a.astype(np.float32))
    return (as_f32(np.concatenate([cos64, pad], axis=1)), as_f32(np.concatenate([sin64, pad], axis=1)),
            as_f32(cos64.T), as_f32(sin64.T))


def _pad_heads(a, width):
    lead = a.shape[:-1]
    a = a.reshape(*lead, HEADS, -1)
    a = jnp.pad(a, [(0, 0)] * len(lead) + [(0, 0), (0, width - a.shape[-1])])
    return a.reshape(*lead, HEADS * width)


def _mla_weights(j, mla_w_dq, mla_q_norm, mla_w_uq, mla_w_dkv, mla_kv_norm, mla_w_ukv, mla_q_head_norm,
                 mla_k_head_norm):
    ukv = mla_w_ukv[j].reshape(KV_RANK, HEADS, NOPE + V_DIM)
    kg = mla_k_head_norm[j]
    return {
        "w_dq": mla_w_dq[j].astype(bf16),
        "q_norm": mla_q_norm[j].reshape(1, -1),
        "w_uq": _pad_heads(mla_w_uq[j], HEAD_PAD).T.astype(bf16),
        "q_gain": jnp.pad(mla_q_head_norm[j], (0, HEAD_PAD - QK)),
        "w_dkv": jnp.pad(mla_w_dkv[j], ((0, 0), (0, LANES - ROPE))).astype(bf16),
        "kv_norm": mla_kv_norm[j].reshape(1, -1),
        "w_uk": ukv[:, :, :NOPE].reshape(KV_RANK, HEADS * NOPE).astype(bf16),
        "w_uv": ukv[:, :, NOPE:].reshape(KV_RANK, HEADS * V_DIM).T.astype(bf16),
        "k_gain": jnp.pad(kg, (0, 2 * NOPE - QK)).reshape(1, 2 * NOPE),
    }


def kernel(x_prompt, x_sample, cache_ckv, cache_krope, c, c_ctx, ada_w, ada_b, norm_g, mix_w_in, sgu_w, sgu_b, hy_conv_w, hy_conv_b, hy_f_w1, hy_f_b1, hy_f_w2, hy_f_b2, hy_f_w3, hy_f_freq, hy_decay, hy_d, mix_w_out, mla_w_dq, mla_q_norm, mla_w_uq, mla_w_dkv, mla_kv_norm, mla_w_ukv, mla_q_head_norm, mla_k_head_norm, mla_w_o, ffn_w_up, ffn_conv_w, ffn_conv_b, ffn_w_down):
    depth = ada_w.shape[0]
    d = x_prompt.shape[-1]
    nbp, seqp, _ = x_prompt.shape
    nbs, seqs, _ = x_sample.shape
    a_width = A_GROUPS * sgu_w.shape[-1]

    cond = jnp.zeros((16, d), f32).at[0].set(c_ctx).at[1:1 + nbs].set(c)
    mods = _ada(cond, ada_w, ada_b).reshape(depth, 16, 6, d)

    n_even = mix_w_in.shape[0]
    w_in = [mix_w_in[i].astype(bf16) for i in range(n_even)]
    sgu_wb = sgu_w.astype(bf16)
    sgu_bb = jnp.broadcast_to(sgu_b[..., None], sgu_b.shape + (sgu_w.shape[-1],))
    w_out_a = [mix_w_out[i, :a_width].astype(bf16) for i in range(n_even)]
    w_out_z = [mix_w_out[i, a_width:].astype(bf16) for i in range(n_even)]
    w_up = [ffn_w_up[l].astype(bf16) for l in range(depth)]
    w_down = [ffn_w_down[l].astype(bf16) for l in range(depth)]
    w_o = [mla_w_o[j].astype(bf16) for j in range(depth // 2)]
    mla_w = [_mla_weights(j, mla_w_dq, mla_q_norm, mla_w_uq, mla_w_dkv, mla_kv_norm, mla_w_ukv,
                          mla_q_head_norm, mla_k_head_norm) for j in range(depth // 2)]

    def trunk(x3, mod_all, cache):
        nb, seq, _ = x3.shape
        x = x3.reshape(nb * seq, d)
        tm = min(seq, 512)
        latent = cache is not None
        tabs = _rope_tables(seq) if latent else None
        new_ckv, new_kr = [], []
        for l in range(depth):
            mod = mod_all[l]
            g1 = norm_g[l, 0].reshape(1, d)
            g2 = norm_g[l, 1].reshape(1, d)
            if l % 2 == 0:
                i = l // 2
                a, xbt = _even_front(x, mod, g1, w_in[i], sgu_wb[i], sgu_bb[i], hy_conv_w[i],
                                     hy_conv_b[i].reshape(1, -1), nb=nb, seq=seq, tm=tm)
                filt = _hyena_filters_t(seq, hy_f_w1[i], hy_f_b1[i], hy_f_w2[i], hy_f_b2[i], hy_f_w3[i],
                                        hy_f_freq[i], hy_decay[i])
                zt = _hyena(xbt, filt, hy_d[i].reshape(-1), nb=nb, seq=seq)
                mixer_ops = [("rows", a, w_out_a[i]), ("cols", zt, w_out_z[i])]
            else:
                j = l // 2
                outs = _mla_front(x, mod, g1, mla_w[j], tabs, nb=nb, seq=seq, tm=tm, emit_cache=not latent)
                q, k, v = outs[:3]
                if latent:
                    past = cache[0].shape[2]
                    ckv_c = cache[0][:, j].reshape(nb * past, KV_RANK)
                    kr_c = jnp.pad(cache[1][:, j].reshape(nb * past, ROPE), ((0, 0), (0, LANES - ROPE)))
                    kc, vc = _cache_expand(ckv_c, kr_c, mla_w[j], nb=nb, seq=past)
                else:
                    kc = vc = None
                    new_ckv.append(outs[3].reshape(nb, seq, KV_RANK))
                    new_kr.append(outs[4].reshape(nb, seq, ROPE))
                if kc is None and seq <= TOEP:
                    at = _attention_short(q, k, v, nb=nb, seq=seq)
                else:
                    at = _attention(q, k, v, kc, vc, nb=nb, seq=seq, tq=min(seq, 1024))
                mixer_ops = [("cols", at, w_o[j])]
            x = _mix_ffn(x, mod, g2, mixer_ops, w_up[l], ffn_conv_w[l], ffn_conv_b[l].reshape(1, -1), w_down[l],
                         seq=seq, tm=tm)
        return x.reshape(nb, seq, d), new_ckv, new_kr

    y_prompt, ckv_list, kr_list = trunk(x_prompt, mods[:, 0:1], None)
    y_sample, _, _ = trunk(x_sample, mods[:, 1:1 + nbs], (cache_ckv, cache_krope))
    return (y_prompt, y_sample, jnp.stack(ckv_list, axis=1), jnp.stack(kr_list, axis=1))
```

```python
import functools
import math

import jax
import jax.numpy as jnp
import numpy as np
from jax import lax
from jax.experimental import pallas as pl
from jax.experimental.pallas import tpu as pltpu

f32 = jnp.float32
bf16 = jnp.bfloat16

EPS = 1e-6
GRID_W = 64
CHUNK = 128
A_GROUPS = 4
HEADS = 8
NOPE = 128
ROPE = 64
QK = NOPE + ROPE
HEAD_PAD = 256
V_DIM = 128
KV_RANK = 256
ROPE_BASE = 10000.0
FILTER_BANDS = 16
LANES = 128
HALO = 8
TOEP = 256
VMEM_LIMIT = 56 * 1024 * 1024
HIGHEST = lax.Precision.HIGHEST


def _cparams(*sem):
    return pltpu.CompilerParams(dimension_semantics=sem, vmem_limit_bytes=VMEM_LIMIT)


def _norm_mod(x, g, sc, sh):
    y = x * lax.rsqrt(jnp.mean(x * x, axis=-1, keepdims=True) + EPS)
    return (y * g) * (1.0 + sc) + sh


def _halo_ext(x, xp, xn, g, sc, sh, first, last):
    hp = jnp.where(first, 0.0, _norm_mod(xp, g, sc, sh))
    hn = jnp.where(last, 0.0, _norm_mod(xn, g, sc, sh))
    h = _norm_mod(x, g, sc, sh)
    return jnp.concatenate([hp, h, hn], axis=0).astype(bf16)


def _dwconv3(y, w, b, tm):
    return (y[HALO - 1:HALO - 1 + tm] * w[0:1] + y[HALO:HALO + tm] * w[1:2]
            + y[HALO + 1:HALO + 1 + tm] * w[2:3] + b)


def _gelu_tanh(x):
    return 0.5 * x * (1.0 + jnp.tanh(math.sqrt(2.0 / math.pi) * (x + 0.044715 * (x * x * x))))


def _row_specs(tm, d, t_rows):
    r = tm // HALO
    nb = t_rows // HALO
    return (pl.BlockSpec((tm, d), lambda i, *_: (i, 0)),
            pl.BlockSpec((HALO, d), lambda i, *_: (jnp.maximum(i * r - 1, 0), 0)),
            pl.BlockSpec((HALO, d), lambda i, *_: (jnp.minimum((i + 1) * r, nb - 1), 0)))


def _ada_kernel(c_ref, w_ref, b_ref, o_ref):
    c = c_ref[...]
    s = c * jax.nn.sigmoid(c)
    o_ref[0] = jnp.dot(s, w_ref[0], precision=HIGHEST, preferred_element_type=f32) + b_ref[0]


def _ada(cond, ada_w, ada_b):
    depth, d, n = ada_w.shape
    tn = n // 4
    rows = cond.shape[0]
    return pl.pallas_call(
        _ada_kernel,
        out_shape=jax.ShapeDtypeStruct((depth, rows, n), f32),
        grid=(depth, n // tn),
        in_specs=[pl.BlockSpec((rows, d), lambda l, j: (0, 0)),
                  pl.BlockSpec((1, d, tn), lambda l, j: (l, 0, j)),
                  pl.BlockSpec((1, 1, tn), lambda l, j: (l, 0, j))],
        out_specs=pl.BlockSpec((1, rows, tn), lambda l, j: (l, 0, j)),
        compiler_params=_cparams("arbitrary", "arbitrary"),
        name="ada_mod",
    )(cond, ada_w, ada_b.reshape(depth, 1, n))


def _even_front_kernel(x_ref, xp_ref, xn_ref, mod_ref, g_ref, win_ref, sw_ref, sb_ref, cw_ref, cb_ref,
                       a_ref, xbt_ref, *, tm, tpb, a_width):
    i = pl.program_id(0)
    mod = mod_ref[0]
    h = _halo_ext(x_ref[...], xp_ref[...], xn_ref[...], g_ref[...], mod[1:2], mod[0:1],
                  (i % tpb) == 0, (i % tpb) == tpb - 1)
    p = jnp.dot(h, win_ref[...], preferred_element_type=f32)
    u = _gelu_tanh(p[HALO:HALO + tm, 0:a_width])
    v = _gelu_tanh(p[HALO:HALO + tm, a_width:2 * a_width]).astype(bf16)
    nch = tm // CHUNK
    a_ch = a_width // A_GROUPS
    for gi in range(A_GROUPS):
        cols = slice(gi * a_ch, (gi + 1) * a_ch)
        rhs = jnp.concatenate([v[n * CHUNK:(n + 1) * CHUNK, cols] for n in range(nch)], axis=1)
        s = jnp.dot(sw_ref[gi], rhs, preferred_element_type=f32)
        for n in range(nch):
            rows = slice(n * CHUNK, (n + 1) * CHUNK)
            sn = s[:, n * a_ch:(n + 1) * a_ch] + sb_ref[gi]
            a_ref[rows, cols] = (u[rows, cols] * sn).astype(bf16)
    xb = _dwconv3(p[:, 2 * a_width:], cw_ref[...], cb_ref[...], tm)
    xbt_ref[0] = xb.T


def _even_front(x, mod, g, w_in, sgu_w, sgu_b, conv_w, conv_b, *, nb, seq, tm):
    t_rows, d = x.shape
    tpb = seq // tm
    n_in = w_in.shape[1]
    a_width = A_GROUPS * sgu_w.shape[1]
    nxb = n_in - 2 * a_width
    mrows = seq if mod.shape[0] > 1 else t_rows
    kern = functools.partial(_even_front_kernel, tm=tm, tpb=tpb, a_width=a_width)
    full = lambda shape: pl.BlockSpec(shape, lambda i: (0,) * len(shape))
    return pl.pallas_call(
        kern,
        out_shape=(jax.ShapeDtypeStruct((t_rows, a_width), bf16),
                   jax.ShapeDtypeStruct((nb, nxb, seq), f32)),
        grid=(t_rows // tm,),
        in_specs=[*_row_specs(tm, d, t_rows),
                  pl.BlockSpec((1, 6, d), lambda i: ((i * tm) // mrows, 0, 0)),
                  full((1, d)), full(w_in.shape), full(sgu_w.shape), full(sgu_b.shape),
                  full(conv_w.shape), full(conv_b.shape)],
        out_specs=(pl.BlockSpec((tm, a_width), lambda i: (i, 0)),
                   pl.BlockSpec((1, nxb, tm), lambda i: (i // tpb, 0, i % tpb))),
        compiler_params=_cparams("arbitrary"),
        name="even_front",
    )(x, x, x, mod, g, w_in, sgu_w, sgu_b, conv_w, conv_b)


def _filter_kernel(w1t_ref, b1_ref, w2t_ref, b2_ref, fr_ref, w3t_ref, dec_ref, o_ref, hid_scr, *, seq):
    @pl.when(pl.program_id(0) == 0)
    def _():
        nfeat = w1t_ref.shape[1]
        t = lax.broadcasted_iota(jnp.int32, (nfeat, seq), 1).astype(f32)
        fi = lax.broadcasted_iota(jnp.int32, (nfeat, seq), 0)
        tn = t / seq
        band = jnp.where(fi <= FILTER_BANDS, fi, fi - FILTER_BANDS).astype(f32)
        ang = ((2.0 * math.pi) * tn) * band
        z = jnp.where(fi == 0, tn, jnp.where(fi <= FILTER_BANDS, jnp.sin(ang), jnp.cos(ang)))
        z = jnp.where(fi <= 2 * FILTER_BANDS, z, 0.0)
        fr = fr_ref[...]
        h1 = jnp.sin(fr * (jnp.dot(w1t_ref[...], z, precision=HIGHEST, preferred_element_type=f32) + b1_ref[...]))
        hid_scr[...] = jnp.sin(fr * (jnp.dot(w2t_ref[...], h1, precision=HIGHEST, preferred_element_type=f32)
                                     + b2_ref[...]))

    h = jnp.dot(w3t_ref[...], hid_scr[...], precision=HIGHEST, preferred_element_type=f32)
    tl = lax.broadcasted_iota(jnp.int32, (1, seq), 1).astype(f32)
    dist = jnp.abs(tl - (seq // 2)) / seq
    h = h * jnp.exp(-jnp.abs(dec_ref[...]) * dist)
    h = h / (jnp.sum(jnp.abs(h), axis=1, keepdims=True) + EPS)
    hb = h.astype(bf16).astype(f32)
    lane = lax.broadcasted_iota(jnp.int32, hb.shape, 1)
    rolled = pltpu.roll(hb, 1, 1)
    prev_bits = lax.bitcast_convert_type(rolled, jnp.int32)
    cur_bits = lax.shift_right_logical(lax.bitcast_convert_type(hb, jnp.int32), 16)
    o_ref[:, 0:seq] = cur_bits | jnp.where(lane == 0, 0, prev_bits)
    tail_lane = lax.broadcasted_iota(jnp.int32, (hb.shape[0], LANES), 1)
    wrapped = lax.bitcast_convert_type(rolled[:, 0:LANES], jnp.int32)
    o_ref[:, seq:seq + LANES] = jnp.where(tail_lane == 0, wrapped, 0)


def _hyena_filters_t(seq, w1, b1, w2, b2, w3, freq, decay):
    nfeat = 40
    hid = w1.shape[1]
    w1t = jnp.zeros((hid, nfeat), f32).at[:, :w1.shape[0]].set(w1.T)
    rows = w3.shape[1]
    rb = 256
    kern = functools.partial(_filter_kernel, seq=seq)
    full = lambda shape: pl.BlockSpec(shape, lambda i: (0,) * len(shape))
    return pl.pallas_call(
        kern,
        out_shape=jax.ShapeDtypeStruct((rows, seq + LANES), jnp.int32),
        grid=(rows // rb,),
        in_specs=[full((hid, nfeat)), full((hid, 1)), full((hid, hid)), full((hid, 1)), full((hid, 1)),
                  pl.BlockSpec((rb, hid), lambda i: (i, 0)), pl.BlockSpec((rb, 1), lambda i: (i, 0))],
        out_specs=pl.BlockSpec((rb, seq + LANES), lambda i: (i, 0)),
        scratch_shapes=[pltpu.VMEM((hid, seq), f32)],
        compiler_params=_cparams("arbitrary"),
        name="hyena_filter",
    )(w1t, b1.reshape(hid, 1), w2.T, b2.reshape(hid, 1), freq.reshape(hid, 1), w3.T, decay.reshape(rows, 1))


def _bank_steps(hrow, bank_ref, seq):
    nseg = seq // LANES
    hl = LANES // 2
    upper = lax.broadcasted_iota(jnp.int32, (hl, LANES), 1) >= 2 * lax.broadcasted_iota(jnp.int32, (hl, LANES), 0)
    zero = jnp.zeros((hl, LANES), jnp.int32)
    bank_ref[0] = zero
    bank_ref[nseg + 2] = zero
    prev = zero
    for k in range(nseg + 1):
        seg = jnp.broadcast_to(hrow[:, k * LANES:(k + 1) * LANES], (hl, LANES))
        cur = pltpu.roll(seg, 0, 1, stride=2, stride_axis=0)
        bank_ref[k + 1] = jnp.where(upper, cur, prev)
        prev = cur
        yield


def _conv_steps(u, bank_ref, upad_ref, out, *, nb, seq):
    nseg = seq // LANES
    nblk = seq // TOEP
    half = seq // 2
    upad_ref[:, 0:half] = jnp.zeros((nb, half), f32)
    upad_ref[:, half:half + seq] = u
    upad_ref[:, half + seq:2 * seq] = jnp.zeros((nb, half), f32)
    acc = None
    for kt in range(nblk + 1):
        k0 = nseg - 2 * kt
        lhs = jnp.concatenate([upad_ref[:, (i + kt) * TOEP:(i + kt + 1) * TOEP] for i in range(nblk)], axis=0)
        blk = lambda idx: pltpu.bitcast(bank_ref[idx], bf16)
        top = jnp.concatenate([blk(k0 + 1), blk(k0 + 2)], axis=1)
        bot = jnp.concatenate([blk(k0), blk(k0 + 1)], axis=1)
        w = jnp.concatenate([top, bot], axis=0)
        part = jnp.dot(lhs.astype(bf16), w, preferred_element_type=f32)
        acc = part if acc is None else acc + part
        yield
    out.append(jnp.concatenate([acc[i * nb:(i + 1) * nb] for i in range(nblk)], axis=1))


def _interleave(main, side, ratio):
    main, side = list(main), list(side)
    while main or side:
        main = [g for g in main if next(g, StopIteration) is not StopIteration]
        for _ in range(ratio if main else 1 << 30):
            side = [g for g in side if next(g, StopIteration) is not StopIteration]
            if not side:
                break


def _hyena_kernel(d_ref, vb_ref, x1_ref, x2_ref, h0_ref, h1_ref, z_ref,
                  u_scr, a_scr, b_scr, o_scr, bank00, bank01, bank10, bank11, upad0, upad1, *, nb, seq, cg, nchan):
    gidx = pl.program_id(0)
    bank_ref = ((bank00, bank01), (bank10, bank11))
    upad_ref = (upad0, upad1)
    u_scr[...] = jnp.swapaxes(vb_ref[...], 0, 1)
    a_scr[...] = jnp.swapaxes(x1_ref[...], 0, 1)
    b_scr[...] = jnp.swapaxes(x2_ref[...], 0, 1)

    def banks(c):
        gens = []
        if c < cg:
            gens.append(_bank_steps(h0_ref[c:c + 1, :], bank_ref[c % 2][0], seq))
        if 1 <= c <= cg:
            gens.append(_bank_steps(h1_ref[c - 1:c, :], bank_ref[c % 2][1], seq))
        return gens

    ratio = -(-(seq // LANES + 1) // (seq // TOEP + 1))
    _interleave([], banks(0), ratio)
    z_prev = None
    for c in range(cg + 1):
        convs, first, second = [], [], []
        if c < cg:
            u = u_scr[c]
            convs.append(_conv_steps(u, bank_ref[c % 2][0], upad_ref[0], first, nb=nb, seq=seq))
        if c >= 1:
            convs.append(_conv_steps(z_prev, bank_ref[c % 2][1], upad_ref[1], second, nb=nb, seq=seq))
        _interleave(convs, banks(c + 1), ratio)
        if c >= 1:
            o_scr[c - 1] = b_scr[c - 1] * (second[0] + z_prev * d_ref[nchan + gidx * cg + c - 1])
        if c < cg:
            z_prev = a_scr[c] * (first[0] + u * d_ref[gidx * cg + c])
    z_ref[...] = jnp.swapaxes(o_scr[...], 0, 1)


def _hyena(xbt, filt_t, d_flat, *, nb, seq, cg=8):
    nchan = xbt.shape[1] // 3
    ng = nchan // cg
    kern = functools.partial(_hyena_kernel, nb=nb, seq=seq, cg=cg, nchan=nchan)
    act = lambda off: pl.BlockSpec((nb, cg, seq), lambda i: (0, off + i, 0))
    wseq = seq + LANES
    return pl.pallas_call(
        kern,
        out_shape=jax.ShapeDtypeStruct((nb, nchan, seq), f32),
        grid=(ng,),
        in_specs=[pl.BlockSpec(memory_space=pltpu.SMEM),
                  act(0), act(ng), act(2 * ng),
                  pl.BlockSpec((cg, wseq), lambda i: (i, 0)),
                  pl.BlockSpec((cg, wseq), lambda i: (ng + i, 0))],
        out_specs=pl.BlockSpec((nb, cg, seq), lambda i: (0, i, 0)),
        scratch_shapes=[pltpu.VMEM((cg, nb, seq), f32), pltpu.VMEM((cg, nb, seq), f32),
                        pltpu.VMEM((cg, nb, seq), f32), pltpu.VMEM((cg, nb, seq), f32),
                        *[pltpu.VMEM((seq // LANES + 3, LANES // 2, LANES), jnp.int32) for _ in range(4)],
                        pltpu.VMEM((nb, 2 * seq), f32), pltpu.VMEM((nb, 2 * seq), f32)],
        compiler_params=_cparams("arbitrary"),
        name="hyena_conv",
    )(d_flat, xbt, xbt, xbt, filt_t, filt_t)


def _rope128(x, cos, sin):
    lane = lax.broadcasted_iota(jnp.int32, x.shape, 1)
    swapped = jnp.where((lane & 16) == 0, pltpu.roll(x, LANES - 16, 1), pltpu.roll(x, 16, 1))
    return x * cos + swapped * sin


def _expand_kv(ckv, kr_pad, wuk_ref, wuv_ref, kg_ref, cos, sin, k_ref, v_ref, rows=slice(None)):
    cb = ckv.astype(bf16)
    kn = jnp.dot(cb, wuk_ref[...], preferred_element_type=f32)
    v_ref[0, :, rows] = jnp.dot(wuv_ref[...], ckv.T.astype(bf16), preferred_element_type=f32).astype(bf16)
    kg = kg_ref[...]
    g_nope, g_rope = kg[:, 0:NOPE], kg[:, NOPE:2 * NOPE]
    kr_ss = jnp.sum(kr_pad * kr_pad, axis=-1, keepdims=True)
    krg = kr_pad * g_rope
    if cos is not None:
        krg = _rope128(krg, cos, sin)
    for hd in range(HEADS):
        kh = kn[:, hd * NOPE:(hd + 1) * NOPE]
        r = lax.rsqrt((jnp.sum(kh * kh, axis=-1, keepdims=True) + kr_ss) / QK + EPS)
        k_ref[rows, hd * HEAD_PAD:hd * HEAD_PAD + NOPE] = (kh * r * g_nope).astype(bf16)
        k_ref[rows, hd * HEAD_PAD + NOPE:(hd + 1) * HEAD_PAD] = (krg * r).astype(bf16)


def _mla_front_kernel(*refs, rope, emit_cache, sub):
    (x_ref, mod_ref, g_ref, wdq_ref, qn_ref, wuq_ref, qg_ref, wdkv_ref, kvn_ref, wuk_ref, wuv_ref, kg_ref) = refs[:12]
    refs = refs[12:]
    if rope:
        cos_ref, sin_ref, qcos_ref, qsin_ref = refs[:4]
        refs = refs[4:]
    q_ref, k_ref, v_ref = refs[:3]
    mod = mod_ref[0]
    c_exp = (1.0 / math.sqrt(QK)) * math.log2(math.e)
    g16 = ROPE // 4
    tm = x_ref.shape[0]
    for r0 in range(0, tm, sub):
        rows = slice(r0, r0 + sub)
        cos = sin = None
        if rope:
            cos, sin, qcos, qsin = cos_ref[rows, :], sin_ref[rows, :], qcos_ref[:, rows], qsin_ref[:, rows]
        h = _norm_mod(x_ref[rows, :], g_ref[...], mod[1:2], mod[0:1]).astype(bf16)
        ql = jnp.dot(h, wdq_ref[...], preferred_element_type=f32)
        ql = ql * lax.rsqrt(jnp.mean(ql * ql, axis=-1, keepdims=True) + EPS) * qn_ref[...]
        qt = jnp.dot(wuq_ref[...], ql.T.astype(bf16), preferred_element_type=f32)
        qg = qg_ref[:, rows]
        for hd in range(HEADS):
            qh = qt[hd * HEAD_PAD:(hd + 1) * HEAD_PAD]
            ssq = jnp.sum(qh * qh, axis=0, keepdims=True)
            qh = qh * (lax.rsqrt(ssq / QK + EPS) * c_exp) * qg
            if rope:
                x = qh[NOPE:QK]
                swapped = jnp.concatenate([x[g16:2 * g16], x[0:g16], x[3 * g16:], x[2 * g16:3 * g16]], axis=0)
                qh = jnp.concatenate([qh[0:NOPE], x * qcos + swapped * qsin, qh[QK:]], axis=0)
            q_ref[0, hd * HEAD_PAD:(hd + 1) * HEAD_PAD, rows] = qh.astype(bf16)
        dkv = jnp.dot(h, wdkv_ref[...], preferred_element_type=f32)
        c_raw = dkv[:, 0:KV_RANK]
        ckv = c_raw * lax.rsqrt(jnp.mean(c_raw * c_raw, axis=-1, keepdims=True) + EPS) * kvn_ref[...]
        kr_pad = dkv[:, KV_RANK:]
        if emit_cache:
            ckv_ref, kr_ref = refs[3:5]
            ckv_ref[rows, :] = ckv
            kr_ref[rows, :] = kr_pad[:, 0:ROPE]
        _expand_kv(ckv, kr_pad, wuk_ref, wuv_ref, kg_ref, cos, sin, k_ref, v_ref, rows)


def _mla_front(x, mod, g, w, tabs, *, nb, seq, tm, emit_cache, sub=256):
    t_rows, d = x.shape
    tpb = seq // tm
    rope = tabs is not None
    mrows = seq if mod.shape[0] > 1 else t_rows
    kern = functools.partial(_mla_front_kernel, rope=rope, emit_cache=emit_cache, sub=min(sub, tm))
    full = lambda a: pl.BlockSpec(a.shape, lambda i: (0,) * a.ndim)
    wnames = ("w_dq", "q_norm", "w_uq", "q_gain", "w_dkv", "kv_norm", "w_uk", "w_uv", "k_gain")
    w = dict(w, q_gain=jnp.broadcast_to(w["q_gain"].reshape(HEAD_PAD, 1), (HEAD_PAD, tm)))
    ins = [x, mod, g] + [w[n] for n in wnames]
    in_specs = [pl.BlockSpec((tm, d), lambda i: (i, 0)),
                pl.BlockSpec((1, 6, d), lambda i: ((i * tm) // mrows, 0, 0)),
                full(g)] + [full(w[n]) for n in wnames]
    if rope:
        ins += list(tabs)
        in_specs += [pl.BlockSpec((tm, LANES), lambda i: (i % tpb, 0))] * 2
        in_specs += [pl.BlockSpec((ROPE, tm), lambda i: (0, i % tpb))] * 2
    hp = HEADS * HEAD_PAD
    out_shape = [jax.ShapeDtypeStruct((nb, hp, seq), bf16), jax.ShapeDtypeStruct((t_rows, hp), bf16),
                 jax.ShapeDtypeStruct((nb, HEADS * V_DIM, seq), bf16)]
    out_specs = [pl.BlockSpec((1, hp, tm), lambda i: (i // tpb, 0, i % tpb)), pl.BlockSpec((tm, hp), lambda i: (i, 0)),
                 pl.BlockSpec((1, HEADS * V_DIM, tm), lambda i: (i // tpb, 0, i % tpb))]
    if emit_cache:
        out_shape += [jax.ShapeDtypeStruct((t_rows, KV_RANK), f32), jax.ShapeDtypeStruct((t_rows, ROPE), f32)]
        out_specs += [pl.BlockSpec((tm, KV_RANK), lambda i: (i, 0)), pl.BlockSpec((tm, ROPE), lambda i: (i, 0))]
    return pl.pallas_call(
        kern, out_shape=tuple(out_shape), grid=(t_rows // tm,), in_specs=in_specs, out_specs=tuple(out_specs),
        compiler_params=_cparams("arbitrary"), name="mla_front",
    )(*ins)


def _cache_expand_kernel(ckv_ref, kr_ref, wuk_ref, wuv_ref, kg_ref, k_ref, v_ref):
    _expand_kv(ckv_ref[...], kr_ref[...], wuk_ref, wuv_ref, kg_ref, None, None, k_ref, v_ref)


def _cache_expand(ckv, kr_pad, w, *, nb, seq):
    t_rows = ckv.shape[0]
    tm = seq
    full = lambda a: pl.BlockSpec(a.shape, lambda i: (0,) * a.ndim)
    hp = HEADS * HEAD_PAD
    return pl.pallas_call(
        _cache_expand_kernel,
        out_shape=(jax.ShapeDtypeStruct((t_rows, hp), bf16), jax.ShapeDtypeStruct((nb, HEADS * V_DIM, seq), bf16)),
        grid=(t_rows // tm,),
        in_specs=[pl.BlockSpec((tm, KV_RANK), lambda i: (i, 0)), pl.BlockSpec((tm, LANES), lambda i: (i, 0)),
                  full(w["w_uk"]), full(w["w_uv"]), full(w["k_gain"])],
        out_specs=(pl.BlockSpec((tm, hp), lambda i: (i, 0)),
                   pl.BlockSpec((1, HEADS * V_DIM, tm), lambda i: (i, 0, 0))),
        compiler_params=_cparams("arbitrary"), name="mla_cache_expand",
    )(ckv, kr_pad, w["w_uk"], w["w_uv"], w["k_gain"])


def _attn_kernel(*refs, cached, tk):
    if cached:
        q_ref, k_ref, kc_ref, vt_ref, vct_ref, o_ref, s_scr, m_scr = refs
        parts = ((k_ref, vt_ref), (kc_ref, vct_ref))
    else:
        q_ref, k_ref, vt_ref, o_ref, s_scr, m_scr = refs
        parts = ((k_ref, vt_ref),)
    @pl.when(pl.program_id(0) == 0)
    def _():
        s_scr[...] = jnp.zeros_like(s_scr)
        m_scr[...] = jnp.zeros_like(m_scr)

    q = q_ref[0]
    m_old = m_scr[...]
    m_new = acc = None
    base = 0
    for kr, vtr in parts:
        ck = min(tk, kr.shape[0])
        ones = jnp.ones((16, ck), bf16)
        for j in range(kr.shape[0] // ck):
            rows = slice(base + j * ck, base + (j + 1) * ck)
            p = jnp.exp2(s_scr[rows, :] - m_old).astype(bf16)
            lhs = jnp.concatenate([vtr[0, :, j * ck:(j + 1) * ck], ones], axis=0)
            part = jnp.dot(lhs, p, preferred_element_type=f32)
            acc = part if acc is None else acc + part
            s = jnp.dot(kr[j * ck:(j + 1) * ck, :], q, preferred_element_type=f32)
            s_scr[rows, :] = s
            cm = jnp.max(s, axis=0, keepdims=True)
            m_new = cm if m_new is None else jnp.maximum(m_new, cm)
        base += kr.shape[0]
    m_scr[...] = m_new
    o_ref[0] = (acc[0:V_DIM] / acc[V_DIM:V_DIM + 1]).astype(bf16)


def _attention(q, k, vt, kc, vct, *, nb, seq, tq, tk=TOEP):
    cached = kc is not None
    nq = seq // tq
    n_tiles = nb * HEADS * nq
    past = kc.shape[0] // nb if cached else 0
    tk = min(seq, tk)

    def cur(n):
        n = jnp.minimum(n, n_tiles - 1)
        return n // (HEADS * nq), (n // nq) % HEADS, n % nq

    def prev(n):
        n = jnp.maximum(n - 1, 0)
        return n // (HEADS * nq), (n // nq) % HEADS, n % nq

    def spec(shape, fn):
        return pl.BlockSpec(shape, fn)

    kern = functools.partial(_attn_kernel, cached=cached, tk=tk)
    q_spec = spec((1, HEAD_PAD, tq), lambda n: cur(n))
    k_spec = lambda rows: spec((rows, HEAD_PAD), lambda n: (cur(n)[0], cur(n)[1]))
    v_spec = lambda cols: spec((1, V_DIM, cols), lambda n: (prev(n)[0], prev(n)[1], 0))
    if cached:
        ins = [q, k, kc, vt, vct]
        in_specs = [q_spec, k_spec(seq), k_spec(past), v_spec(seq), v_spec(past)]
    else:
        ins = [q, k, vt]
        in_specs = [q_spec, k_spec(seq), v_spec(seq)]
    return pl.pallas_call(
        kern,
        out_shape=jax.ShapeDtypeStruct((nb, HEADS * V_DIM, seq), bf16),
        grid=(n_tiles + 1,),
        in_specs=in_specs,
        out_specs=pl.BlockSpec((1, V_DIM, tq), lambda n: prev(n)),
        scratch_shapes=[pltpu.VMEM((seq + past, tq), f32), pltpu.VMEM((1, tq), f32)],
        compiler_params=_cparams("arbitrary"),
        name="mla_attention",
    )(*ins)


def _attn_short_kernel(q_ref, k_ref, vt_ref, o_ref):
    seq = k_ref.shape[0]
    ones = jnp.ones((16, seq), bf16)
    for hd in range(HEADS):
        cols = slice(hd * HEAD_PAD, (hd + 1) * HEAD_PAD)
        s = jnp.dot(k_ref[:, cols], q_ref[0, cols, :], preferred_element_type=f32)
        p = jnp.exp2(s - jnp.max(s, axis=0, keepdims=True)).astype(bf16)
        rows = slice(hd * V_DIM, (hd + 1) * V_DIM)
        acc = jnp.dot(jnp.concatenate([vt_ref[0, rows, :], ones], axis=0), p, preferred_element_type=f32)
        o_ref[0, rows, :] = (acc[0:V_DIM] / acc[V_DIM:V_DIM + 1]).astype(bf16)


def _attention_short(q, k, vt, *, nb, seq):
    return pl.pallas_call(
        _attn_short_kernel,
        out_shape=jax.ShapeDtypeStruct((nb, HEADS * V_DIM, seq), bf16),
        grid=(nb,),
        in_specs=[pl.BlockSpec((1, HEADS * HEAD_PAD, seq), lambda b: (b, 0, 0)),
                  pl.BlockSpec((seq, HEADS * HEAD_PAD), lambda b: (b, 0)),
                  pl.BlockSpec((1, HEADS * V_DIM, seq), lambda b: (b, 0, 0))],
        out_specs=pl.BlockSpec((1, HEADS * V_DIM, seq), lambda b: (b, 0, 0)),
        compiler_params=_cparams("arbitrary"),
        name="mla_attention_short",
    )(q, k, vt)


ROW_HALO = 16


def _mix_ffn_kernel(*refs, tm, tpb, chunks, kinds):
    x_ref, xp_ref, xn_ref, mod_ref, g_ref = refs[:5]
    wup_ref, cw_ref, cb_ref, wd_ref, o_ref = refs[5 + 4 * len(kinds):]
    i = pl.program_id(0)
    mod = mod_ref[0]
    tn = (((0,), (0,)), ((), ()))
    proj = [None, None, None]
    for n, kind in enumerate(kinds):
        main, prev, nxt, w_ref = refs[5 + 4 * n:9 + 4 * n]
        w = w_ref[...]
        if kind == "rows":
            parts = [jnp.dot(main[...], w, preferred_element_type=f32),
                     jnp.dot(prev[...], w, preferred_element_type=f32)[ROW_HALO - HALO:ROW_HALO],
                     jnp.dot(nxt[...], w, preferred_element_type=f32)[0:HALO]]
        else:
            ops = [main[0], prev[0][:, LANES - HALO:LANES], nxt[0][:, 0:HALO]]
            parts = [lax.dot_general(o.astype(bf16), w, tn, preferred_element_type=f32) for o in ops]
        proj = [p if q is None else q + p for q, p in zip(proj, parts)]
    g1 = mod[2:3]
    x1 = x_ref[...] + g1 * proj[0]
    h = _halo_ext(x1, xp_ref[...] + g1 * proj[1], xn_ref[...] + g1 * proj[2], g_ref[...], mod[4:5], mod[3:4],
                  (i % tpb) == 0, (i % tpb) == tpb - 1)
    dff = wd_ref.shape[1]
    acc = None
    for c0, c1 in chunks:
        halves = []
        for off in (c0, dff + c0):
            cols = slice(off, off + c1 - c0)
            y = jnp.dot(h, wup_ref[0, :, cols], preferred_element_type=f32)
            halves.append(_dwconv3(y, cw_ref[:, cols], cb_ref[:, cols], tm))
        gate, up = halves
        act = ((gate * jax.nn.sigmoid(gate)) * up).astype(bf16)
        part = jnp.dot(act, wd_ref[0, c0:c1, :], preferred_element_type=f32)
        acc = part if acc is None else acc + part
    o_ref[...] = x1 + mod[5:6] * acc


def _mix_ffn(x, mod, g, mixer_ops, w_up, conv_w, conv_b, w_down, layer, *, seq, tm):
    t_rows, d = x.shape
    dff = w_down.shape[1]
    step = 6 * TOEP
    chunks = tuple((c, min(c + step, dff)) for c in range(0, dff, step))
    tpb = seq // tm
    mrows = seq if mod.shape[0] > 1 else t_rows
    kinds = tuple(k for k, _, _ in mixer_ops)
    kern = functools.partial(_mix_ffn_kernel, tm=tm, tpb=tpb, chunks=chunks, kinds=kinds)
    resident = lambda a: pl.BlockSpec(a.shape, lambda i: (0,) * a.ndim, pipeline_mode=pl.Buffered(1))
    layer_block = lambda a: pl.BlockSpec((1,) + a.shape[1:], lambda i: (layer,) + (0,) * (a.ndim - 1),
                                         pipeline_mode=pl.Buffered(1))
    ins, in_specs = [], []
    for kind, op, w in mixer_ops:
        c = op.shape[1]
        if kind == "rows":
            r, nblk = tm // ROW_HALO, t_rows // ROW_HALO
            in_specs += [pl.BlockSpec((tm, c), lambda i: (i, 0)),
                         pl.BlockSpec((ROW_HALO, c), lambda i, r=r: (jnp.maximum(i * r - 1, 0), 0)),
                         pl.BlockSpec((ROW_HALO, c), lambda i, r=r, nblk=nblk: (jnp.minimum((i + 1) * r, nblk - 1), 0))]
        else:
            r, nblk = tm // LANES, seq // LANES
            in_specs += [pl.BlockSpec((1, c, tm), lambda i: (i // tpb, 0, i % tpb)),
                         pl.BlockSpec((1, c, LANES), lambda i, r=r: (i // tpb, 0, jnp.maximum((i % tpb) * r - 1, 0))),
                         pl.BlockSpec((1, c, LANES),
                                      lambda i, r=r, nblk=nblk: (i // tpb, 0, jnp.minimum((i % tpb + 1) * r, nblk - 1)))]
        in_specs.append(resident(w))
        ins += [op, op, op, w]
    return pl.pallas_call(
        kern,
        out_shape=jax.ShapeDtypeStruct((t_rows, d), f32),
        grid=(t_rows // tm,),
        in_specs=[*_row_specs(tm, d, t_rows),
                  pl.BlockSpec((1, 6, d), lambda i: ((i * tm) // mrows, 0, 0)),
                  resident(g), *in_specs,
                  layer_block(w_up), resident(conv_w), resident(conv_b), layer_block(w_down)],
        out_specs=pl.BlockSpec((tm, d), lambda i: (i, 0)),
        compiler_params=_cparams("arbitrary"),
        name="mix_ffn",
    )(x, x, x, mod, g, *ins, w_up, conv_w, conv_b, w_down)


def _rope_tables(seq):
    half = ROPE // 2
    rows = seq // GRID_W
    row = np.repeat(np.arange(rows), GRID_W).astype(np.float32)
    col = np.tile(np.arange(GRID_W), rows).astype(np.float32)
    inv = (1.0 / (ROPE_BASE ** (np.arange(0, half, 2, dtype=np.float32) / half))).astype(np.float32)
    ar = (row[:, None] * inv[None]).astype(np.float64)
    ac = (col[:, None] * inv[None]).astype(np.float64)
    pad = np.zeros((seq, LANES - ROPE))
    cos64 = np.concatenate([np.cos(ar), np.cos(ar), np.cos(ac), np.cos(ac)], axis=1)
    sin64 = np.concatenate([-np.sin(ar), np.sin(ar), -np.sin(ac), np.sin(ac)], axis=1)
    as_f32 = lambda a: jnp.asarray(a.astype(np.float32))
    return (as_f32(np.concatenate([cos64, pad], axis=1)), as_f32(np.concatenate([sin64, pad], axis=1)),
            as_f32(cos64.T), as_f32(sin64.T))


def _pad_heads(a, width):
    lead = a.shape[:-1]
    a = a.reshape(*lead, HEADS, -1)
    a = jnp.pad(a, [(0, 0)] * len(lead) + [(0, 0), (0, width - a.shape[-1])])
    return a.reshape(*lead, HEADS * width)


def _mla_weights(j, mla_w_dq, mla_q_norm, mla_w_uq, mla_w_dkv, mla_kv_norm, mla_w_ukv, mla_q_head_norm,
                 mla_k_head_norm):
    ukv = mla_w_ukv[j].reshape(KV_RANK, HEADS, NOPE + V_DIM)
    kg = mla_k_head_norm[j]
    return {
        "w_dq": mla_w_dq[j].astype(bf16),
        "q_norm": mla_q_norm[j].reshape(1, -1),
        "w_uq": _pad_heads(mla_w_uq[j], HEAD_PAD).T.astype(bf16),
        "q_gain": jnp.pad(mla_q_head_norm[j], (0, HEAD_PAD - QK)),
        "w_dkv": jnp.pad(mla_w_dkv[j], ((0, 0), (0, LANES - ROPE))).astype(bf16),
        "kv_norm": mla_kv_norm[j].reshape(1, -1),
        "w_uk": ukv[:, :, :NOPE].reshape(KV_RANK, HEADS * NOPE).astype(bf16),
        "w_uv": ukv[:, :, NOPE:].reshape(KV_RANK, HEADS * V_DIM).T.astype(bf16),
        "k_gain": jnp.pad(kg, (0, 2 * NOPE - QK)).reshape(1, 2 * NOPE),
    }


def kernel(x_prompt, x_sample, cache_ckv, cache_krope, c, c_ctx, ada_w, ada_b, norm_g, mix_w_in, sgu_w, sgu_b, hy_conv_w, hy_conv_b, hy_f_w1, hy_f_b1, hy_f_w2, hy_f_b2, hy_f_w3, hy_f_freq, hy_decay, hy_d, mix_w_out, mla_w_dq, mla_q_norm, mla_w_uq, mla_w_dkv, mla_kv_norm, mla_w_ukv, mla_q_head_norm, mla_k_head_norm, mla_w_o, ffn_w_up, ffn_conv_w, ffn_conv_b, ffn_w_down):
    depth = ada_w.shape[0]
    d = x_prompt.shape[-1]
    nbp, seqp, _ = x_prompt.shape
    nbs, seqs, _ = x_sample.shape
    a_width = A_GROUPS * sgu_w.shape[-1]

    cond = jnp.zeros((16, d), f32).at[0].set(c_ctx).at[1:1 + nbs].set(c)
    mods = _ada(cond, ada_w, ada_b).reshape(depth, 16, 6, d)

    n_even = mix_w_in.shape[0]
    w_in = [mix_w_in[i].astype(bf16) for i in range(n_even)]
    sgu_wb = sgu_w.astype(bf16)
    sgu_bb = jnp.broadcast_to(sgu_b[..., None], sgu_b.shape + (sgu_w.shape[-1],))
    w_out_a = [mix_w_out[i, :a_width].astype(bf16) for i in range(n_even)]
    w_out_z = [mix_w_out[i, a_width:].astype(bf16) for i in range(n_even)]
    w_up = ffn_w_up.astype(bf16)
    w_down = ffn_w_down.astype(bf16)
    w_o = [mla_w_o[j].astype(bf16) for j in range(depth // 2)]
    mla_w = [_mla_weights(j, mla_w_dq, mla_q_norm, mla_w_uq, mla_w_dkv, mla_kv_norm, mla_w_ukv,
                          mla_q_head_norm, mla_k_head_norm) for j in range(depth // 2)]

    def trunk(x3, mod_all, cache):
        nb, seq, _ = x3.shape
        x = x3.reshape(nb * seq, d)
        tm = min(seq, 512)
        latent = cache is not None
        tabs = _rope_tables(seq) if latent else None
        new_ckv, new_kr = [], []
        for l in range(depth):
            mod = mod_all[l]
            g1 = norm_g[l, 0].reshape(1, d)
            g2 = norm_g[l, 1].reshape(1, d)
            if l % 2 == 0:
                i = l // 2
                a, xbt = _even_front(x, mod, g1, w_in[i], sgu_wb[i], sgu_bb[i], hy_conv_w[i],
                                     hy_conv_b[i].reshape(1, -1), nb=nb, seq=seq, tm=tm)
                filt = _hyena_filters_t(seq, hy_f_w1[i], hy_f_b1[i], hy_f_w2[i], hy_f_b2[i], hy_f_w3[i],
                                        hy_f_freq[i], hy_decay[i])
                zt = _hyena(xbt, filt, hy_d[i].reshape(-1), nb=nb, seq=seq)
                mixer_ops = [("rows", a, w_out_a[i]), ("cols", zt, w_out_z[i])]
            else:
                j = l // 2
                outs = _mla_front(x, mod, g1, mla_w[j], tabs, nb=nb, seq=seq, tm=tm, emit_cache=not latent)
                q, k, v = outs[:3]
                if latent:
                    past = cache[0].shape[2]
                    ckv_c = cache[0][:, j].reshape(nb * past, KV_RANK)
                    kr_c = jnp.pad(cache[1][:, j].reshape(nb * past, ROPE), ((0, 0), (0, LANES - ROPE)))
                    kc, vc = _cache_expand(ckv_c, kr_c, mla_w[j], nb=nb, seq=past)
                else:
                    kc = vc = None
                    new_ckv.append(outs[3].reshape(nb, seq, KV_RANK))
                    new_kr.append(outs[4].reshape(nb, seq, ROPE))
                if kc is None and seq <= TOEP:
                    at = _attention_short(q, k, v, nb=nb, seq=seq)
                else:
                    at = _attention(q, k, v, kc, vc, nb=nb, seq=seq, tq=min(seq, 1024))
                mixer_ops = [("cols", at, w_o[j])]
            x = _mix_ffn(x, mod, g2, mixer_ops, w_up, ffn_conv_w[l], ffn_conv_b[l].reshape(1, -1), w_down, l,
                         seq=seq, tm=tm)
        return x.reshape(nb, seq, d), new_ckv, new_kr

    y_prompt, ckv_list, kr_list = trunk(x_prompt, mods[:, 0:1], None)
    y_sample, _, _ = trunk(x_sample, mods[:, 1:1 + nbs], (cache_ckv, cache_krope))
    return (y_prompt, y_sample, jnp.stack(ckv_list, axis=1), jnp.stack(kr_list, axis=1))
```

```python
import functools
import math

import jax
import jax.numpy as jnp
import numpy as np
from jax import lax
from jax.experimental import pallas as pl
from jax.experimental.pallas import tpu as pltpu

f32 = jnp.float32
bf16 = jnp.bfloat16

EPS = 1e-6
GRID_W = 64
CHUNK = 128
A_GROUPS = 4
HEADS = 8
NOPE = 128
ROPE = 64
QK = NOPE + ROPE
HEAD_PAD = 256
V_DIM = 128
KV_RANK = 256
ROPE_BASE = 10000.0
FILTER_BANDS = 16
LANES = 128
HALO = 8
TOEP = 256
VMEM_LIMIT = 56 * 1024 * 1024
HIGHEST = lax.Precision.HIGHEST


def _cparams(*sem):
    return pltpu.CompilerParams(dimension_semantics=sem, vmem_limit_bytes=VMEM_LIMIT)


def _norm_mod(x, g, sc, sh):
    y = x * lax.rsqrt(jnp.mean(x * x, axis=-1, keepdims=True) + EPS)
    return (y * g) * (1.0 + sc) + sh


def _halo_ext(x, xp, xn, g, sc, sh, first, last):
    hp = jnp.where(first, 0.0, _norm_mod(xp, g, sc, sh))
    hn = jnp.where(last, 0.0, _norm_mod(xn, g, sc, sh))
    h = _norm_mod(x, g, sc, sh)
    return jnp.concatenate([hp, h, hn], axis=0).astype(bf16)


def _dwconv3(y, w, b, tm):
    return (y[HALO - 1:HALO - 1 + tm] * w[0:1] + y[HALO:HALO + tm] * w[1:2]
            + y[HALO + 1:HALO + 1 + tm] * w[2:3] + b)


def _gelu_tanh(x):
    return 0.5 * x * (1.0 + jnp.tanh(math.sqrt(2.0 / math.pi) * (x + 0.044715 * (x * x * x))))


def _row_specs(tm, d, t_rows):
    r = tm // HALO
    nb = t_rows // HALO
    return (pl.BlockSpec((tm, d), lambda i, *_: (i, 0)),
            pl.BlockSpec((HALO, d), lambda i, *_: (jnp.maximum(i * r - 1, 0), 0)),
            pl.BlockSpec((HALO, d), lambda i, *_: (jnp.minimum((i + 1) * r, nb - 1), 0)))


def _ada_kernel(c_ref, w_ref, b_ref, o_ref):
    c = c_ref[...]
    s = c * jax.nn.sigmoid(c)
    o_ref[0] = jnp.dot(s, w_ref[0], precision=HIGHEST, preferred_element_type=f32) + b_ref[0]


def _ada(cond, ada_w, ada_b):
    depth, d, n = ada_w.shape
    tn = n // 4
    rows = cond.shape[0]
    return pl.pallas_call(
        _ada_kernel,
        out_shape=jax.ShapeDtypeStruct((depth, rows, n), f32),
        grid=(depth, n // tn),
        in_specs=[pl.BlockSpec((rows, d), lambda l, j: (0, 0)),
                  pl.BlockSpec((1, d, tn), lambda l, j: (l, 0, j)),
                  pl.BlockSpec((1, 1, tn), lambda l, j: (l, 0, j))],
        out_specs=pl.BlockSpec((1, rows, tn), lambda l, j: (l, 0, j)),
        compiler_params=_cparams("arbitrary", "arbitrary"),
        name="ada_mod",
    )(cond, ada_w, ada_b.reshape(depth, 1, n))


def _even_front_kernel(x_ref, xp_ref, xn_ref, mod_ref, g_ref, win_ref, sw_ref, sb_ref, cw_ref, cb_ref,
                       a_ref, xbt_ref, *, tm, tpb, a_width):
    i = pl.program_id(0)
    mod = mod_ref[0]
    h = _halo_ext(x_ref[...], xp_ref[...], xn_ref[...], g_ref[...], mod[1:2], mod[0:1],
                  (i % tpb) == 0, (i % tpb) == tpb - 1)
    p = jnp.dot(h, win_ref[...], preferred_element_type=f32)
    u = _gelu_tanh(p[HALO:HALO + tm, 0:a_width])
    v = _gelu_tanh(p[HALO:HALO + tm, a_width:2 * a_width]).astype(bf16)
    nch = tm // CHUNK
    a_ch = a_width // A_GROUPS
    for gi in range(A_GROUPS):
        cols = slice(gi * a_ch, (gi + 1) * a_ch)
        rhs = jnp.concatenate([v[n * CHUNK:(n + 1) * CHUNK, cols] for n in range(nch)], axis=1)
        s = jnp.dot(sw_ref[gi], rhs, preferred_element_type=f32)
        for n in range(nch):
            rows = slice(n * CHUNK, (n + 1) * CHUNK)
            sn = s[:, n * a_ch:(n + 1) * a_ch] + sb_ref[gi]
            a_ref[rows, cols] = (u[rows, cols] * sn).astype(bf16)
    xb = _dwconv3(p[:, 2 * a_width:], cw_ref[...], cb_ref[...], tm)
    xbt_ref[0] = xb.T


def _even_front(x, mod, g, w_in, sgu_w, sgu_b, conv_w, conv_b, *, nb, seq, tm):
    t_rows, d = x.shape
    tpb = seq // tm
    n_in = w_in.shape[1]
    a_width = A_GROUPS * sgu_w.shape[1]
    nxb = n_in - 2 * a_width
    mrows = seq if mod.shape[0] > 1 else t_rows
    kern = functools.partial(_even_front_kernel, tm=tm, tpb=tpb, a_width=a_width)
    full = lambda shape: pl.BlockSpec(shape, lambda i: (0,) * len(shape))
    return pl.pallas_call(
        kern,
        out_shape=(jax.ShapeDtypeStruct((t_rows, a_width), bf16),
                   jax.ShapeDtypeStruct((nb, nxb, seq), f32)),
        grid=(t_rows // tm,),
        in_specs=[*_row_specs(tm, d, t_rows),
                  pl.BlockSpec((1, 6, d), lambda i: ((i * tm) // mrows, 0, 0)),
                  full((1, d)), full(w_in.shape), full(sgu_w.shape), full(sgu_b.shape),
                  full(conv_w.shape), full(conv_b.shape)],
        out_specs=(pl.BlockSpec((tm, a_width), lambda i: (i, 0)),
                   pl.BlockSpec((1, nxb, tm), lambda i: (i // tpb, 0, i % tpb))),
        compiler_params=_cparams("arbitrary"),
        name="even_front",
    )(x, x, x, mod, g, w_in, sgu_w, sgu_b, conv_w, conv_b)


def _filter_kernel(w1t_ref, b1_ref, w2t_ref, b2_ref, fr_ref, w3t_ref, dec_ref, o_ref, hid_scr, *, seq):
    @pl.when(pl.program_id(0) == 0)
    def _():
        nfeat = w1t_ref.shape[1]
        t = lax.broadcasted_iota(jnp.int32, (nfeat, seq), 1).astype(f32)
        fi = lax.broadcasted_iota(jnp.int32, (nfeat, seq), 0)
        tn = t / seq
        band = jnp.where(fi <= FILTER_BANDS, fi, fi - FILTER_BANDS).astype(f32)
        ang = ((2.0 * math.pi) * tn) * band
        z = jnp.where(fi == 0, tn, jnp.where(fi <= FILTER_BANDS, jnp.sin(ang), jnp.cos(ang)))
        z = jnp.where(fi <= 2 * FILTER_BANDS, z, 0.0)
        fr = fr_ref[...]
        h1 = jnp.sin(fr * (jnp.dot(w1t_ref[...], z, precision=HIGHEST, preferred_element_type=f32) + b1_ref[...]))
        hid_scr[...] = jnp.sin(fr * (jnp.dot(w2t_ref[...], h1, precision=HIGHEST, preferred_element_type=f32)
                                     + b2_ref[...]))

    h = jnp.dot(w3t_ref[...], hid_scr[...], precision=HIGHEST, preferred_element_type=f32)
    tl = lax.broadcasted_iota(jnp.int32, (1, seq), 1).astype(f32)
    dist = jnp.abs(tl - (seq // 2)) / seq
    h = h * jnp.exp(-jnp.abs(dec_ref[...]) * dist)
    h = h / (jnp.sum(jnp.abs(h), axis=1, keepdims=True) + EPS)
    hb = h.astype(bf16).astype(f32)
    lane = lax.broadcasted_iota(jnp.int32, hb.shape, 1)
    rolled = pltpu.roll(hb, 1, 1)
    prev_bits = lax.bitcast_convert_type(rolled, jnp.int32)
    cur_bits = lax.shift_right_logical(lax.bitcast_convert_type(hb, jnp.int32), 16)
    o_ref[:, 0:seq] = cur_bits | jnp.where(lane == 0, 0, prev_bits)
    tail_lane = lax.broadcasted_iota(jnp.int32, (hb.shape[0], LANES), 1)
    wrapped = lax.bitcast_convert_type(rolled[:, 0:LANES], jnp.int32)
    o_ref[:, seq:seq + LANES] = jnp.where(tail_lane == 0, wrapped, 0)


def _hyena_filters_t(seq, w1, b1, w2, b2, w3, freq, decay):
    nfeat = 40
    hid = w1.shape[1]
    w1t = jnp.zeros((hid, nfeat), f32).at[:, :w1.shape[0]].set(w1.T)
    rows = w3.shape[1]
    rb = 256
    kern = functools.partial(_filter_kernel, seq=seq)
    full = lambda shape: pl.BlockSpec(shape, lambda i: (0,) * len(shape))
    return pl.pallas_call(
        kern,
        out_shape=jax.ShapeDtypeStruct((rows, seq + LANES), jnp.int32),
        grid=(rows // rb,),
        in_specs=[full((hid, nfeat)), full((hid, 1)), full((hid, hid)), full((hid, 1)), full((hid, 1)),
                  pl.BlockSpec((rb, hid), lambda i: (i, 0)), pl.BlockSpec((rb, 1), lambda i: (i, 0))],
        out_specs=pl.BlockSpec((rb, seq + LANES), lambda i: (i, 0)),
        scratch_shapes=[pltpu.VMEM((hid, seq), f32)],
        compiler_params=_cparams("arbitrary"),
        name="hyena_filter",
    )(w1t, b1.reshape(hid, 1), w2.T, b2.reshape(hid, 1), freq.reshape(hid, 1), w3.T, decay.reshape(rows, 1))


def _bank_steps(hrow, bank_ref, seq):
    nseg = seq // LANES
    hl = LANES // 2
    upper = lax.broadcasted_iota(jnp.int32, (hl, LANES), 1) >= 2 * lax.broadcasted_iota(jnp.int32, (hl, LANES), 0)
    zero = jnp.zeros((hl, LANES), jnp.int32)
    bank_ref[0] = zero
    bank_ref[nseg + 2] = zero
    prev = zero
    for k in range(nseg + 1):
        seg = jnp.broadcast_to(hrow[:, k * LANES:(k + 1) * LANES], (hl, LANES))
        cur = pltpu.roll(seg, 0, 1, stride=2, stride_axis=0)
        bank_ref[k + 1] = jnp.where(upper, cur, prev)
        prev = cur
        yield


def _conv_steps(u, bank_ref, upad_ref, out, *, nb, seq):
    nseg = seq // LANES
    nblk = seq // TOEP
    half = seq // 2
    upad_ref[:, 0:half] = jnp.zeros((nb, half), f32)
    upad_ref[:, half:half + seq] = u
    upad_ref[:, half + seq:2 * seq] = jnp.zeros((nb, half), f32)
    acc = None
    for kt in range(nblk + 1):
        k0 = nseg - 2 * kt
        lhs = jnp.concatenate([upad_ref[:, (i + kt) * TOEP:(i + kt + 1) * TOEP] for i in range(nblk)], axis=0)
        blk = lambda idx: pltpu.bitcast(bank_ref[idx], bf16)
        top = jnp.concatenate([blk(k0 + 1), blk(k0 + 2)], axis=1)
        bot = jnp.concatenate([blk(k0), blk(k0 + 1)], axis=1)
        w = jnp.concatenate([top, bot], axis=0)
        part = jnp.dot(lhs.astype(bf16), w, preferred_element_type=f32)
        acc = part if acc is None else acc + part
        yield
    out.append(jnp.concatenate([acc[i * nb:(i + 1) * nb] for i in range(nblk)], axis=1))


def _interleave(main, side, ratio):
    main, side = list(main), list(side)
    while main or side:
        main = [g for g in main if next(g, StopIteration) is not StopIteration]
        for _ in range(ratio if main else 1 << 30):
            side = [g for g in side if next(g, StopIteration) is not StopIteration]
            if not side:
                break


def _hyena_kernel(d_ref, vb_ref, x1_ref, x2_ref, h0_ref, h1_ref, z_ref,
                  u_scr, a_scr, b_scr, o_scr, bank00, bank01, bank10, bank11, upad0, upad1, *, nb, seq, cg, nchan):
    gidx = pl.program_id(0)
    bank_ref = ((bank00, bank01), (bank10, bank11))
    upad_ref = (upad0, upad1)
    u_scr[...] = jnp.swapaxes(vb_ref[...], 0, 1)
    a_scr[...] = jnp.swapaxes(x1_ref[...], 0, 1)
    b_scr[...] = jnp.swapaxes(x2_ref[...], 0, 1)

    def banks(c):
        gens = []
        if c < cg:
            gens.append(_bank_steps(h0_ref[c:c + 1, :], bank_ref[c % 2][0], seq))
        if 1 <= c <= cg:
            gens.append(_bank_steps(h1_ref[c - 1:c, :], bank_ref[c % 2][1], seq))
        return gens

    ratio = -(-(seq // LANES + 1) // (seq // TOEP + 1))
    _interleave([], banks(0), ratio)
    z_prev = None
    for c in range(cg + 1):
        convs, first, second = [], [], []
        if c < cg:
            u = u_scr[c]
            convs.append(_conv_steps(u, bank_ref[c % 2][0], upad_ref[0], first, nb=nb, seq=seq))
        if c >= 1:
            convs.append(_conv_steps(z_prev, bank_ref[c % 2][1], upad_ref[1], second, nb=nb, seq=seq))
        _interleave(convs, banks(c + 1), ratio)
        if c >= 1:
            o_scr[c - 1] = b_scr[c - 1] * (second[0] + z_prev * d_ref[nchan + gidx * cg + c - 1])
        if c < cg:
            z_prev = a_scr[c] * (first[0] + u * d_ref[gidx * cg + c])
    z_ref[...] = jnp.swapaxes(o_scr[...], 0, 1)


def _hyena(xbt, filt_t, d_flat, *, nb, seq):
    nchan = xbt.shape[1] // 3
    cg = HALO
    ng = nchan // cg
    kern = functools.partial(_hyena_kernel, nb=nb, seq=seq, cg=cg, nchan=nchan)
    act = lambda off: pl.BlockSpec((nb, cg, seq), lambda i: (0, off + i, 0))
    wseq = seq + LANES
    return pl.pallas_call(
        kern,
        out_shape=jax.ShapeDtypeStruct((nb, nchan, seq), f32),
        grid=(ng,),
        in_specs=[pl.BlockSpec(memory_space=pltpu.SMEM),
                  act(0), act(ng), act(2 * ng),
                  pl.BlockSpec((cg, wseq), lambda i: (i, 0)),
                  pl.BlockSpec((cg, wseq), lambda i: (ng + i, 0))],
        out_specs=pl.BlockSpec((nb, cg, seq), lambda i: (0, i, 0)),
        scratch_shapes=[pltpu.VMEM((cg, nb, seq), f32), pltpu.VMEM((cg, nb, seq), f32),
                        pltpu.VMEM((cg, nb, seq), f32), pltpu.VMEM((cg, nb, seq), f32),
                        *[pltpu.VMEM((seq // LANES + 3, LANES // 2, LANES), jnp.int32) for _ in range(4)],
                        pltpu.VMEM((nb, 2 * seq), f32), pltpu.VMEM((nb, 2 * seq), f32)],
        compiler_params=_cparams("arbitrary"),
        name="hyena_conv",
    )(d_flat, xbt, xbt, xbt, filt_t, filt_t)


def _rope128(x, cos, sin):
    lane = lax.broadcasted_iota(jnp.int32, x.shape, 1)
    swapped = jnp.where((lane & 16) == 0, pltpu.roll(x, LANES - 16, 1), pltpu.roll(x, 16, 1))
    return x * cos + swapped * sin


def _expand_kv(ckv, kr_pad, wuk_ref, wuv_ref, kg_ref, cos, sin, k_ref, v_ref, rows=slice(None)):
    cb = ckv.astype(bf16)
    kn = jnp.dot(cb, wuk_ref[...], preferred_element_type=f32)
    v_ref[0, :, rows] = jnp.dot(wuv_ref[...], ckv.T.astype(bf16), preferred_element_type=f32).astype(bf16)
    kg = kg_ref[...]
    g_nope, g_rope = kg[:, 0:NOPE], kg[:, NOPE:2 * NOPE]
    kr_ss = jnp.sum(kr_pad * kr_pad, axis=-1, keepdims=True)
    krg = kr_pad * g_rope
    if cos is not None:
        krg = _rope128(krg, cos, sin)
    for hd in range(HEADS):
        kh = kn[:, hd * NOPE:(hd + 1) * NOPE]
        r = lax.rsqrt((jnp.sum(kh * kh, axis=-1, keepdims=True) + kr_ss) / QK + EPS)
        k_ref[rows, hd * HEAD_PAD:hd * HEAD_PAD + NOPE] = (kh * r * g_nope).astype(bf16)
        k_ref[rows, hd * HEAD_PAD + NOPE:(hd + 1) * HEAD_PAD] = (krg * r).astype(bf16)


def _mla_front_kernel(*refs, rope, emit_cache, sub):
    (x_ref, mod_ref, g_ref, wdq_ref, qn_ref, wuq_ref, qg_ref, wdkv_ref, kvn_ref, wuk_ref, wuv_ref, kg_ref) = refs[:12]
    refs = refs[12:]
    if rope:
        cos_ref, sin_ref, qcos_ref, qsin_ref = refs[:4]
        refs = refs[4:]
    q_ref, k_ref, v_ref = refs[:3]
    mod = mod_ref[0]
    c_exp = (1.0 / math.sqrt(QK)) * math.log2(math.e)
    g16 = ROPE // 4
    tm = x_ref.shape[0]
    for r0 in range(0, tm, sub):
        rows = slice(r0, r0 + sub)
        cos = sin = None
        if rope:
            cos, sin, qcos, qsin = cos_ref[rows, :], sin_ref[rows, :], qcos_ref[:, rows], qsin_ref[:, rows]
        h = _norm_mod(x_ref[rows, :], g_ref[...], mod[1:2], mod[0:1]).astype(bf16)
        ql = jnp.dot(h, wdq_ref[...], preferred_element_type=f32)
        ql = ql * lax.rsqrt(jnp.mean(ql * ql, axis=-1, keepdims=True) + EPS) * qn_ref[...]
        qt = jnp.dot(wuq_ref[...], ql.T.astype(bf16), preferred_element_type=f32)
        qg = qg_ref[:, rows]
        for hd in range(HEADS):
            qh = qt[hd * HEAD_PAD:(hd + 1) * HEAD_PAD]
            ssq = jnp.sum(qh * qh, axis=0, keepdims=True)
            qh = qh * (lax.rsqrt(ssq / QK + EPS) * c_exp) * qg
            if rope:
                x = qh[NOPE:QK]
                swapped = jnp.concatenate([x[g16:2 * g16], x[0:g16], x[3 * g16:], x[2 * g16:3 * g16]], axis=0)
                qh = jnp.concatenate([qh[0:NOPE], x * qcos + swapped * qsin, qh[QK:]], axis=0)
            q_ref[0, hd * HEAD_PAD:(hd + 1) * HEAD_PAD, rows] = qh.astype(bf16)
        dkv = jnp.dot(h, wdkv_ref[...], preferred_element_type=f32)
        c_raw = dkv[:, 0:KV_RANK]
        ckv = c_raw * lax.rsqrt(jnp.mean(c_raw * c_raw, axis=-1, keepdims=True) + EPS) * kvn_ref[...]
        kr_pad = dkv[:, KV_RANK:]
        if emit_cache:
            ckv_ref, kr_ref = refs[3:5]
            ckv_ref[rows, :] = ckv
            kr_ref[rows, :] = kr_pad[:, 0:ROPE]
        _expand_kv(ckv, kr_pad, wuk_ref, wuv_ref, kg_ref, cos, sin, k_ref, v_ref, rows)


def _mla_front(x, mod, g, w, tabs, *, nb, seq, tm, emit_cache):
    t_rows, d = x.shape
    tpb = seq // tm
    rope = tabs is not None
    mrows = seq if mod.shape[0] > 1 else t_rows
    kern = functools.partial(_mla_front_kernel, rope=rope, emit_cache=emit_cache, sub=min(TOEP, tm))
    full = lambda a: pl.BlockSpec(a.shape, lambda i: (0,) * a.ndim)
    wnames = ("w_dq", "q_norm", "w_uq", "q_gain", "w_dkv", "kv_norm", "w_uk", "w_uv", "k_gain")
    w = dict(w, q_gain=jnp.broadcast_to(w["q_gain"].reshape(HEAD_PAD, 1), (HEAD_PAD, tm)))
    ins = [x, mod, g] + [w[n] for n in wnames]
    in_specs = [pl.BlockSpec((tm, d), lambda i: (i, 0)),
                pl.BlockSpec((1, 6, d), lambda i: ((i * tm) // mrows, 0, 0)),
                full(g)] + [full(w[n]) for n in wnames]
    if rope:
        ins += list(tabs)
        in_specs += [pl.BlockSpec((tm, LANES), lambda i: (i % tpb, 0))] * 2
        in_specs += [pl.BlockSpec((ROPE, tm), lambda i: (0, i % tpb))] * 2
    hp = HEADS * HEAD_PAD
    out_shape = [jax.ShapeDtypeStruct((nb, hp, seq), bf16), jax.ShapeDtypeStruct((t_rows, hp), bf16),
                 jax.ShapeDtypeStruct((nb, HEADS * V_DIM, seq), bf16)]
    out_specs = [pl.BlockSpec((1, hp, tm), lambda i: (i // tpb, 0, i % tpb)), pl.BlockSpec((tm, hp), lambda i: (i, 0)),
                 pl.BlockSpec((1, HEADS * V_DIM, tm), lambda i: (i // tpb, 0, i % tpb))]
    if emit_cache:
        out_shape += [jax.ShapeDtypeStruct((t_rows, KV_RANK), f32), jax.ShapeDtypeStruct((t_rows, ROPE), f32)]
        out_specs += [pl.BlockSpec((tm, KV_RANK), lambda i: (i, 0)), pl.BlockSpec((tm, ROPE), lambda i: (i, 0))]
    return pl.pallas_call(
        kern, out_shape=tuple(out_shape), grid=(t_rows // tm,), in_specs=in_specs, out_specs=tuple(out_specs),
        compiler_params=_cparams("arbitrary"), name="mla_front",
    )(*ins)


def _cache_expand_kernel(ckv_ref, kr_ref, wuk_ref, wuv_ref, kg_ref, k_ref, v_ref):
    _expand_kv(ckv_ref[...], kr_ref[...], wuk_ref, wuv_ref, kg_ref, None, None, k_ref, v_ref)


def _cache_expand(ckv, kr_pad, w, *, nb, seq):
    t_rows = ckv.shape[0]
    tm = seq
    full = lambda a: pl.BlockSpec(a.shape, lambda i: (0,) * a.ndim)
    hp = HEADS * HEAD_PAD
    return pl.pallas_call(
        _cache_expand_kernel,
        out_shape=(jax.ShapeDtypeStruct((t_rows, hp), bf16), jax.ShapeDtypeStruct((nb, HEADS * V_DIM, seq), bf16)),
        grid=(t_rows // tm,),
        in_specs=[pl.BlockSpec((tm, KV_RANK), lambda i: (i, 0)), pl.BlockSpec((tm, LANES), lambda i: (i, 0)),
                  full(w["w_uk"]), full(w["w_uv"]), full(w["k_gain"])],
        out_specs=(pl.BlockSpec((tm, hp), lambda i: (i, 0)),
                   pl.BlockSpec((1, HEADS * V_DIM, tm), lambda i: (i, 0, 0))),
        compiler_params=_cparams("arbitrary"), name="mla_cache_expand",
    )(ckv, kr_pad, w["w_uk"], w["w_uv"], w["k_gain"])


def _attn_kernel(*refs, cached, tk):
    if cached:
        q_ref, k_ref, kc_ref, vt_ref, vct_ref, o_ref, s_scr, m_scr = refs
        parts = ((k_ref, vt_ref), (kc_ref, vct_ref))
    else:
        q_ref, k_ref, vt_ref, o_ref, s_scr, m_scr = refs
        parts = ((k_ref, vt_ref),)

    @pl.when(pl.program_id(0) == 0)
    def _():
        s_scr[...] = jnp.zeros_like(s_scr)
        m_scr[...] = jnp.zeros_like(m_scr)

    q = q_ref[0]
    m_old = m_scr[...]
    m_new = acc = None
    base = 0
    for kr, vtr in parts:
        ck = min(tk, kr.shape[0])
        ones = jnp.ones((16, ck), bf16)
        for j in range(kr.shape[0] // ck):
            rows = slice(base + j * ck, base + (j + 1) * ck)
            p = jnp.exp2(s_scr[rows, :] - m_old).astype(bf16)
            lhs = jnp.concatenate([vtr[0, :, j * ck:(j + 1) * ck], ones], axis=0)
            part = jnp.dot(lhs, p, preferred_element_type=f32)
            acc = part if acc is None else acc + part
            s = jnp.dot(kr[j * ck:(j + 1) * ck, :], q, preferred_element_type=f32)
            s_scr[rows, :] = s
            cm = jnp.max(s, axis=0, keepdims=True)
            m_new = cm if m_new is None else jnp.maximum(m_new, cm)
        base += kr.shape[0]
    m_scr[...] = m_new
    o_ref[0] = (acc[0:V_DIM] / acc[V_DIM:V_DIM + 1]).astype(bf16)


def _attention(q, k, vt, kc, vct, *, nb, seq, tq):
    cached = kc is not None
    nq = seq // tq
    n_tiles = nb * HEADS * nq
    past = kc.shape[0] // nb if cached else 0
    tk = min(seq, TOEP)

    def cur(n):
        n = jnp.minimum(n, n_tiles - 1)
        return n // (HEADS * nq), (n // nq) % HEADS, n % nq

    def prev(n):
        n = jnp.maximum(n - 1, 0)
        return n // (HEADS * nq), (n // nq) % HEADS, n % nq

    def spec(shape, fn):
        return pl.BlockSpec(shape, fn)

    kern = functools.partial(_attn_kernel, cached=cached, tk=tk)
    q_spec = spec((1, HEAD_PAD, tq), lambda n: cur(n))
    k_spec = lambda rows: spec((rows, HEAD_PAD), lambda n: (cur(n)[0], cur(n)[1]))
    v_spec = lambda cols: spec((1, V_DIM, cols), lambda n: (prev(n)[0], prev(n)[1], 0))
    if cached:
        ins = [q, k, kc, vt, vct]
        in_specs = [q_spec, k_spec(seq), k_spec(past), v_spec(seq), v_spec(past)]
    else:
        ins = [q, k, vt]
        in_specs = [q_spec, k_spec(seq), v_spec(seq)]
    return pl.pallas_call(
        kern,
        out_shape=jax.ShapeDtypeStruct((nb, HEADS * V_DIM, seq), bf16),
        grid=(n_tiles + 1,),
        in_specs=in_specs,
        out_specs=pl.BlockSpec((1, V_DIM, tq), lambda n: prev(n)),
        scratch_shapes=[pltpu.VMEM((seq + past, tq), f32), pltpu.VMEM((1, tq), f32)],
        compiler_params=_cparams("arbitrary"),
        name="mla_attention",
    )(*ins)


def _attn_short_kernel(q_ref, k_ref, vt_ref, o_ref):
    seq = k_ref.shape[0]
    ones = jnp.ones((16, seq), bf16)
    for hd in range(HEADS):
        cols = slice(hd * HEAD_PAD, (hd + 1) * HEAD_PAD)
        s = jnp.dot(k_ref[:, cols], q_ref[0, cols, :], preferred_element_type=f32)
        p = jnp.exp2(s - jnp.max(s, axis=0, keepdims=True)).astype(bf16)
        rows = slice(hd * V_DIM, (hd + 1) * V_DIM)
        acc = jnp.dot(jnp.concatenate([vt_ref[0, rows, :], ones], axis=0), p, preferred_element_type=f32)
        o_ref[0, rows, :] = (acc[0:V_DIM] / acc[V_DIM:V_DIM + 1]).astype(bf16)


def _attention_short(q, k, vt, *, nb, seq):
    return pl.pallas_call(
        _attn_short_kernel,
        out_shape=jax.ShapeDtypeStruct((nb, HEADS * V_DIM, seq), bf16),
        grid=(nb,),
        in_specs=[pl.BlockSpec((1, HEADS * HEAD_PAD, seq), lambda b: (b, 0, 0)),
                  pl.BlockSpec((seq, HEADS * HEAD_PAD), lambda b: (b, 0)),
                  pl.BlockSpec((1, HEADS * V_DIM, seq), lambda b: (b, 0, 0))],
        out_specs=pl.BlockSpec((1, HEADS * V_DIM, seq), lambda b: (b, 0, 0)),
        compiler_params=_cparams("arbitrary"),
        name="mla_attention_short",
    )(q, k, vt)


ROW_HALO = 16


def _mix_ffn_kernel(*refs, tm, tpb, chunks, kinds):
    x_ref, xp_ref, xn_ref, mod_ref, g_ref = refs[:5]
    wup_ref, cw_ref, cb_ref, wd_ref, o_ref = refs[5 + 4 * len(kinds):]
    i = pl.program_id(0)
    mod = mod_ref[0]
    tn = (((0,), (0,)), ((), ()))
    proj = [None, None, None]
    for n, kind in enumerate(kinds):
        main, prev, nxt, w_ref = refs[5 + 4 * n:9 + 4 * n]
        w = w_ref[...]
        if kind == "rows":
            parts = [jnp.dot(main[...], w, preferred_element_type=f32),
                     jnp.dot(prev[...], w, preferred_element_type=f32)[ROW_HALO - HALO:ROW_HALO],
                     jnp.dot(nxt[...], w, preferred_element_type=f32)[0:HALO]]
        else:
            ops = [main[0], prev[0][:, LANES - HALO:LANES], nxt[0][:, 0:HALO]]
            parts = [lax.dot_general(o.astype(bf16), w, tn, preferred_element_type=f32) for o in ops]
        proj = [p if q is None else q + p for q, p in zip(proj, parts)]
    g1 = mod[2:3]
    x1 = x_ref[...] + g1 * proj[0]
    h = _halo_ext(x1, xp_ref[...] + g1 * proj[1], xn_ref[...] + g1 * proj[2], g_ref[...], mod[4:5], mod[3:4],
                  (i % tpb) == 0, (i % tpb) == tpb - 1)
    dff = wd_ref.shape[1]
    acc = None
    for c0, c1 in chunks:
        halves = []
        for off in (c0, dff + c0):
            cols = slice(off, off + c1 - c0)
            y = jnp.dot(h, wup_ref[0, :, cols], preferred_element_type=f32)
            halves.append(_dwconv3(y, cw_ref[:, cols], cb_ref[:, cols], tm))
        gate, up = halves
        act = ((gate * jax.nn.sigmoid(gate)) * up).astype(bf16)
        part = jnp.dot(act, wd_ref[0, c0:c1, :], preferred_element_type=f32)
        acc = part if acc is None else acc + part
    o_ref[...] = x1 + mod[5:6] * acc


def _mix_ffn(x, mod, g, mixer_ops, w_up, conv_w, conv_b, w_down, layer, *, seq, tm):
    t_rows, d = x.shape
    dff = w_down.shape[1]
    step = 6 * TOEP
    chunks = tuple((c, min(c + step, dff)) for c in range(0, dff, step))
    tpb = seq // tm
    mrows = seq if mod.shape[0] > 1 else t_rows
    kinds = tuple(k for k, _, _ in mixer_ops)
    kern = functools.partial(_mix_ffn_kernel, tm=tm, tpb=tpb, chunks=chunks, kinds=kinds)
    resident = lambda a: pl.BlockSpec(a.shape, lambda i: (0,) * a.ndim, pipeline_mode=pl.Buffered(1))
    layer_block = lambda a: pl.BlockSpec((1,) + a.shape[1:], lambda i: (layer,) + (0,) * (a.ndim - 1),
                                         pipeline_mode=pl.Buffered(1))
    ins, in_specs = [], []
    for kind, op, w in mixer_ops:
        c = op.shape[1]
        if kind == "rows":
            r, nblk = tm // ROW_HALO, t_rows // ROW_HALO
            in_specs += [pl.BlockSpec((tm, c), lambda i: (i, 0)),
                         pl.BlockSpec((ROW_HALO, c), lambda i, r=r: (jnp.maximum(i * r - 1, 0), 0)),
                         pl.BlockSpec((ROW_HALO, c), lambda i, r=r, nblk=nblk: (jnp.minimum((i + 1) * r, nblk - 1), 0))]
        else:
            r, nblk = tm // LANES, seq // LANES
            in_specs += [pl.BlockSpec((1, c, tm), lambda i: (i // tpb, 0, i % tpb)),
                         pl.BlockSpec((1, c, LANES), lambda i, r=r: (i // tpb, 0, jnp.maximum((i % tpb) * r - 1, 0))),
                         pl.BlockSpec((1, c, LANES),
                                      lambda i, r=r, nblk=nblk: (i // tpb, 0, jnp.minimum((i % tpb + 1) * r, nblk - 1)))]
        in_specs.append(resident(w))
        ins += [op, op, op, w]
    return pl.pallas_call(
        kern,
        out_shape=jax.ShapeDtypeStruct((t_rows, d), f32),
        grid=(t_rows // tm,),
        in_specs=[*_row_specs(tm, d, t_rows),
                  pl.BlockSpec((1, 6, d), lambda i: ((i * tm) // mrows, 0, 0)),
                  resident(g), *in_specs,
                  layer_block(w_up), resident(conv_w), resident(conv_b), layer_block(w_down)],
        out_specs=pl.BlockSpec((tm, d), lambda i: (i, 0)),
        compiler_params=_cparams("arbitrary"),
        name="mix_ffn",
    )(x, x, x, mod, g, *ins, w_up, conv_w, conv_b, w_down)


def _rope_tables(seq):
    half = ROPE // 2
    rows = seq // GRID_W
    row = np.repeat(np.arange(rows), GRID_W).astype(np.float32)
    col = np.tile(np.arange(GRID_W), rows).astype(np.float32)
    inv = (1.0 / (ROPE_BASE ** (np.arange(0, half, 2, dtype=np.float32) / half))).astype(np.float32)
    ar = (row[:, None] * inv[None]).astype(np.float64)
    ac = (col[:, None] * inv[None]).astype(np.float64)
    pad = np.zeros((seq, LANES - ROPE))
    cos64 = np.concatenate([np.cos(ar), np.cos(ar), np.cos(ac), np.cos(ac)], axis=1)
    sin64 = np.concatenate([-np.sin(ar), np.sin(ar), -np.sin(ac), np.sin(ac)], axis=1)
    as_f32 = lambda a: jnp.asarray(a.astype(np.float32))
    return (as_f32(np.concatenate([cos64, pad], axis=1)), as_f32(np.concatenate([sin64, pad], axis=1)),
            as_f32(cos64.T), as_f32(sin64.T))


def _pad_heads(a, width):
    lead = a.shape[:-1]
    a = a.reshape(*lead, HEADS, -1)
    a = jnp.pad(a, [(0, 0)] * len(lead) + [(0, 0), (0, width - a.shape[-1])])
    return a.reshape(*lead, HEADS * width)


def _mla_weights(j, mla_w_dq, mla_q_norm, mla_w_uq, mla_w_dkv, mla_kv_norm, mla_w_ukv, mla_q_head_norm,
                 mla_k_head_norm):
    ukv = mla_w_ukv[j].reshape(KV_RANK, HEADS, NOPE + V_DIM)
    kg = mla_k_head_norm[j]
    return {
        "w_dq": mla_w_dq[j].astype(bf16),
        "q_norm": mla_q_norm[j].reshape(1, -1),
        "w_uq": _pad_heads(mla_w_uq[j], HEAD_PAD).T.astype(bf16),
        "q_gain": jnp.pad(mla_q_head_norm[j], (0, HEAD_PAD - QK)),
        "w_dkv": jnp.pad(mla_w_dkv[j], ((0, 0), (0, LANES - ROPE))).astype(bf16),
        "kv_norm": mla_kv_norm[j].reshape(1, -1),
        "w_uk": ukv[:, :, :NOPE].reshape(KV_RANK, HEADS * NOPE).astype(bf16),
        "w_uv": ukv[:, :, NOPE:].reshape(KV_RANK, HEADS * V_DIM).T.astype(bf16),
        "k_gain": jnp.pad(kg, (0, 2 * NOPE - QK)).reshape(1, 2 * NOPE),
    }


def kernel(x_prompt, x_sample, cache_ckv, cache_krope, c, c_ctx, ada_w, ada_b, norm_g, mix_w_in, sgu_w, sgu_b, hy_conv_w, hy_conv_b, hy_f_w1, hy_f_b1, hy_f_w2, hy_f_b2, hy_f_w3, hy_f_freq, hy_decay, hy_d, mix_w_out, mla_w_dq, mla_q_norm, mla_w_uq, mla_w_dkv, mla_kv_norm, mla_w_ukv, mla_q_head_norm, mla_k_head_norm, mla_w_o, ffn_w_up, ffn_conv_w, ffn_conv_b, ffn_w_down):
    depth = ada_w.shape[0]
    d = x_prompt.shape[-1]
    nbp, seqp, _ = x_prompt.shape
    nbs, seqs, _ = x_sample.shape
    a_width = A_GROUPS * sgu_w.shape[-1]

    cond = jnp.zeros((16, d), f32).at[0].set(c_ctx).at[1:1 + nbs].set(c)
    mods = _ada(cond, ada_w, ada_b).reshape(depth, 16, 6, d)

    n_even = mix_w_in.shape[0]
    w_in = [mix_w_in[i].astype(bf16) for i in range(n_even)]
    sgu_wb = sgu_w.astype(bf16)
    sgu_bb = jnp.broadcast_to(sgu_b[..., None], sgu_b.shape + (sgu_w.shape[-1],))
    w_out_a = [mix_w_out[i, :a_width].astype(bf16) for i in range(n_even)]
    w_out_z = [mix_w_out[i, a_width:].astype(bf16) for i in range(n_even)]
    w_up = ffn_w_up.astype(bf16)
    w_down = ffn_w_down.astype(bf16)
    w_o = [mla_w_o[j].astype(bf16) for j in range(depth // 2)]
    mla_w = [_mla_weights(j, mla_w_dq, mla_q_norm, mla_w_uq, mla_w_dkv, mla_kv_norm, mla_w_ukv,
                          mla_q_head_norm, mla_k_head_norm) for j in range(depth // 2)]

    def trunk(x3, mod_all, cache):
        nb, seq, _ = x3.shape
        x = x3.reshape(nb * seq, d)
        tm = min(seq, 512)
        latent = cache is not None
        tabs = _rope_tables(seq) if latent else None
        new_ckv, new_kr = [], []
        for l in range(depth):
            mod = mod_all[l]
            g1 = norm_g[l, 0].reshape(1, d)
            g2 = norm_g[l, 1].reshape(1, d)
            if l % 2 == 0:
                i = l // 2
                a, xbt = _even_front(x, mod, g1, w_in[i], sgu_wb[i], sgu_bb[i], hy_conv_w[i],
                                     hy_conv_b[i].reshape(1, -1), nb=nb, seq=seq, tm=tm)
                filt = _hyena_filters_t(seq, hy_f_w1[i], hy_f_b1[i], hy_f_w2[i], hy_f_b2[i], hy_f_w3[i],
                                        hy_f_freq[i], hy_decay[i])
                zt = _hyena(xbt, filt, hy_d[i].reshape(-1), nb=nb, seq=seq)
                mixer_ops = [("rows", a, w_out_a[i]), ("cols", zt, w_out_z[i])]
            else:
                j = l // 2
                outs = _mla_front(x, mod, g1, mla_w[j], tabs, nb=nb, seq=seq, tm=tm, emit_cache=not latent)
                q, k, v = outs[:3]
                if latent:
                    past = cache[0].shape[2]
                    ckv_c = cache[0][:, j].reshape(nb * past, KV_RANK)
                    kr_c = jnp.pad(cache[1][:, j].reshape(nb * past, ROPE), ((0, 0), (0, LANES - ROPE)))
                    kc, vc = _cache_expand(ckv_c, kr_c, mla_w[j], nb=nb, seq=past)
                else:
                    kc = vc = None
                    new_ckv.append(outs[3].reshape(nb, seq, KV_RANK))
                    new_kr.append(outs[4].reshape(nb, seq, ROPE))
                if kc is None and seq <= TOEP:
                    at = _attention_short(q, k, v, nb=nb, seq=seq)
                else:
                    at = _attention(q, k, v, kc, vc, nb=nb, seq=seq, tq=min(seq, 1024))
                mixer_ops = [("cols", at, w_o[j])]
            x = _mix_ffn(x, mod, g2, mixer_ops, w_up, ffn_conv_w[l], ffn_conv_b[l].reshape(1, -1), w_down, l,
                         seq=seq, tm=tm)
        return x.reshape(nb, seq, d), new_ckv, new_kr

    y_prompt, ckv_list, kr_list = trunk(x_prompt, mods[:, 0:1], None)
    y_sample, _, _ = trunk(x_sample, mods[:, 1:1 + nbs], (cache_ckv, cache_krope))
    return (y_prompt, y_sample, jnp.stack(ckv_list, axis=1), jnp.stack(kr_list, axis=1))
```

```python
import functools
import math

import jax
import jax.numpy as jnp
import numpy as np
from jax import lax
from jax.experimental import pallas as pl
from jax.experimental.pallas import tpu as pltpu

f32 = jnp.float32
bf16 = jnp.bfloat16

EPS = 1e-6
GRID_W = 64
CHUNK = 128
A_GROUPS = 4
HEADS = 8
NOPE = 128
ROPE = 64
QK = NOPE + ROPE
HEAD_PAD = 256
V_DIM = 128
KV_RANK = 256
ROPE_BASE = 10000.0
FILTER_BANDS = 16
LANES = 128
HALO = 8
TOEP = 256
VMEM_LIMIT = 56 * 1024 * 1024
HIGHEST = lax.Precision.HIGHEST


def _cparams(*sem):
    return pltpu.CompilerParams(dimension_semantics=sem, vmem_limit_bytes=VMEM_LIMIT)


def _norm_mod(x, g, sc, sh):
    y = x * lax.rsqrt(jnp.mean(x * x, axis=-1, keepdims=True) + EPS)
    return (y * g) * (1.0 + sc) + sh


def _halo_ext(x, xp, xn, g, sc, sh, first, last):
    hp = jnp.where(first, 0.0, _norm_mod(xp, g, sc, sh))
    hn = jnp.where(last, 0.0, _norm_mod(xn, g, sc, sh))
    h = _norm_mod(x, g, sc, sh)
    return jnp.concatenate([hp, h, hn], axis=0).astype(bf16)


def _dwconv3(y, w, b, tm):
    return (y[HALO - 1:HALO - 1 + tm] * w[0:1] + y[HALO:HALO + tm] * w[1:2]
            + y[HALO + 1:HALO + 1 + tm] * w[2:3] + b)


def _gelu_tanh(x):
    return 0.5 * x * (1.0 + jnp.tanh(math.sqrt(2.0 / math.pi) * (x + 0.044715 * (x * x * x))))


def _row_specs(tm, d, t_rows):
    r = tm // HALO
    nb = t_rows // HALO
    return (pl.BlockSpec((tm, d), lambda i, *_: (i, 0)),
            pl.BlockSpec((HALO, d), lambda i, *_: (jnp.maximum(i * r - 1, 0), 0)),
            pl.BlockSpec((HALO, d), lambda i, *_: (jnp.minimum((i + 1) * r, nb - 1), 0)))


def _ada_kernel(c_ref, w_ref, b_ref, o_ref):
    c = c_ref[...]
    s = c * jax.nn.sigmoid(c)
    o_ref[0] = jnp.dot(s, w_ref[0], precision=HIGHEST, preferred_element_type=f32) + b_ref[0]


def _ada(cond, ada_w, ada_b):
    depth, d, n = ada_w.shape
    tn = n // 4
    rows = cond.shape[0]
    return pl.pallas_call(
        _ada_kernel,
        out_shape=jax.ShapeDtypeStruct((depth, rows, n), f32),
        grid=(depth, n // tn),
        in_specs=[pl.BlockSpec((rows, d), lambda l, j: (0, 0)),
                  pl.BlockSpec((1, d, tn), lambda l, j: (l, 0, j)),
                  pl.BlockSpec((1, 1, tn), lambda l, j: (l, 0, j))],
        out_specs=pl.BlockSpec((1, rows, tn), lambda l, j: (l, 0, j)),
        compiler_params=_cparams("arbitrary", "arbitrary"),
        name="ada_mod",
    )(cond, ada_w, ada_b.reshape(depth, 1, n))


def _even_front_kernel(x_ref, xp_ref, xn_ref, mod_ref, g_ref, win_ref, sw_ref, sb_ref, cw_ref, cb_ref,
                       a_ref, xbt_ref, *, tm, tpb, a_width):
    i = pl.program_id(0)
    mod = mod_ref[0]
    h = _halo_ext(x_ref[...], xp_ref[...], xn_ref[...], g_ref[...], mod[1:2], mod[0:1],
                  (i % tpb) == 0, (i % tpb) == tpb - 1)
    p = jnp.dot(h, win_ref[...], preferred_element_type=f32)
    u = _gelu_tanh(p[HALO:HALO + tm, 0:a_width])
    v = _gelu_tanh(p[HALO:HALO + tm, a_width:2 * a_width]).astype(bf16)
    nch = tm // CHUNK
    a_ch = a_width // A_GROUPS
    for gi in range(A_GROUPS):
        cols = slice(gi * a_ch, (gi + 1) * a_ch)
        rhs = jnp.concatenate([v[n * CHUNK:(n + 1) * CHUNK, cols] for n in range(nch)], axis=1)
        s = jnp.dot(sw_ref[gi], rhs, preferred_element_type=f32)
        for n in range(nch):
            rows = slice(n * CHUNK, (n + 1) * CHUNK)
            sn = s[:, n * a_ch:(n + 1) * a_ch] + sb_ref[gi]
            a_ref[rows, cols] = (u[rows, cols] * sn).astype(bf16)
    xb = _dwconv3(p[:, 2 * a_width:], cw_ref[...], cb_ref[...], tm)
    xbt_ref[0] = xb.T


def _even_front(x, mod, g, w_in, sgu_w, sgu_b, conv_w, conv_b, *, nb, seq, tm):
    t_rows, d = x.shape
    tpb = seq // tm
    n_in = w_in.shape[1]
    a_width = A_GROUPS * sgu_w.shape[1]
    nxb = n_in - 2 * a_width
    mrows = seq if mod.shape[0] > 1 else t_rows
    kern = functools.partial(_even_front_kernel, tm=tm, tpb=tpb, a_width=a_width)
    full = lambda shape: pl.BlockSpec(shape, lambda i: (0,) * len(shape))
    return pl.pallas_call(
        kern,
        out_shape=(jax.ShapeDtypeStruct((t_rows, a_width), bf16),
                   jax.ShapeDtypeStruct((nb, nxb, seq), f32)),
        grid=(t_rows // tm,),
        in_specs=[*_row_specs(tm, d, t_rows),
                  pl.BlockSpec((1, 6, d), lambda i: ((i * tm) // mrows, 0, 0)),
                  full((1, d)), full(w_in.shape), full(sgu_w.shape), full(sgu_b.shape),
                  full(conv_w.shape), full(conv_b.shape)],
        out_specs=(pl.BlockSpec((tm, a_width), lambda i: (i, 0)),
                   pl.BlockSpec((1, nxb, tm), lambda i: (i // tpb, 0, i % tpb))),
        compiler_params=_cparams("arbitrary"),
        name="even_front",
    )(x, x, x, mod, g, w_in, sgu_w, sgu_b, conv_w, conv_b)


def _filter_kernel(w1t_ref, b1_ref, w2t_ref, b2_ref, fr_ref, w3t_ref, dec_ref, o_ref, hid_scr, *, seq):
    @pl.when(pl.program_id(0) == 0)
    def _():
        nfeat = w1t_ref.shape[1]
        t = lax.broadcasted_iota(jnp.int32, (nfeat, seq), 1).astype(f32)
        fi = lax.broadcasted_iota(jnp.int32, (nfeat, seq), 0)
        tn = t / seq
        band = jnp.where(fi <= FILTER_BANDS, fi, fi - FILTER_BANDS).astype(f32)
        ang = ((2.0 * math.pi) * tn) * band
        z = jnp.where(fi == 0, tn, jnp.where(fi <= FILTER_BANDS, jnp.sin(ang), jnp.cos(ang)))
        z = jnp.where(fi <= 2 * FILTER_BANDS, z, 0.0)
        fr = fr_ref[...]
        h1 = jnp.sin(fr * (jnp.dot(w1t_ref[...], z, precision=HIGHEST, preferred_element_type=f32) + b1_ref[...]))
        hid_scr[...] = jnp.sin(fr * (jnp.dot(w2t_ref[...], h1, precision=HIGHEST, preferred_element_type=f32)
                                     + b2_ref[...]))

    h = jnp.dot(w3t_ref[...], hid_scr[...], precision=HIGHEST, preferred_element_type=f32)
    tl = lax.broadcasted_iota(jnp.int32, (1, seq), 1).astype(f32)
    dist = jnp.abs(tl - (seq // 2)) / seq
    h = h * jnp.exp(-jnp.abs(dec_ref[...]) * dist)
    h = h / (jnp.sum(jnp.abs(h), axis=1, keepdims=True) + EPS)
    hb = h.astype(bf16).astype(f32)
    lane = lax.broadcasted_iota(jnp.int32, hb.shape, 1)
    rolled = pltpu.roll(hb, 1, 1)
    prev_bits = lax.bitcast_convert_type(rolled, jnp.int32)
    cur_bits = lax.shift_right_logical(lax.bitcast_convert_type(hb, jnp.int32), 16)
    o_ref[:, 0:seq] = cur_bits | jnp.where(lane == 0, 0, prev_bits)
    tail_lane = lax.broadcasted_iota(jnp.int32, (hb.shape[0], LANES), 1)
    wrapped = lax.bitcast_convert_type(rolled[:, 0:LANES], jnp.int32)
    o_ref[:, seq:seq + LANES] = jnp.where(tail_lane == 0, wrapped, 0)


def _hyena_filters_t(seq, w1, b1, w2, b2, w3, freq, decay):
    nfeat = 40
    hid = w1.shape[1]
    w1t = jnp.zeros((hid, nfeat), f32).at[:, :w1.shape[0]].set(w1.T)
    rows = w3.shape[1]
    rb = 256
    kern = functools.partial(_filter_kernel, seq=seq)
    full = lambda shape: pl.BlockSpec(shape, lambda i: (0,) * len(shape))
    return pl.pallas_call(
        kern,
        out_shape=jax.ShapeDtypeStruct((rows, seq + LANES), jnp.int32),
        grid=(rows // rb,),
        in_specs=[full((hid, nfeat)), full((hid, 1)), full((hid, hid)), full((hid, 1)), full((hid, 1)),
                  pl.BlockSpec((rb, hid), lambda i: (i, 0)), pl.BlockSpec((rb, 1), lambda i: (i, 0))],
        out_specs=pl.BlockSpec((rb, seq + LANES), lambda i: (i, 0)),
        scratch_shapes=[pltpu.VMEM((hid, seq), f32)],
        compiler_params=_cparams("arbitrary"),
        name="hyena_filter",
    )(w1t, b1.reshape(hid, 1), w2.T, b2.reshape(hid, 1), freq.reshape(hid, 1), w3.T, decay.reshape(rows, 1))


def _bank_steps(hrow, bank_ref, seq):
    nseg = seq // LANES
    hl = LANES // 2
    upper = lax.broadcasted_iota(jnp.int32, (hl, LANES), 1) >= 2 * lax.broadcasted_iota(jnp.int32, (hl, LANES), 0)
    zero = jnp.zeros((hl, LANES), jnp.int32)
    bank_ref[0] = zero
    bank_ref[nseg + 2] = zero
    prev = zero
    for k in range(nseg + 1):
        seg = jnp.broadcast_to(hrow[:, k * LANES:(k + 1) * LANES], (hl, LANES))
        cur = pltpu.roll(seg, 0, 1, stride=2, stride_axis=0)
        bank_ref[k + 1] = jnp.where(upper, cur, prev)
        prev = cur
        yield


def _conv_steps(u, bank_ref, upad_ref, out, *, nb, seq):
    nseg = seq // LANES
    nblk = seq // TOEP
    half = seq // 2
    upad_ref[:, 0:half] = jnp.zeros((nb, half), f32)
    upad_ref[:, half:half + seq] = u
    upad_ref[:, half + seq:2 * seq] = jnp.zeros((nb, half), f32)
    acc = None
    for kt in range(nblk + 1):
        k0 = nseg - 2 * kt
        lhs = jnp.concatenate([upad_ref[:, (i + kt) * TOEP:(i + kt + 1) * TOEP] for i in range(nblk)], axis=0)
        blk = lambda idx: pltpu.bitcast(bank_ref[idx], bf16)
        top = jnp.concatenate([blk(k0 + 1), blk(k0 + 2)], axis=1)
        bot = jnp.concatenate([blk(k0), blk(k0 + 1)], axis=1)
        w = jnp.concatenate([top, bot], axis=0)
        part = jnp.dot(lhs.astype(bf16), w, preferred_element_type=f32)
        acc = part if acc is None else acc + part
        yield
    out.append(jnp.concatenate([acc[i * nb:(i + 1) * nb] for i in range(nblk)], axis=1))


def _interleave(main, side, ratio):
    main, side = list(main), list(side)
    while main or side:
        main = [g for g in main if next(g, StopIteration) is not StopIteration]
        for _ in range(ratio if main else 1 << 30):
            side = [g for g in side if next(g, StopIteration) is not StopIteration]
            if not side:
                break


def _hyena_kernel(d_ref, vb_ref, x1_ref, x2_ref, h0_ref, h1_ref, z_ref,
                  u_scr, a_scr, b_scr, o_scr, bank00, bank01, bank10, bank11, upad0, upad1, *, nb, seq, cg, nchan):
    gidx = pl.program_id(0)
    bank_ref = ((bank00, bank01), (bank10, bank11))
    upad_ref = (upad0, upad1)
    u_scr[...] = jnp.swapaxes(vb_ref[...], 0, 1)
    a_scr[...] = jnp.swapaxes(x1_ref[...], 0, 1)
    b_scr[...] = jnp.swapaxes(x2_ref[...], 0, 1)

    def banks(c):
        gens = []
        if c < cg:
            gens.append(_bank_steps(h0_ref[c:c + 1, :], bank_ref[c % 2][0], seq))
        if 1 <= c <= cg:
            gens.append(_bank_steps(h1_ref[c - 1:c, :], bank_ref[c % 2][1], seq))
        return gens

    ratio = -(-(seq // LANES + 1) // (seq // TOEP + 1))
    _interleave([], banks(0), ratio)
    z_prev = None
    for c in range(cg + 1):
        convs, first, second = [], [], []
        if c < cg:
            u = u_scr[c]
            convs.append(_conv_steps(u, bank_ref[c % 2][0], upad_ref[0], first, nb=nb, seq=seq))
        if c >= 1:
            convs.append(_conv_steps(z_prev, bank_ref[c % 2][1], upad_ref[1], second, nb=nb, seq=seq))
        _interleave(convs, banks(c + 1), ratio)
        if c >= 1:
            o_scr[c - 1] = b_scr[c - 1] * (second[0] + z_prev * d_ref[nchan + gidx * cg + c - 1])
        if c < cg:
            z_prev = a_scr[c] * (first[0] + u * d_ref[gidx * cg + c])
    z_ref[...] = jnp.swapaxes(o_scr[...], 0, 1)


def _hyena(xbt, filt_t, d_flat, *, nb, seq):
    nchan = xbt.shape[1] // 3
    cg = 2 * HALO
    ng = nchan // cg
    kern = functools.partial(_hyena_kernel, nb=nb, seq=seq, cg=cg, nchan=nchan)
    act = lambda off: pl.BlockSpec((nb, cg, seq), lambda i: (0, off + i, 0))
    wseq = seq + LANES
    return pl.pallas_call(
        kern,
        out_shape=jax.ShapeDtypeStruct((nb, nchan, seq), f32),
        grid=(ng,),
        in_specs=[pl.BlockSpec(memory_space=pltpu.SMEM),
                  act(0), act(ng), act(2 * ng),
                  pl.BlockSpec((cg, wseq), lambda i: (i, 0)),
                  pl.BlockSpec((cg, wseq), lambda i: (ng + i, 0))],
        out_specs=pl.BlockSpec((nb, cg, seq), lambda i: (0, i, 0)),
        scratch_shapes=[pltpu.VMEM((cg, nb, seq), f32), pltpu.VMEM((cg, nb, seq), f32),
                        pltpu.VMEM((cg, nb, seq), f32), pltpu.VMEM((cg, nb, seq), f32),
                        *[pltpu.VMEM((seq // LANES + 3, LANES // 2, LANES), jnp.int32) for _ in range(4)],
                        pltpu.VMEM((nb, 2 * seq), f32), pltpu.VMEM((nb, 2 * seq), f32)],
        compiler_params=_cparams("arbitrary"),
        name="hyena_conv",
    )(d_flat, xbt, xbt, xbt, filt_t, filt_t)


def _rope128(x, cos, sin):
    lane = lax.broadcasted_iota(jnp.int32, x.shape, 1)
    swapped = jnp.where((lane & 16) == 0, pltpu.roll(x, LANES - 16, 1), pltpu.roll(x, 16, 1))
    return x * cos + swapped * sin


def _expand_kv(ckv, kr_pad, wuk_ref, wuv_ref, kg_ref, cos, sin, k_ref, v_ref, rows=slice(None)):
    cb = ckv.astype(bf16)
    kn = jnp.dot(cb, wuk_ref[...], preferred_element_type=f32)
    v_ref[0, :, rows] = jnp.dot(wuv_ref[...], ckv.T.astype(bf16), preferred_element_type=f32).astype(bf16)
    kg = kg_ref[...]
    g_nope, g_rope = kg[:, 0:NOPE], kg[:, NOPE:2 * NOPE]
    kr_ss = jnp.sum(kr_pad * kr_pad, axis=-1, keepdims=True)
    krg = kr_pad * g_rope
    if cos is not None:
        krg = _rope128(krg, cos, sin)
    for hd in range(HEADS):
        kh = kn[:, hd * NOPE:(hd + 1) * NOPE]
        r = lax.rsqrt((jnp.sum(kh * kh, axis=-1, keepdims=True) + kr_ss) / QK + EPS)
        k_ref[rows, hd * HEAD_PAD:hd * HEAD_PAD + NOPE] = (kh * r * g_nope).astype(bf16)
        k_ref[rows, hd * HEAD_PAD + NOPE:(hd + 1) * HEAD_PAD] = (krg * r).astype(bf16)


def _mla_front_kernel(*refs, rope, emit_cache, sub):
    (x_ref, mod_ref, g_ref, wdq_ref, qn_ref, wuq_ref, qg_ref, wdkv_ref, kvn_ref, wuk_ref, wuv_ref, kg_ref) = refs[:12]
    refs = refs[12:]
    if rope:
        cos_ref, sin_ref, qcos_ref, qsin_ref = refs[:4]
        refs = refs[4:]
    q_ref, k_ref, v_ref = refs[:3]
    mod = mod_ref[0]
    c_exp = (1.0 / math.sqrt(QK)) * math.log2(math.e)
    g16 = ROPE // 4
    tm = x_ref.shape[0]
    for r0 in range(0, tm, sub):
        rows = slice(r0, r0 + sub)
        cos = sin = None
        if rope:
            cos, sin, qcos, qsin = cos_ref[rows, :], sin_ref[rows, :], qcos_ref[:, rows], qsin_ref[:, rows]
        h = _norm_mod(x_ref[rows, :], g_ref[...], mod[1:2], mod[0:1]).astype(bf16)
        ql = jnp.dot(h, wdq_ref[...], preferred_element_type=f32)
        ql = ql * lax.rsqrt(jnp.mean(ql * ql, axis=-1, keepdims=True) + EPS) * qn_ref[...]
        qt = jnp.dot(wuq_ref[...], ql.T.astype(bf16), preferred_element_type=f32)
        qg = qg_ref[:, rows]
        for hd in range(HEADS):
            qh = qt[hd * HEAD_PAD:(hd + 1) * HEAD_PAD]
            ssq = jnp.sum(qh * qh, axis=0, keepdims=True)
            qh = qh * (lax.rsqrt(ssq / QK + EPS) * c_exp) * qg
            if rope:
                x = qh[NOPE:QK]
                swapped = jnp.concatenate([x[g16:2 * g16], x[0:g16], x[3 * g16:], x[2 * g16:3 * g16]], axis=0)
                qh = jnp.concatenate([qh[0:NOPE], x * qcos + swapped * qsin, qh[QK:]], axis=0)
            q_ref[0, hd * HEAD_PAD:(hd + 1) * HEAD_PAD, rows] = qh.astype(bf16)
        dkv = jnp.dot(h, wdkv_ref[...], preferred_element_type=f32)
        c_raw = dkv[:, 0:KV_RANK]
        ckv = c_raw * lax.rsqrt(jnp.mean(c_raw * c_raw, axis=-1, keepdims=True) + EPS) * kvn_ref[...]
        kr_pad = dkv[:, KV_RANK:]
        if emit_cache:
            ckv_ref, kr_ref = refs[3:5]
            ckv_ref[rows, :] = ckv
            kr_ref[rows, :] = kr_pad[:, 0:ROPE]
        _expand_kv(ckv, kr_pad, wuk_ref, wuv_ref, kg_ref, cos, sin, k_ref, v_ref, rows)


def _mla_front(x, mod, g, w, tabs, *, nb, seq, tm, emit_cache):
    t_rows, d = x.shape
    tpb = seq // tm
    rope = tabs is not None
    mrows = seq if mod.shape[0] > 1 else t_rows
    kern = functools.partial(_mla_front_kernel, rope=rope, emit_cache=emit_cache, sub=min(TOEP, tm))
    full = lambda a: pl.BlockSpec(a.shape, lambda i: (0,) * a.ndim)
    wnames = ("w_dq", "q_norm", "w_uq", "q_gain", "w_dkv", "kv_norm", "w_uk", "w_uv", "k_gain")
    w = dict(w, q_gain=jnp.broadcast_to(w["q_gain"].reshape(HEAD_PAD, 1), (HEAD_PAD, tm)))
    ins = [x, mod, g] + [w[n] for n in wnames]
    in_specs = [pl.BlockSpec((tm, d), lambda i: (i, 0)),
                pl.BlockSpec((1, 6, d), lambda i: ((i * tm) // mrows, 0, 0)),
                full(g)] + [full(w[n]) for n in wnames]
    if rope:
        ins += list(tabs)
        in_specs += [pl.BlockSpec((tm, LANES), lambda i: (i % tpb, 0))] * 2
        in_specs += [pl.BlockSpec((ROPE, tm), lambda i: (0, i % tpb))] * 2
    hp = HEADS * HEAD_PAD
    out_shape = [jax.ShapeDtypeStruct((nb, hp, seq), bf16), jax.ShapeDtypeStruct((t_rows, hp), bf16),
                 jax.ShapeDtypeStruct((nb, HEADS * V_DIM, seq), bf16)]
    out_specs = [pl.BlockSpec((1, hp, tm), lambda i: (i // tpb, 0, i % tpb)), pl.BlockSpec((tm, hp), lambda i: (i, 0)),
                 pl.BlockSpec((1, HEADS * V_DIM, tm), lambda i: (i // tpb, 0, i % tpb))]
    if emit_cache:
        out_shape += [jax.ShapeDtypeStruct((t_rows, KV_RANK), f32), jax.ShapeDtypeStruct((t_rows, ROPE), f32)]
        out_specs += [pl.BlockSpec((tm, KV_RANK), lambda i: (i, 0)), pl.BlockSpec((tm, ROPE), lambda i: (i, 0))]
    return pl.pallas_call(
        kern, out_shape=tuple(out_shape), grid=(t_rows // tm,), in_specs=in_specs, out_specs=tuple(out_specs),
        compiler_params=_cparams("arbitrary"), name="mla_front",
    )(*ins)


def _cache_expand_kernel(ckv_ref, kr_ref, wuk_ref, wuv_ref, kg_ref, k_ref, v_ref):
    _expand_kv(ckv_ref[...], kr_ref[...], wuk_ref, wuv_ref, kg_ref, None, None, k_ref, v_ref)


def _cache_expand(ckv, kr_pad, w, *, nb, seq):
    t_rows = ckv.shape[0]
    tm = seq
    full = lambda a: pl.BlockSpec(a.shape, lambda i: (0,) * a.ndim)
    hp = HEADS * HEAD_PAD
    return pl.pallas_call(
        _cache_expand_kernel,
        out_shape=(jax.ShapeDtypeStruct((t_rows, hp), bf16), jax.ShapeDtypeStruct((nb, HEADS * V_DIM, seq), bf16)),
        grid=(t_rows // tm,),
        in_specs=[pl.BlockSpec((tm, KV_RANK), lambda i: (i, 0)), pl.BlockSpec((tm, LANES), lambda i: (i, 0)),
                  full(w["w_uk"]), full(w["w_uv"]), full(w["k_gain"])],
        out_specs=(pl.BlockSpec((tm, hp), lambda i: (i, 0)),
                   pl.BlockSpec((1, HEADS * V_DIM, tm), lambda i: (i, 0, 0))),
        compiler_params=_cparams("arbitrary"), name="mla_cache_expand",
    )(ckv, kr_pad, w["w_uk"], w["w_uv"], w["k_gain"])


def _attn_kernel(*refs, cached, tk):
    if cached:
        q_ref, k_ref, kc_ref, vt_ref, vct_ref, o_ref, s_scr, m_scr = refs
        parts = ((k_ref, vt_ref), (kc_ref, vct_ref))
    else:
        q_ref, k_ref, vt_ref, o_ref, s_scr, m_scr = refs
        parts = ((k_ref, vt_ref),)

    @pl.when(pl.program_id(0) == 0)
    def _():
        s_scr[...] = jnp.zeros_like(s_scr)
        m_scr[...] = jnp.zeros_like(m_scr)

    q = q_ref[0]
    m_old = m_scr[...]
    m_new = acc = None
    base = 0
    for kr, vtr in parts:
        ck = min(tk, kr.shape[0])
        ones = jnp.ones((16, ck), bf16)
        for j in range(kr.shape[0] // ck):
            rows = slice(base + j * ck, base + (j + 1) * ck)
            p = jnp.exp2(s_scr[rows, :] - m_old).astype(bf16)
            lhs = jnp.concatenate([vtr[0, :, j * ck:(j + 1) * ck], ones], axis=0)
            part = jnp.dot(lhs, p, preferred_element_type=f32)
            acc = part if acc is None else acc + part
            s = jnp.dot(kr[j * ck:(j + 1) * ck, :], q, preferred_element_type=f32)
            s_scr[rows, :] = s
            cm = jnp.max(s, axis=0, keepdims=True)
            m_new = cm if m_new is None else jnp.maximum(m_new, cm)
        base += kr.shape[0]
    m_scr[...] = m_new
    o_ref[0] = (acc[0:V_DIM] / acc[V_DIM:V_DIM + 1]).astype(bf16)


def _attention(q, k, vt, kc, vct, *, nb, seq, tq):
    cached = kc is not None
    nq = seq // tq
    n_tiles = nb * HEADS * nq
    past = kc.shape[0] // nb if cached else 0
    tk = min(seq, TOEP)

    def cur(n):
        n = jnp.minimum(n, n_tiles - 1)
        return n // (HEADS * nq), (n // nq) % HEADS, n % nq

    def prev(n):
        n = jnp.maximum(n - 1, 0)
        return n // (HEADS * nq), (n // nq) % HEADS, n % nq

    def spec(shape, fn):
        return pl.BlockSpec(shape, fn)

    kern = functools.partial(_attn_kernel, cached=cached, tk=tk)
    q_spec = spec((1, HEAD_PAD, tq), lambda n: cur(n))
    k_spec = lambda rows: spec((rows, HEAD_PAD), lambda n: (cur(n)[0], cur(n)[1]))
    v_spec = lambda cols: spec((1, V_DIM, cols), lambda n: (prev(n)[0], prev(n)[1], 0))
    if cached:
        ins = [q, k, kc, vt, vct]
        in_specs = [q_spec, k_spec(seq), k_spec(past), v_spec(seq), v_spec(past)]
    else:
        ins = [q, k, vt]
        in_specs = [q_spec, k_spec(seq), v_spec(seq)]
    return pl.pallas_call(
        kern,
        out_shape=jax.ShapeDtypeStruct((nb, HEADS * V_DIM, seq), bf16),
        grid=(n_tiles + 1,),
        in_specs=in_specs,
        out_specs=pl.BlockSpec((1, V_DIM, tq), lambda n: prev(n)),
        scratch_shapes=[pltpu.VMEM((seq + past, tq), f32), pltpu.VMEM((1, tq), f32)],
        compiler_params=_cparams("arbitrary"),
        name="mla_attention",
    )(*ins)


def _attn_short_kernel(q_ref, k_ref, vt_ref, o_ref):
    seq = k_ref.shape[0]
    ones = jnp.ones((16, seq), bf16)
    for hd in range(HEADS):
        cols = slice(hd * HEAD_PAD, (hd + 1) * HEAD_PAD)
        s = jnp.dot(k_ref[:, cols], q_ref[0, cols, :], preferred_element_type=f32)
        p = jnp.exp2(s - jnp.max(s, axis=0, keepdims=True)).astype(bf16)
        rows = slice(hd * V_DIM, (hd + 1) * V_DIM)
        acc = jnp.dot(jnp.concatenate([vt_ref[0, rows, :], ones], axis=0), p, preferred_element_type=f32)
        o_ref[0, rows, :] = (acc[0:V_DIM] / acc[V_DIM:V_DIM + 1]).astype(bf16)


def _attention_short(q, k, vt, *, nb, seq):
    return pl.pallas_call(
        _attn_short_kernel,
        out_shape=jax.ShapeDtypeStruct((nb, HEADS * V_DIM, seq), bf16),
        grid=(nb,),
        in_specs=[pl.BlockSpec((1, HEADS * HEAD_PAD, seq), lambda b: (b, 0, 0)),
                  pl.BlockSpec((seq, HEADS * HEAD_PAD), lambda b: (b, 0)),
                  pl.BlockSpec((1, HEADS * V_DIM, seq), lambda b: (b, 0, 0))],
        out_specs=pl.BlockSpec((1, HEADS * V_DIM, seq), lambda b: (b, 0, 0)),
        compiler_params=_cparams("arbitrary"),
        name="mla_attention_short",
    )(q, k, vt)


ROW_HALO = 16


def _mix_ffn_kernel(*refs, tm, tpb, chunks, kinds):
    x_ref, xp_ref, xn_ref, mod_ref, g_ref = refs[:5]
    wup_ref, cw_ref, cb_ref, wd_ref, o_ref = refs[5 + 4 * len(kinds):]
    i = pl.program_id(0)
    mod = mod_ref[0]
    tn = (((0,), (0,)), ((), ()))
    proj = [None, None, None]
    for n, kind in enumerate(kinds):
        main, prev, nxt, w_ref = refs[5 + 4 * n:9 + 4 * n]
        w = w_ref[...]
        if kind == "rows":
            parts = [jnp.dot(main[...], w, preferred_element_type=f32),
                     jnp.dot(prev[...], w, preferred_element_type=f32)[ROW_HALO - HALO:ROW_HALO],
                     jnp.dot(nxt[...], w, preferred_element_type=f32)[0:HALO]]
        else:
            ops = [main[0], prev[0][:, LANES - HALO:LANES], nxt[0][:, 0:HALO]]
            parts = [lax.dot_general(o.astype(bf16), w, tn, preferred_element_type=f32) for o in ops]
        proj = [p if q is None else q + p for q, p in zip(proj, parts)]
    g1 = mod[2:3]
    x1 = x_ref[...] + g1 * proj[0]
    h = _halo_ext(x1, xp_ref[...] + g1 * proj[1], xn_ref[...] + g1 * proj[2], g_ref[...], mod[4:5], mod[3:4],
                  (i % tpb) == 0, (i % tpb) == tpb - 1)
    dff = wd_ref.shape[1]
    acc = None
    for c0, c1 in chunks:
        halves = []
        for off in (c0, dff + c0):
            cols = slice(off, off + c1 - c0)
            y = jnp.dot(h, wup_ref[0, :, cols], preferred_element_type=f32)
            halves.append(_dwconv3(y, cw_ref[:, cols], cb_ref[:, cols], tm))
        gate, up = halves
        act = ((gate * jax.nn.sigmoid(gate)) * up).astype(bf16)
        part = jnp.dot(act, wd_ref[0, c0:c1, :], preferred_element_type=f32)
        acc = part if acc is None else acc + part
    o_ref[...] = x1 + mod[5:6] * acc


def _mix_ffn(x, mod, g, mixer_ops, w_up, conv_w, conv_b, w_down, layer, *, seq, tm):
    t_rows, d = x.shape
    dff = w_down.shape[1]
    step = 6 * TOEP
    chunks = tuple((c, min(c + step, dff)) for c in range(0, dff, step))
    tpb = seq // tm
    mrows = seq if mod.shape[0] > 1 else t_rows
    kinds = tuple(k for k, _, _ in mixer_ops)
    kern = functools.partial(_mix_ffn_kernel, tm=tm, tpb=tpb, chunks=chunks, kinds=kinds)
    resident = lambda a: pl.BlockSpec(a.shape, lambda i: (0,) * a.ndim, pipeline_mode=pl.Buffered(1))
    layer_block = lambda a: pl.BlockSpec((1,) + a.shape[1:], lambda i: (layer,) + (0,) * (a.ndim - 1),
                                         pipeline_mode=pl.Buffered(1))
    ins, in_specs = [], []
    for kind, op, w in mixer_ops:
        c = op.shape[1]
        if kind == "rows":
            r, nblk = tm // ROW_HALO, t_rows // ROW_HALO
            in_specs += [pl.BlockSpec((tm, c), lambda i: (i, 0)),
                         pl.BlockSpec((ROW_HALO, c), lambda i, r=r: (jnp.maximum(i * r - 1, 0), 0)),
                         pl.BlockSpec((ROW_HALO, c), lambda i, r=r, nblk=nblk: (jnp.minimum((i + 1) * r, nblk - 1), 0))]
        else:
            r, nblk = tm // LANES, seq // LANES
            in_specs += [pl.BlockSpec((1, c, tm), lambda i: (i // tpb, 0, i % tpb)),
                         pl.BlockSpec((1, c, LANES), lambda i, r=r: (i // tpb, 0, jnp.maximum((i % tpb) * r - 1, 0))),
                         pl.BlockSpec((1, c, LANES),
                                      lambda i, r=r, nblk=nblk: (i // tpb, 0, jnp.minimum((i % tpb + 1) * r, nblk - 1)))]
        in_specs.append(resident(w))
        ins += [op, op, op, w]
    return pl.pallas_call(
        kern,
        out_shape=jax.ShapeDtypeStruct((t_rows, d), f32),
        grid=(t_rows // tm,),
        in_specs=[*_row_specs(tm, d, t_rows),
                  pl.BlockSpec((1, 6, d), lambda i: ((i * tm) // mrows, 0, 0)),
                  resident(g), *in_specs,
                  layer_block(w_up), resident(conv_w), resident(conv_b), layer_block(w_down)],
        out_specs=pl.BlockSpec((tm, d), lambda i: (i, 0)),
        compiler_params=_cparams("arbitrary"),
        name="mix_ffn",
    )(x, x, x, mod, g, *ins, w_up, conv_w, conv_b, w_down)


def _rope_tables(seq):
    half = ROPE // 2
    rows = seq // GRID_W
    row = np.repeat(np.arange(rows), GRID_W).astype(np.float32)
    col = np.tile(np.arange(GRID_W), rows).astype(np.float32)
    inv = (1.0 / (ROPE_BASE ** (np.arange(0, half, 2, dtype=np.float32) / half))).astype(np.float32)
    ar = (row[:, None] * inv[None]).astype(np.float64)
    ac = (col[:, None] * inv[None]).astype(np.float64)
    pad = np.zeros((seq, LANES - ROPE))
    cos64 = np.concatenate([np.cos(ar), np.cos(ar), np.cos(ac), np.cos(ac)], axis=1)
    sin64 = np.concatenate([-np.sin(ar), np.sin(ar), -np.sin(ac), np.sin(ac)], axis=1)
    as_f32 = lambda a: jnp.asarray(a.astype(np.float32))
    return (as_f32(np.concatenate([cos64, pad], axis=1)), as_f32(np.concatenate([sin64, pad], axis=1)),
            as_f32(cos64.T), as_f32(sin64.T))


def _pad_heads(a, width):
    lead = a.shape[:-1]
    a = a.reshape(*lead, HEADS, -1)
    a = jnp.pad(a, [(0, 0)] * len(lead) + [(0, 0), (0, width - a.shape[-1])])
    return a.reshape(*lead, HEADS * width)


def _mla_weights(j, mla_w_dq, mla_q_norm, mla_w_uq, mla_w_dkv, mla_kv_norm, mla_w_ukv, mla_q_head_norm,
                 mla_k_head_norm):
    ukv = mla_w_ukv[j].reshape(KV_RANK, HEADS, NOPE + V_DIM)
    kg = mla_k_head_norm[j]
    return {
        "w_dq": mla_w_dq[j].astype(bf16),
        "q_norm": mla_q_norm[j].reshape(1, -1),
        "w_uq": _pad_heads(mla_w_uq[j], HEAD_PAD).T.astype(bf16),
        "q_gain": jnp.pad(mla_q_head_norm[j], (0, HEAD_PAD - QK)),
        "w_dkv": jnp.pad(mla_w_dkv[j], ((0, 0), (0, LANES - ROPE))).astype(bf16),
        "kv_norm": mla_kv_norm[j].reshape(1, -1),
        "w_uk": ukv[:, :, :NOPE].reshape(KV_RANK, HEADS * NOPE).astype(bf16),
        "w_uv": ukv[:, :, NOPE:].reshape(KV_RANK, HEADS * V_DIM).T.astype(bf16),
        "k_gain": jnp.pad(kg, (0, 2 * NOPE - QK)).reshape(1, 2 * NOPE),
    }


def kernel(x_prompt, x_sample, cache_ckv, cache_krope, c, c_ctx, ada_w, ada_b, norm_g, mix_w_in, sgu_w, sgu_b, hy_conv_w, hy_conv_b, hy_f_w1, hy_f_b1, hy_f_w2, hy_f_b2, hy_f_w3, hy_f_freq, hy_decay, hy_d, mix_w_out, mla_w_dq, mla_q_norm, mla_w_uq, mla_w_dkv, mla_kv_norm, mla_w_ukv, mla_q_head_norm, mla_k_head_norm, mla_w_o, ffn_w_up, ffn_conv_w, ffn_conv_b, ffn_w_down):
    depth = ada_w.shape[0]
    d = x_prompt.shape[-1]
    nbp, seqp, _ = x_prompt.shape
    nbs, seqs, _ = x_sample.shape
    a_width = A_GROUPS * sgu_w.shape[-1]

    cond = jnp.zeros((16, d), f32).at[0].set(c_ctx).at[1:1 + nbs].set(c)
    mods = _ada(cond, ada_w, ada_b).reshape(depth, 16, 6, d)

    n_even = mix_w_in.shape[0]
    w_in = [mix_w_in[i].astype(bf16) for i in range(n_even)]
    sgu_wb = sgu_w.astype(bf16)
    sgu_bb = jnp.broadcast_to(sgu_b[..., None], sgu_b.shape + (sgu_w.shape[-1],))
    w_out_a = [mix_w_out[i, :a_width].astype(bf16) for i in range(n_even)]
    w_out_z = [mix_w_out[i, a_width:].astype(bf16) for i in range(n_even)]
    w_up = ffn_w_up.astype(bf16)
    w_down = ffn_w_down.astype(bf16)
    w_o = [mla_w_o[j].astype(bf16) for j in range(depth // 2)]
    mla_w = [_mla_weights(j, mla_w_dq, mla_q_norm, mla_w_uq, mla_w_dkv, mla_kv_norm, mla_w_ukv,
                          mla_q_head_norm, mla_k_head_norm) for j in range(depth // 2)]

    def trunk(x3, mod_all, cache):
        nb, seq, _ = x3.shape
        x = x3.reshape(nb * seq, d)
        tm = min(seq, 512)
        latent = cache is not None
        tabs = _rope_tables(seq) if latent else None
        new_ckv, new_kr = [], []
        for l in range(depth):
            mod = mod_all[l]
            g1 = norm_g[l, 0].reshape(1, d)
            g2 = norm_g[l, 1].reshape(1, d)
            if l % 2 == 0:
                i = l // 2
                a, xbt = _even_front(x, mod, g1, w_in[i], sgu_wb[i], sgu_bb[i], hy_conv_w[i],
                                     hy_conv_b[i].reshape(1, -1), nb=nb, seq=seq, tm=tm)
                filt = _hyena_filters_t(seq, hy_f_w1[i], hy_f_b1[i], hy_f_w2[i], hy_f_b2[i], hy_f_w3[i],
                                        hy_f_freq[i], hy_decay[i])
                zt = _hyena(xbt, filt, hy_d[i].reshape(-1), nb=nb, seq=seq)
                mixer_ops = [("rows", a, w_out_a[i]), ("cols", zt, w_out_z[i])]
            else:
                j = l // 2
                outs = _mla_front(x, mod, g1, mla_w[j], tabs, nb=nb, seq=seq, tm=tm, emit_cache=not latent)
                q, k, v = outs[:3]
                if latent:
                    past = cache[0].shape[2]
                    ckv_c = cache[0][:, j].reshape(nb * past, KV_RANK)
                    kr_c = jnp.pad(cache[1][:, j].reshape(nb * past, ROPE), ((0, 0), (0, LANES - ROPE)))
                    kc, vc = _cache_expand(ckv_c, kr_c, mla_w[j], nb=nb, seq=past)
                else:
                    kc = vc = None
                    new_ckv.append(outs[3].reshape(nb, seq, KV_RANK))
                    new_kr.append(outs[4].reshape(nb, seq, ROPE))
                if kc is None and seq <= TOEP:
                    at = _attention_short(q, k, v, nb=nb, seq=seq)
                else:
                    at = _attention(q, k, v, kc, vc, nb=nb, seq=seq, tq=min(seq, 1024))
                mixer_ops = [("cols", at, w_o[j])]
            x = _mix_ffn(x, mod, g2, mixer_ops, w_up, ffn_conv_w[l], ffn_conv_b[l].reshape(1, -1), w_down, l,
                         seq=seq, tm=tm)
        return x.reshape(nb, seq, d), new_ckv, new_kr

    y_prompt, ckv_list, kr_list = trunk(x_prompt, mods[:, 0:1], None)
    y_sample, _, _ = trunk(x_sample, mods[:, 1:1 + nbs], (cache_ckv, cache_krope))
    return (y_prompt, y_sample, jnp.stack(ckv_list, axis=1), jnp.stack(kr_list, axis=1))
```

```python
import functools
import math

import jax
import jax.numpy as jnp
import numpy as np
from jax import lax
from jax.experimental import pallas as pl
from jax.experimental.pallas import tpu as pltpu

f32 = jnp.float32
bf16 = jnp.bfloat16

EPS = 1e-6
GRID_W = 64
CHUNK = 128
A_GROUPS = 4
HEADS = 8
NOPE = 128
ROPE = 64
QK = NOPE + ROPE
HEAD_PAD = 256
V_DIM = 128
KV_RANK = 256
ROPE_BASE = 10000.0
FILTER_BANDS = 16
LANES = 128
HALO = 8
TOEP = 256
VMEM_LIMIT = 56 * 1024 * 1024
HIGHEST = lax.Precision.HIGHEST


def _cparams(*sem):
    return pltpu.CompilerParams(dimension_semantics=sem, vmem_limit_bytes=VMEM_LIMIT)


def _norm_mod(x, g, sc, sh):
    y = x * lax.rsqrt(jnp.mean(x * x, axis=-1, keepdims=True) + EPS)
    return (y * g) * (1.0 + sc) + sh


def _halo_ext(x, xp, xn, g, sc, sh, first, last):
    hp = jnp.where(first, 0.0, _norm_mod(xp, g, sc, sh))
    hn = jnp.where(last, 0.0, _norm_mod(xn, g, sc, sh))
    h = _norm_mod(x, g, sc, sh)
    return jnp.concatenate([hp, h, hn], axis=0).astype(bf16)


def _dwconv3(y, w, b, tm):
    return (y[HALO - 1:HALO - 1 + tm] * w[0:1] + y[HALO:HALO + tm] * w[1:2]
            + y[HALO + 1:HALO + 1 + tm] * w[2:3] + b)


def _gelu_tanh(x):
    return 0.5 * x * (1.0 + jnp.tanh(math.sqrt(2.0 / math.pi) * (x + 0.044715 * (x * x * x))))


def _row_specs(tm, d, t_rows):
    r = tm // HALO
    nb = t_rows // HALO
    return (pl.BlockSpec((tm, d), lambda i, *_: (i, 0)),
            pl.BlockSpec((HALO, d), lambda i, *_: (jnp.maximum(i * r - 1, 0), 0)),
            pl.BlockSpec((HALO, d), lambda i, *_: (jnp.minimum((i + 1) * r, nb - 1), 0)))


def _ada_kernel(c_ref, w_ref, b_ref, o_ref):
    c = c_ref[...]
    s = c * jax.nn.sigmoid(c)
    o_ref[0] = jnp.dot(s, w_ref[0], precision=HIGHEST, preferred_element_type=f32) + b_ref[0]


def _ada(cond, ada_w, ada_b):
    depth, d, n = ada_w.shape
    tn = n // 4
    rows = cond.shape[0]
    return pl.pallas_call(
        _ada_kernel,
        out_shape=jax.ShapeDtypeStruct((depth, rows, n), f32),
        grid=(depth, n // tn),
        in_specs=[pl.BlockSpec((rows, d), lambda l, j: (0, 0)),
                  pl.BlockSpec((1, d, tn), lambda l, j: (l, 0, j)),
                  pl.BlockSpec((1, 1, tn), lambda l, j: (l, 0, j))],
        out_specs=pl.BlockSpec((1, rows, tn), lambda l, j: (l, 0, j)),
        compiler_params=_cparams("arbitrary", "arbitrary"),
        name="ada_mod",
    )(cond, ada_w, ada_b.reshape(depth, 1, n))


def _even_front_kernel(x_ref, xp_ref, xn_ref, mod_ref, g_ref, win_ref, sw_ref, sb_ref, cw_ref, cb_ref,
                       a_ref, xbt_ref, *, tm, tpb, a_width):
    i = pl.program_id(0)
    mod = mod_ref[0]
    h = _halo_ext(x_ref[...], xp_ref[...], xn_ref[...], g_ref[...], mod[1:2], mod[0:1],
                  (i % tpb) == 0, (i % tpb) == tpb - 1)
    p = jnp.dot(h, win_ref[...], preferred_element_type=f32)
    u = _gelu_tanh(p[HALO:HALO + tm, 0:a_width])
    v = _gelu_tanh(p[HALO:HALO + tm, a_width:2 * a_width]).astype(bf16)
    nch = tm // CHUNK
    a_ch = a_width // A_GROUPS
    for gi in range(A_GROUPS):
        cols = slice(gi * a_ch, (gi + 1) * a_ch)
        rhs = jnp.concatenate([v[n * CHUNK:(n + 1) * CHUNK, cols] for n in range(nch)], axis=1)
        s = jnp.dot(sw_ref[gi], rhs, preferred_element_type=f32)
        for n in range(nch):
            rows = slice(n * CHUNK, (n + 1) * CHUNK)
            sn = s[:, n * a_ch:(n + 1) * a_ch] + sb_ref[gi]
            a_ref[rows, cols] = (u[rows, cols] * sn).astype(bf16)
    xb = _dwconv3(p[:, 2 * a_width:], cw_ref[...], cb_ref[...], tm)
    xbt_ref[0] = xb.T


def _even_front(x, mod, g, w_in, sgu_w, sgu_b, conv_w, conv_b, *, nb, seq, tm):
    t_rows, d = x.shape
    tpb = seq // tm
    n_in = w_in.shape[1]
    a_width = A_GROUPS * sgu_w.shape[1]
    nxb = n_in - 2 * a_width
    mrows = seq if mod.shape[0] > 1 else t_rows
    kern = functools.partial(_even_front_kernel, tm=tm, tpb=tpb, a_width=a_width)
    full = lambda shape: pl.BlockSpec(shape, lambda i: (0,) * len(shape))
    return pl.pallas_call(
        kern,
        out_shape=(jax.ShapeDtypeStruct((t_rows, a_width), bf16),
                   jax.ShapeDtypeStruct((nb, nxb, seq), f32)),
        grid=(t_rows // tm,),
        in_specs=[*_row_specs(tm, d, t_rows),
                  pl.BlockSpec((1, 6, d), lambda i: ((i * tm) // mrows, 0, 0)),
                  full((1, d)), full(w_in.shape), full(sgu_w.shape), full(sgu_b.shape),
                  full(conv_w.shape), full(conv_b.shape)],
        out_specs=(pl.BlockSpec((tm, a_width), lambda i: (i, 0)),
                   pl.BlockSpec((1, nxb, tm), lambda i: (i // tpb, 0, i % tpb))),
        compiler_params=_cparams("arbitrary"),
        name="even_front",
    )(x, x, x, mod, g, w_in, sgu_w, sgu_b, conv_w, conv_b)


def _filter_kernel(w1t_ref, b1_ref, w2t_ref, b2_ref, fr_ref, w3t_ref, dec_ref, o_ref, hid_scr, *, seq):
    @pl.when(pl.program_id(0) == 0)
    def _():
        nfeat = w1t_ref.shape[1]
        t = lax.broadcasted_iota(jnp.int32, (nfeat, seq), 1).astype(f32)
        fi = lax.broadcasted_iota(jnp.int32, (nfeat, seq), 0)
        tn = t / seq
        band = jnp.where(fi <= FILTER_BANDS, fi, fi - FILTER_BANDS).astype(f32)
        ang = ((2.0 * math.pi) * tn) * band
        z = jnp.where(fi == 0, tn, jnp.where(fi <= FILTER_BANDS, jnp.sin(ang), jnp.cos(ang)))
        z = jnp.where(fi <= 2 * FILTER_BANDS, z, 0.0)
        fr = fr_ref[...]
        h1 = jnp.sin(fr * (jnp.dot(w1t_ref[...], z, precision=HIGHEST, preferred_element_type=f32) + b1_ref[...]))
        hid_scr[...] = jnp.sin(fr * (jnp.dot(w2t_ref[...], h1, precision=HIGHEST, preferred_element_type=f32)
                                     + b2_ref[...]))

    h = jnp.dot(w3t_ref[...], hid_scr[...], precision=HIGHEST, preferred_element_type=f32)
    tl = lax.broadcasted_iota(jnp.int32, (1, seq), 1).astype(f32)
    dist = jnp.abs(tl - (seq // 2)) / seq
    h = h * jnp.exp(-jnp.abs(dec_ref[...]) * dist)
    h = h / (jnp.sum(jnp.abs(h), axis=1, keepdims=True) + EPS)
    hb = h.astype(bf16).astype(f32)
    lane = lax.broadcasted_iota(jnp.int32, hb.shape, 1)
    rolled = pltpu.roll(hb, 1, 1)
    prev_bits = lax.bitcast_convert_type(rolled, jnp.int32)
    cur_bits = lax.shift_right_logical(lax.bitcast_convert_type(hb, jnp.int32), 16)
    o_ref[:, 0:seq] = cur_bits | jnp.where(lane == 0, 0, prev_bits)
    tail_lane = lax.broadcasted_iota(jnp.int32, (hb.shape[0], LANES), 1)
    wrapped = lax.bitcast_convert_type(rolled[:, 0:LANES], jnp.int32)
    o_ref[:, seq:seq + LANES] = jnp.where(tail_lane == 0, wrapped, 0)


def _hyena_filters_t(seq, w1, b1, w2, b2, w3, freq, decay):
    nfeat = 40
    hid = w1.shape[1]
    w1t = jnp.zeros((hid, nfeat), f32).at[:, :w1.shape[0]].set(w1.T)
    rows = w3.shape[1]
    rb = 256
    kern = functools.partial(_filter_kernel, seq=seq)
    full = lambda shape: pl.BlockSpec(shape, lambda i: (0,) * len(shape))
    return pl.pallas_call(
        kern,
        out_shape=jax.ShapeDtypeStruct((rows, seq + LANES), jnp.int32),
        grid=(rows // rb,),
        in_specs=[full((hid, nfeat)), full((hid, 1)), full((hid, hid)), full((hid, 1)), full((hid, 1)),
                  pl.BlockSpec((rb, hid), lambda i: (i, 0)), pl.BlockSpec((rb, 1), lambda i: (i, 0))],
        out_specs=pl.BlockSpec((rb, seq + LANES), lambda i: (i, 0)),
        scratch_shapes=[pltpu.VMEM((hid, seq), f32)],
        compiler_params=_cparams("arbitrary"),
        name="hyena_filter",
    )(w1t, b1.reshape(hid, 1), w2.T, b2.reshape(hid, 1), freq.reshape(hid, 1), w3.T, decay.reshape(rows, 1))


def _bank_steps(hrow, bank_ref, seq):
    nseg = seq // LANES
    hl = LANES // 2
    upper = lax.broadcasted_iota(jnp.int32, (hl, LANES), 1) >= 2 * lax.broadcasted_iota(jnp.int32, (hl, LANES), 0)
    zero = jnp.zeros((hl, LANES), jnp.int32)
    bank_ref[0] = zero
    bank_ref[nseg + 2] = zero
    prev = zero
    for k in range(nseg + 1):
        seg = jnp.broadcast_to(hrow[:, k * LANES:(k + 1) * LANES], (hl, LANES))
        cur = pltpu.roll(seg, 0, 1, stride=2, stride_axis=0)
        bank_ref[k + 1] = jnp.where(upper, cur, prev)
        prev = cur
        yield


def _conv_steps(u, bank_ref, upad_ref, out, *, nb, seq):
    nseg = seq // LANES
    nblk = seq // TOEP
    half = seq // 2
    upad_ref[:, 0:half] = jnp.zeros((nb, half), f32)
    upad_ref[:, half:half + seq] = u
    upad_ref[:, half + seq:2 * seq] = jnp.zeros((nb, half), f32)
    acc = None
    for kt in range(nblk + 1):
        k0 = nseg - 2 * kt
        lhs = jnp.concatenate([upad_ref[:, (i + kt) * TOEP:(i + kt + 1) * TOEP] for i in range(nblk)], axis=0)
        blk = lambda idx: pltpu.bitcast(bank_ref[idx], bf16)
        top = jnp.concatenate([blk(k0 + 1), blk(k0 + 2)], axis=1)
        bot = jnp.concatenate([blk(k0), blk(k0 + 1)], axis=1)
        w = jnp.concatenate([top, bot], axis=0)
        part = jnp.dot(lhs.astype(bf16), w, preferred_element_type=f32)
        acc = part if acc is None else acc + part
        yield
    out.append(jnp.concatenate([acc[i * nb:(i + 1) * nb] for i in range(nblk)], axis=1))


def _interleave(main, side, ratio):
    main, side = list(main), list(side)
    while main or side:
        main = [g for g in main if next(g, StopIteration) is not StopIteration]
        for _ in range(ratio if main else 1 << 30):
            side = [g for g in side if next(g, StopIteration) is not StopIteration]
            if not side:
                break


def _hyena_kernel(d_ref, vb_ref, x1_ref, x2_ref, h0_ref, h1_ref, z_ref,
                  u_scr, a_scr, b_scr, o_scr, bank00, bank01, bank10, bank11, upad0, upad1, *, nb, seq, cg, nchan):
    gidx = pl.program_id(0)
    bank_ref = ((bank00, bank01), (bank10, bank11))
    upad_ref = (upad0, upad1)
    u_scr[...] = jnp.swapaxes(vb_ref[...], 0, 1)
    a_scr[...] = jnp.swapaxes(x1_ref[...], 0, 1)
    b_scr[...] = jnp.swapaxes(x2_ref[...], 0, 1)

    def banks(c):
        gens = []
        if c < cg:
            gens.append(_bank_steps(h0_ref[c:c + 1, :], bank_ref[c % 2][0], seq))
        if 1 <= c <= cg:
            gens.append(_bank_steps(h1_ref[c - 1:c, :], bank_ref[c % 2][1], seq))
        return gens

    ratio = -(-(seq // LANES + 1) // (seq // TOEP + 1))
    _interleave([], banks(0), ratio)
    z_prev = None
    for c in range(cg + 1):
        convs, first, second = [], [], []
        if c < cg:
            u = u_scr[c]
            convs.append(_conv_steps(u, bank_ref[c % 2][0], upad_ref[0], first, nb=nb, seq=seq))
        if c >= 1:
            convs.append(_conv_steps(z_prev, bank_ref[c % 2][1], upad_ref[1], second, nb=nb, seq=seq))
        _interleave(convs, banks(c + 1), ratio)
        if c >= 1:
            o_scr[c - 1] = b_scr[c - 1] * (second[0] + z_prev * d_ref[nchan + gidx * cg + c - 1])
        if c < cg:
            z_prev = a_scr[c] * (first[0] + u * d_ref[gidx * cg + c])
    z_ref[...] = jnp.swapaxes(o_scr[...], 0, 1).astype(bf16)


def _hyena(xbt, filt_t, d_flat, *, nb, seq):
    nchan = xbt.shape[1] // 3
    cg = 2 * HALO
    ng = nchan // cg
    kern = functools.partial(_hyena_kernel, nb=nb, seq=seq, cg=cg, nchan=nchan)
    act = lambda off: pl.BlockSpec((nb, cg, seq), lambda i: (0, off + i, 0))
    wseq = seq + LANES
    return pl.pallas_call(
        kern,
        out_shape=jax.ShapeDtypeStruct((nb, nchan, seq), bf16),
        grid=(ng,),
        in_specs=[pl.BlockSpec(memory_space=pltpu.SMEM),
                  act(0), act(ng), act(2 * ng),
                  pl.BlockSpec((cg, wseq), lambda i: (i, 0)),
                  pl.BlockSpec((cg, wseq), lambda i: (ng + i, 0))],
        out_specs=pl.BlockSpec((nb, cg, seq), lambda i: (0, i, 0)),
        scratch_shapes=[pltpu.VMEM((cg, nb, seq), f32), pltpu.VMEM((cg, nb, seq), f32),
                        pltpu.VMEM((cg, nb, seq), f32), pltpu.VMEM((cg, nb, seq), f32),
                        *[pltpu.VMEM((seq // LANES + 3, LANES // 2, LANES), jnp.int32) for _ in range(4)],
                        pltpu.VMEM((nb, 2 * seq), f32), pltpu.VMEM((nb, 2 * seq), f32)],
        compiler_params=_cparams("arbitrary"),
        name="hyena_conv",
    )(d_flat, xbt, xbt, xbt, filt_t, filt_t)


def _rope128(x, cos, sin):
    lane = lax.broadcasted_iota(jnp.int32, x.shape, 1)
    swapped = jnp.where((lane & 16) == 0, pltpu.roll(x, LANES - 16, 1), pltpu.roll(x, 16, 1))
    return x * cos + swapped * sin


def _expand_kv(ckv, kr_pad, wuk_ref, wuv_ref, kg_ref, cos, sin, k_ref, v_ref, rows=slice(None)):
    cb = ckv.astype(bf16)
    kn = jnp.dot(cb, wuk_ref[...], preferred_element_type=f32)
    v_ref[0, :, rows] = jnp.dot(wuv_ref[...], ckv.T.astype(bf16), preferred_element_type=f32).astype(bf16)
    kg = kg_ref[...]
    g_nope, g_rope = kg[:, 0:NOPE], kg[:, NOPE:2 * NOPE]
    kr_ss = jnp.sum(kr_pad * kr_pad, axis=-1, keepdims=True)
    krg = kr_pad * g_rope
    if cos is not None:
        krg = _rope128(krg, cos, sin)
    for hd in range(HEADS):
        kh = kn[:, hd * NOPE:(hd + 1) * NOPE]
        r = lax.rsqrt((jnp.sum(kh * kh, axis=-1, keepdims=True) + kr_ss) / QK + EPS)
        k_ref[rows, hd * HEAD_PAD:hd * HEAD_PAD + NOPE] = (kh * r * g_nope).astype(bf16)
        k_ref[rows, hd * HEAD_PAD + NOPE:(hd + 1) * HEAD_PAD] = (krg * r).astype(bf16)


def _mla_front_kernel(*refs, rope, emit_cache, sub):
    (x_ref, mod_ref, g_ref, wdq_ref, qn_ref, wuq_ref, qg_ref, wdkv_ref, kvn_ref, wuk_ref, wuv_ref, kg_ref) = refs[:12]
    refs = refs[12:]
    if rope:
        cos_ref, sin_ref, qcos_ref, qsin_ref = refs[:4]
        refs = refs[4:]
    q_ref, k_ref, v_ref = refs[:3]
    mod = mod_ref[0]
    c_exp = (1.0 / math.sqrt(QK)) * math.log2(math.e)
    g16 = ROPE // 4
    tm = x_ref.shape[0]
    for r0 in range(0, tm, sub):
        rows = slice(r0, r0 + sub)
        cos = sin = None
        if rope:
            cos, sin, qcos, qsin = cos_ref[rows, :], sin_ref[rows, :], qcos_ref[:, rows], qsin_ref[:, rows]
        h = _norm_mod(x_ref[rows, :], g_ref[...], mod[1:2], mod[0:1]).astype(bf16)
        ql = jnp.dot(h, wdq_ref[...], preferred_element_type=f32)
        ql = ql * lax.rsqrt(jnp.mean(ql * ql, axis=-1, keepdims=True) + EPS) * qn_ref[...]
        qt = jnp.dot(wuq_ref[...], ql.T.astype(bf16), preferred_element_type=f32)
        qg = qg_ref[:, rows]
        for hd in range(HEADS):
            qh = qt[hd * HEAD_PAD:(hd + 1) * HEAD_PAD]
            ssq = jnp.sum(qh * qh, axis=0, keepdims=True)
            qh = qh * (lax.rsqrt(ssq / QK + EPS) * c_exp) * qg
            if rope:
                x = qh[NOPE:QK]
                swapped = jnp.concatenate([x[g16:2 * g16], x[0:g16], x[3 * g16:], x[2 * g16:3 * g16]], axis=0)
                qh = jnp.concatenate([qh[0:NOPE], x * qcos + swapped * qsin, qh[QK:]], axis=0)
            q_ref[0, hd * HEAD_PAD:(hd + 1) * HEAD_PAD, rows] = qh.astype(bf16)
        dkv = jnp.dot(h, wdkv_ref[...], preferred_element_type=f32)
        c_raw = dkv[:, 0:KV_RANK]
        ckv = c_raw * lax.rsqrt(jnp.mean(c_raw * c_raw, axis=-1, keepdims=True) + EPS) * kvn_ref[...]
        kr_pad = dkv[:, KV_RANK:]
        if emit_cache:
            ckv_ref, kr_ref = refs[3:5]
            ckv_ref[rows, :] = ckv
            kr_ref[rows, :] = kr_pad[:, 0:ROPE]
        _expand_kv(ckv, kr_pad, wuk_ref, wuv_ref, kg_ref, cos, sin, k_ref, v_ref, rows)


def _mla_front(x, mod, g, w, tabs, *, nb, seq, tm, emit_cache):
    t_rows, d = x.shape
    tpb = seq // tm
    rope = tabs is not None
    mrows = seq if mod.shape[0] > 1 else t_rows
    kern = functools.partial(_mla_front_kernel, rope=rope, emit_cache=emit_cache, sub=min(TOEP, tm))
    full = lambda a: pl.BlockSpec(a.shape, lambda i: (0,) * a.ndim)
    wnames = ("w_dq", "q_norm", "w_uq", "q_gain", "w_dkv", "kv_norm", "w_uk", "w_uv", "k_gain")
    w = dict(w, q_gain=jnp.broadcast_to(w["q_gain"].reshape(HEAD_PAD, 1), (HEAD_PAD, tm)))
    ins = [x, mod, g] + [w[n] for n in wnames]
    in_specs = [pl.BlockSpec((tm, d), lambda i: (i, 0)),
                pl.BlockSpec((1, 6, d), lambda i: ((i * tm) // mrows, 0, 0)),
                full(g)] + [full(w[n]) for n in wnames]
    if rope:
        ins += list(tabs)
        in_specs += [pl.BlockSpec((tm, LANES), lambda i: (i % tpb, 0))] * 2
        in_specs += [pl.BlockSpec((ROPE, tm), lambda i: (0, i % tpb))] * 2
    hp = HEADS * HEAD_PAD
    out_shape = [jax.ShapeDtypeStruct((nb, hp, seq), bf16), jax.ShapeDtypeStruct((t_rows, hp), bf16),
                 jax.ShapeDtypeStruct((nb, HEADS * V_DIM, seq), bf16)]
    out_specs = [pl.BlockSpec((1, hp, tm), lambda i: (i // tpb, 0, i % tpb)), pl.BlockSpec((tm, hp), lambda i: (i, 0)),
                 pl.BlockSpec((1, HEADS * V_DIM, tm), lambda i: (i // tpb, 0, i % tpb))]
    if emit_cache:
        out_shape += [jax.ShapeDtypeStruct((t_rows, KV_RANK), f32), jax.ShapeDtypeStruct((t_rows, ROPE), f32)]
        out_specs += [pl.BlockSpec((tm, KV_RANK), lambda i: (i, 0)), pl.BlockSpec((tm, ROPE), lambda i: (i, 0))]
    return pl.pallas_call(
        kern, out_shape=tuple(out_shape), grid=(t_rows // tm,), in_specs=in_specs, out_specs=tuple(out_specs),
        compiler_params=_cparams("arbitrary"), name="mla_front",
    )(*ins)


def _cache_expand_kernel(ckv_ref, kr_ref, wuk_ref, wuv_ref, kg_ref, k_ref, v_ref):
    _expand_kv(ckv_ref[...], kr_ref[...], wuk_ref, wuv_ref, kg_ref, None, None, k_ref, v_ref)


def _cache_expand(ckv, kr_pad, w, *, nb, seq):
    t_rows = ckv.shape[0]
    tm = seq
    full = lambda a: pl.BlockSpec(a.shape, lambda i: (0,) * a.ndim)
    hp = HEADS * HEAD_PAD
    return pl.pallas_call(
        _cache_expand_kernel,
        out_shape=(jax.ShapeDtypeStruct((t_rows, hp), bf16), jax.ShapeDtypeStruct((nb, HEADS * V_DIM, seq), bf16)),
        grid=(t_rows // tm,),
        in_specs=[pl.BlockSpec((tm, KV_RANK), lambda i: (i, 0)), pl.BlockSpec((tm, LANES), lambda i: (i, 0)),
                  full(w["w_uk"]), full(w["w_uv"]), full(w["k_gain"])],
        out_specs=(pl.BlockSpec((tm, hp), lambda i: (i, 0)),
                   pl.BlockSpec((1, HEADS * V_DIM, tm), lambda i: (i, 0, 0))),
        compiler_params=_cparams("arbitrary"), name="mla_cache_expand",
    )(ckv, kr_pad, w["w_uk"], w["w_uv"], w["k_gain"])


def _attn_kernel(*refs, cached, tk):
    if cached:
        q_ref, k_ref, kc_ref, vt_ref, vct_ref, o_ref, s_scr, m_scr = refs
        parts = ((k_ref, vt_ref), (kc_ref, vct_ref))
    else:
        q_ref, k_ref, vt_ref, o_ref, s_scr, m_scr = refs
        parts = ((k_ref, vt_ref),)

    @pl.when(pl.program_id(0) == 0)
    def _():
        s_scr[...] = jnp.zeros_like(s_scr)
        m_scr[...] = jnp.zeros_like(m_scr)

    q = q_ref[0]
    m_old = m_scr[...]
    m_new = acc = None
    base = 0
    for kr, vtr in parts:
        ck = min(tk, kr.shape[0])
        ones = jnp.ones((16, ck), bf16)
        for j in range(kr.shape[0] // ck):
            rows = slice(base + j * ck, base + (j + 1) * ck)
            p = jnp.exp2(s_scr[rows, :] - m_old).astype(bf16)
            lhs = jnp.concatenate([vtr[0, :, j * ck:(j + 1) * ck], ones], axis=0)
            part = jnp.dot(lhs, p, preferred_element_type=f32)
            acc = part if acc is None else acc + part
            s = jnp.dot(kr[j * ck:(j + 1) * ck, :], q, preferred_element_type=f32)
            s_scr[rows, :] = s
            cm = jnp.max(s, axis=0, keepdims=True)
            m_new = cm if m_new is None else jnp.maximum(m_new, cm)
        base += kr.shape[0]
    m_scr[...] = m_new
    o_ref[0] = (acc[0:V_DIM] / acc[V_DIM:V_DIM + 1]).astype(bf16)


def _attention(q, k, vt, kc, vct, *, nb, seq, tq):
    cached = kc is not None
    nq = seq // tq
    n_tiles = nb * HEADS * nq
    past = kc.shape[0] // nb if cached else 0
    tk = min(seq, TOEP)

    def cur(n):
        n = jnp.minimum(n, n_tiles - 1)
        return n // (HEADS * nq), (n // nq) % HEADS, n % nq

    def prev(n):
        n = jnp.maximum(n - 1, 0)
        return n // (HEADS * nq), (n // nq) % HEADS, n % nq

    def spec(shape, fn):
        return pl.BlockSpec(shape, fn)

    kern = functools.partial(_attn_kernel, cached=cached, tk=tk)
    q_spec = spec((1, HEAD_PAD, tq), lambda n: cur(n))
    k_spec = lambda rows: spec((rows, HEAD_PAD), lambda n: (cur(n)[0], cur(n)[1]))
    v_spec = lambda cols: spec((1, V_DIM, cols), lambda n: (prev(n)[0], prev(n)[1], 0))
    if cached:
        ins = [q, k, kc, vt, vct]
        in_specs = [q_spec, k_spec(seq), k_spec(past), v_spec(seq), v_spec(past)]
    else:
        ins = [q, k, vt]
        in_specs = [q_spec, k_spec(seq), v_spec(seq)]
    return pl.pallas_call(
        kern,
        out_shape=jax.ShapeDtypeStruct((nb, HEADS * V_DIM, seq), bf16),
        grid=(n_tiles + 1,),
        in_specs=in_specs,
        out_specs=pl.BlockSpec((1, V_DIM, tq), lambda n: prev(n)),
        scratch_shapes=[pltpu.VMEM((seq + past, tq), f32), pltpu.VMEM((1, tq), f32)],
        compiler_params=_cparams("arbitrary"),
        name="mla_attention",
    )(*ins)


def _attn_short_kernel(q_ref, k_ref, vt_ref, o_ref):
    seq = k_ref.shape[0]
    ones = jnp.ones((16, seq), bf16)
    for hd in range(HEADS):
        cols = slice(hd * HEAD_PAD, (hd + 1) * HEAD_PAD)
        s = jnp.dot(k_ref[:, cols], q_ref[0, cols, :], preferred_element_type=f32)
        p = jnp.exp2(s - jnp.max(s, axis=0, keepdims=True)).astype(bf16)
        rows = slice(hd * V_DIM, (hd + 1) * V_DIM)
        acc = jnp.dot(jnp.concatenate([vt_ref[0, rows, :], ones], axis=0), p, preferred_element_type=f32)
        o_ref[0, rows, :] = (acc[0:V_DIM] / acc[V_DIM:V_DIM + 1]).astype(bf16)


def _attention_short(q, k, vt, *, nb, seq):
    return pl.pallas_call(
        _attn_short_kernel,
        out_shape=jax.ShapeDtypeStruct((nb, HEADS * V_DIM, seq), bf16),
        grid=(nb,),
        in_specs=[pl.BlockSpec((1, HEADS * HEAD_PAD, seq), lambda b: (b, 0, 0)),
                  pl.BlockSpec((seq, HEADS * HEAD_PAD), lambda b: (b, 0)),
                  pl.BlockSpec((1, HEADS * V_DIM, seq), lambda b: (b, 0, 0))],
        out_specs=pl.BlockSpec((1, HEADS * V_DIM, seq), lambda b: (b, 0, 0)),
        compiler_params=_cparams("arbitrary"),
        name="mla_attention_short",
    )(q, k, vt)


ROW_HALO = 16


def _mix_ffn_kernel(*refs, tm, tpb, chunks, kinds):
    x_ref, xp_ref, xn_ref, mod_ref, g_ref = refs[:5]
    wup_ref, cw_ref, cb_ref, wd_ref, o_ref = refs[5 + 4 * len(kinds):]
    i = pl.program_id(0)
    mod = mod_ref[0]
    tn = (((0,), (0,)), ((), ()))
    proj = [None, None, None]
    for n, kind in enumerate(kinds):
        main, prev, nxt, w_ref = refs[5 + 4 * n:9 + 4 * n]
        w = w_ref[...]
        if kind == "rows":
            parts = [jnp.dot(main[...], w, preferred_element_type=f32),
                     jnp.dot(prev[...], w, preferred_element_type=f32)[ROW_HALO - HALO:ROW_HALO],
                     jnp.dot(nxt[...], w, preferred_element_type=f32)[0:HALO]]
        else:
            ops = [main[0], prev[0][:, LANES - HALO:LANES], nxt[0][:, 0:HALO]]
            parts = [lax.dot_general(o.astype(bf16), w, tn, preferred_element_type=f32) for o in ops]
        proj = [p if q is None else q + p for q, p in zip(proj, parts)]
    g1 = mod[2:3]
    x1 = x_ref[...] + g1 * proj[0]
    h = _halo_ext(x1, xp_ref[...] + g1 * proj[1], xn_ref[...] + g1 * proj[2], g_ref[...], mod[4:5], mod[3:4],
                  (i % tpb) == 0, (i % tpb) == tpb - 1)
    dff = wd_ref.shape[1]
    acc = None
    for c0, c1 in chunks:
        halves = []
        for off in (c0, dff + c0):
            cols = slice(off, off + c1 - c0)
            y = jnp.dot(h, wup_ref[0, :, cols], preferred_element_type=f32)
            halves.append(_dwconv3(y, cw_ref[:, cols], cb_ref[:, cols], tm))
        gate, up = halves
        act = ((gate * jax.nn.sigmoid(gate)) * up).astype(bf16)
        part = jnp.dot(act, wd_ref[0, c0:c1, :], preferred_element_type=f32)
        acc = part if acc is None else acc + part
    o_ref[...] = x1 + mod[5:6] * acc


def _mix_ffn(x, mod, g, mixer_ops, w_up, conv_w, conv_b, w_down, layer, *, seq, tm):
    t_rows, d = x.shape
    dff = w_down.shape[1]
    step = 6 * TOEP
    chunks = tuple((c, min(c + step, dff)) for c in range(0, dff, step))
    tpb = seq // tm
    mrows = seq if mod.shape[0] > 1 else t_rows
    kinds = tuple(k for k, _, _ in mixer_ops)
    kern = functools.partial(_mix_ffn_kernel, tm=tm, tpb=tpb, chunks=chunks, kinds=kinds)
    resident = lambda a: pl.BlockSpec(a.shape, lambda i: (0,) * a.ndim, pipeline_mode=pl.Buffered(1))
    layer_block = lambda a: pl.BlockSpec((1,) + a.shape[1:], lambda i: (layer,) + (0,) * (a.ndim - 1),
                                         pipeline_mode=pl.Buffered(1))
    ins, in_specs = [], []
    for kind, op, w in mixer_ops:
        c = op.shape[1]
        if kind == "rows":
            r, nblk = tm // ROW_HALO, t_rows // ROW_HALO
            in_specs += [pl.BlockSpec((tm, c), lambda i: (i, 0)),
                         pl.BlockSpec((ROW_HALO, c), lambda i, r=r: (jnp.maximum(i * r - 1, 0), 0)),
                         pl.BlockSpec((ROW_HALO, c), lambda i, r=r, nblk=nblk: (jnp.minimum((i + 1) * r, nblk - 1), 0))]
        else:
            r, nblk = tm // LANES, seq // LANES
            in_specs += [pl.BlockSpec((1, c, tm), lambda i: (i // tpb, 0, i % tpb)),
                         pl.BlockSpec((1, c, LANES), lambda i, r=r: (i // tpb, 0, jnp.maximum((i % tpb) * r - 1, 0))),
                         pl.BlockSpec((1, c, LANES),
                                      lambda i, r=r, nblk=nblk: (i // tpb, 0, jnp.minimum((i % tpb + 1) * r, nblk - 1)))]
        in_specs.append(resident(w))
        ins += [op, op, op, w]
    return pl.pallas_call(
        kern,
        out_shape=jax.ShapeDtypeStruct((t_rows, d), f32),
        grid=(t_rows // tm,),
        in_specs=[*_row_specs(tm, d, t_rows),
                  pl.BlockSpec((1, 6, d), lambda i: ((i * tm) // mrows, 0, 0)),
                  resident(g), *in_specs,
                  layer_block(w_up), resident(conv_w), resident(conv_b), layer_block(w_down)],
        out_specs=pl.BlockSpec((tm, d), lambda i: (i, 0)),
        compiler_params=_cparams("arbitrary"),
        name="mix_ffn",
    )(x, x, x, mod, g, *ins, w_up, conv_w, conv_b, w_down)


def _rope_tables(seq):
    half = ROPE // 2
    rows = seq // GRID_W
    row = np.repeat(np.arange(rows), GRID_W).astype(np.float32)
    col = np.tile(np.arange(GRID_W), rows).astype(np.float32)
    inv = (1.0 / (ROPE_BASE ** (np.arange(0, half, 2, dtype=np.float32) / half))).astype(np.float32)
    ar = (row[:, None] * inv[None]).astype(np.float64)
    ac = (col[:, None] * inv[None]).astype(np.float64)
    pad = np.zeros((seq, LANES - ROPE))
    cos64 = np.concatenate([np.cos(ar), np.cos(ar), np.cos(ac), np.cos(ac)], axis=1)
    sin64 = np.concatenate([-np.sin(ar), np.sin(ar), -np.sin(ac), np.sin(ac)], axis=1)
    as_f32 = lambda a: jnp.asarray(a.astype(np.float32))
    return (as_f32(np.concatenate([cos64, pad], axis=1)), as_f32(np.concatenate([sin64, pad], axis=1)),
            as_f32(cos64.T), as_f32(sin64.T))


def _pad_heads(a, width):
    lead = a.shape[:-1]
    a = a.reshape(*lead, HEADS, -1)
    a = jnp.pad(a, [(0, 0)] * len(lead) + [(0, 0), (0, width - a.shape[-1])])
    return a.reshape(*lead, HEADS * width)


def _mla_weights(j, mla_w_dq, mla_q_norm, mla_w_uq, mla_w_dkv, mla_kv_norm, mla_w_ukv, mla_q_head_norm,
                 mla_k_head_norm):
    ukv = mla_w_ukv[j].reshape(KV_RANK, HEADS, NOPE + V_DIM)
    kg = mla_k_head_norm[j]
    return {
        "w_dq": mla_w_dq[j].astype(bf16),
        "q_norm": mla_q_norm[j].reshape(1, -1),
        "w_uq": _pad_heads(mla_w_uq[j], HEAD_PAD).T.astype(bf16),
        "q_gain": jnp.pad(mla_q_head_norm[j], (0, HEAD_PAD - QK)),
        "w_dkv": jnp.pad(mla_w_dkv[j], ((0, 0), (0, LANES - ROPE))).astype(bf16),
        "kv_norm": mla_kv_norm[j].reshape(1, -1),
        "w_uk": ukv[:, :, :NOPE].reshape(KV_RANK, HEADS * NOPE).astype(bf16),
        "w_uv": ukv[:, :, NOPE:].reshape(KV_RANK, HEADS * V_DIM).T.astype(bf16),
        "k_gain": jnp.pad(kg, (0, 2 * NOPE - QK)).reshape(1, 2 * NOPE),
    }


def kernel(x_prompt, x_sample, cache_ckv, cache_krope, c, c_ctx, ada_w, ada_b, norm_g, mix_w_in, sgu_w, sgu_b, hy_conv_w, hy_conv_b, hy_f_w1, hy_f_b1, hy_f_w2, hy_f_b2, hy_f_w3, hy_f_freq, hy_decay, hy_d, mix_w_out, mla_w_dq, mla_q_norm, mla_w_uq, mla_w_dkv, mla_kv_norm, mla_w_ukv, mla_q_head_norm, mla_k_head_norm, mla_w_o, ffn_w_up, ffn_conv_w, ffn_conv_b, ffn_w_down):
    depth = ada_w.shape[0]
    d = x_prompt.shape[-1]
    nbp, seqp, _ = x_prompt.shape
    nbs, seqs, _ = x_sample.shape
    a_width = A_GROUPS * sgu_w.shape[-1]

    cond = jnp.zeros((16, d), f32).at[0].set(c_ctx).at[1:1 + nbs].set(c)
    mods = _ada(cond, ada_w, ada_b).reshape(depth, 16, 6, d)

    n_even = mix_w_in.shape[0]
    w_in = [mix_w_in[i].astype(bf16) for i in range(n_even)]
    sgu_wb = sgu_w.astype(bf16)
    sgu_bb = jnp.broadcast_to(sgu_b[..., None], sgu_b.shape + (sgu_w.shape[-1],))
    w_out_a = [mix_w_out[i, :a_width].astype(bf16) for i in range(n_even)]
    w_out_z = [mix_w_out[i, a_width:].astype(bf16) for i in range(n_even)]
    w_up = ffn_w_up.astype(bf16)
    w_down = ffn_w_down.astype(bf16)
    w_o = [mla_w_o[j].astype(bf16) for j in range(depth // 2)]
    mla_w = [_mla_weights(j, mla_w_dq, mla_q_norm, mla_w_uq, mla_w_dkv, mla_kv_norm, mla_w_ukv,
                          mla_q_head_norm, mla_k_head_norm) for j in range(depth // 2)]

    def trunk(x3, mod_all, cache):
        nb, seq, _ = x3.shape
        x = x3.reshape(nb * seq, d)
        tm = min(seq, 512)
        latent = cache is not None
        tabs = _rope_tables(seq) if latent else None
        new_ckv, new_kr = [], []
        for l in range(depth):
            mod = mod_all[l]
            g1 = norm_g[l, 0].reshape(1, d)
            g2 = norm_g[l, 1].reshape(1, d)
            if l % 2 == 0:
                i = l // 2
                a, xbt = _even_front(x, mod, g1, w_in[i], sgu_wb[i], sgu_bb[i], hy_conv_w[i],
                                     hy_conv_b[i].reshape(1, -1), nb=nb, seq=seq, tm=tm)
                filt = _hyena_filters_t(seq, hy_f_w1[i], hy_f_b1[i], hy_f_w2[i], hy_f_b2[i], hy_f_w3[i],
                                        hy_f_freq[i], hy_decay[i])
                zt = _hyena(xbt, filt, hy_d[i].reshape(-1), nb=nb, seq=seq)
                mixer_ops = [("rows", a, w_out_a[i]), ("cols", zt, w_out_z[i])]
            else:
                j = l // 2
                outs = _mla_front(x, mod, g1, mla_w[j], tabs, nb=nb, seq=seq, tm=tm, emit_cache=not latent)
                q, k, v = outs[:3]
                if latent:
                    past = cache[0].shape[2]
                    ckv_c = cache[0][:, j].reshape(nb * past, KV_RANK)
                    kr_c = jnp.pad(cache[1][:, j].reshape(nb * past, ROPE), ((0, 0), (0, LANES - ROPE)))
                    kc, vc = _cache_expand(ckv_c, kr_c, mla_w[j], nb=nb, seq=past)
                else:
                    kc = vc = None
                    new_ckv.append(outs[3].reshape(nb, seq, KV_RANK))
                    new_kr.append(outs[4].reshape(nb, seq, ROPE))
                if kc is None and seq <= TOEP:
                    at = _attention_short(q, k, v, nb=nb, seq=seq)
                else:
                    at = _attention(q, k, v, kc, vc, nb=nb, seq=seq, tq=min(seq, 1024))
                mixer_ops = [("cols", at, w_o[j])]
            x = _mix_ffn(x, mod, g2, mixer_ops, w_up, ffn_conv_w[l], ffn_conv_b[l].reshape(1, -1), w_down, l,
                         seq=seq, tm=tm)
        return x.reshape(nb, seq, d), new_ckv, new_kr

    y_prompt, ckv_list, kr_list = trunk(x_prompt, mods[:, 0:1], None)
    y_sample, _, _ = trunk(x_sample, mods[:, 1:1 + nbs], (cache_ckv, cache_krope))
    return (y_prompt, y_sample, jnp.stack(ckv_list, axis=1), jnp.stack(kr_list, axis=1))
```
